```python
import math
import jax
import jax.numpy as jnp
from jax import lax
import numpy as np

D_MODEL = 1024
BATCH = 4
SEQ = 8192
DEPTH = 2

GRID_W = 64
CTX_LEN = 256
Q_BLOCK = 128
NORM_EPS = 1e-6

ATTN_HEADS = 8
ATTN_KV_HEADS = 2
ATTN_HEAD_DIM = 64
ATTN_AXIS_FREQS = ATTN_HEAD_DIM // 4
ROPE_THETA = 10000.0

SSD_HEADS = 8
SSD_HEAD_DIM = 64
SSD_D_INNER = SSD_HEADS * SSD_HEAD_DIM
SSD_GROUPS = 2
SSD_STATE = 128
SSD_CONV_K = 3
SSD_CONV_DIM = SSD_D_INNER + 2 * SSD_GROUPS * SSD_STATE
SSD_CHUNK = 128

RET_HEADS = 4
RET_DK = 128
RET_DV = 128
RET_CHUNK = 128

N_BRANCH = 3
BRANCH_W = 512
MLP_HIDDEN = 4 * D_MODEL

IN_SPLITS = (ATTN_HEADS * ATTN_HEAD_DIM, ATTN_KV_HEADS * ATTN_HEAD_DIM, ATTN_KV_HEADS * ATTN_HEAD_DIM,
             SSD_D_INNER, SSD_CONV_DIM, 2 * SSD_HEADS,
             RET_HEADS * RET_DK, RET_HEADS * RET_DK, RET_HEADS * RET_DV, RET_HEADS * RET_DV,
             N_BRANCH * D_MODEL)
IN_DIM = sum(IN_SPLITS)

kernel_name = 'hybrid_attn_ssd_retention_prefix_dit'


def rms_norm(x, w):
    xf = x.astype(jnp.float32)
    y = xf * lax.rsqrt(jnp.mean(xf * xf, axis=-1, keepdims=True) + NORM_EPS)
    return (y * w.astype(jnp.float32)).astype(x.dtype)


def modulate(x, shift, scale):
    return x * (1 + scale[:, None, :]) + shift[:, None, :]


def split_cols(p):
    out = []
    off = 0
    for size in IN_SPLITS:
        out.append(p[..., off:off + size])
        off += size
    return out


def flip(t):
    return jnp.flip(t, axis=1)


def rope_apply(x, cos, sin):
    half = x.shape[-1] // 2
    x1 = x[..., :half]
    x2 = x[..., half:]
    cs = cos[:, None, :].astype(x.dtype)
    sn = sin[:, None, :].astype(x.dtype)
    return jnp.concatenate([x1 * cs - x2 * sn, x1 * sn + x2 * cs], axis=-1)


def axial_angles(rows):
    row = jnp.repeat(jnp.arange(rows, dtype=jnp.float32), GRID_W)
    col = jnp.tile(jnp.arange(GRID_W, dtype=jnp.float32), rows)
    inv = ROPE_THETA ** (-jnp.arange(ATTN_AXIS_FREQS, dtype=jnp.float32) / ATTN_AXIS_FREQS)
    ang = jnp.concatenate([row[:, None] * inv, col[:, None] * inv], axis=-1)
    return jnp.cos(ang), jnp.sin(ang)


def seq_angles(start, n):
    pos = jnp.arange(n, dtype=jnp.float32) + start
    inv = ROPE_THETA ** (-jnp.linspace(0.0, 1.0, RET_DK // 2, dtype=jnp.float32))
    ang = pos[:, None] * inv
    return jnp.cos(ang), jnp.sin(ang)


def dwconv_centred(x, w, bias):
    y = lax.conv_general_dilated(
        x, w[:, None, :].astype(x.dtype), window_strides=(1,),
        padding=[(SSD_CONV_K // 2, SSD_CONV_K // 2)],
        dimension_numbers=('NWC', 'WIO', 'NWC'), feature_group_count=x.shape[-1])
    return y + bias.astype(x.dtype)


def gqa_attend(q, k, v):
    b, lq, h, hd = q.shape
    kvh = k.shape[2]
    grp = h // kvh
    qb = q.reshape(b, lq // Q_BLOCK, Q_BLOCK, kvh, grp, hd).transpose(1, 0, 2, 3, 4, 5)
    scale = hd ** -0.5

    def block(qblk):
        s = jnp.einsum('bqkgd,bskd->bkgqs', qblk, k, preferred_element_type=jnp.float32) * scale
        p = jax.nn.softmax(s, axis=-1).astype(v.dtype)
        return jnp.einsum('bkgqs,bskd->bqkgd', p, v)

    o = lax.map(block, qb)
    return o.transpose(1, 0, 2, 3, 4, 5).reshape(b, lq, h * hd)


def chunk_scan(init, states, decay):
    def step(s, inp):
        st, dc = inp
        return s * dc + st, s
    final, prev = lax.scan(step, init, (states, decay))
    return final, prev


def ssd_scan(xh, dt, a_neg, bm, cm, init, return_y):
    b, l, nh, hp = xh.shape
    ng, ns = bm.shape[2], bm.shape[3]
    r = nh // ng
    L = SSD_CHUNK
    nc = l // L
    dtf = dt.astype(jnp.float32)
    xd = (xh.astype(jnp.float32) * dtf[..., None]).reshape(b, nc, L, ng, r, hp)
    a = (dtf * a_neg).reshape(b, nc, L, ng, r).transpose(0, 3, 4, 1, 2)
    bc = bm.astype(jnp.float32).reshape(b, nc, L, ng, ns)
    a_cum = jnp.cumsum(a, axis=-1)
    decay_states = jnp.exp(a_cum[..., -1:] - a_cum)
    states = jnp.einsum('bclgn,bgrcl,bclgrp->cbgrpn', bc, decay_states, xd)
    chunk_decay = jnp.exp(a_cum[..., -1]).transpose(3, 0, 1, 2)[..., None, None]
    final, prev = chunk_scan(init, states, chunk_decay)
    if not return_y:
        return None, final
    cc = cm.astype(jnp.float32).reshape(b, nc, L, ng, ns)
    seg = a_cum[..., :, None] - a_cum[..., None, :]
    causal = jnp.tril(jnp.ones((L, L), dtype=bool))
    lmat = jnp.where(causal, jnp.exp(jnp.where(causal, seg, 0.0)), 0.0)
    cb = jnp.einsum('bclgn,bcsgn->bcgls', cc, bc)
    y_diag = jnp.einsum('bcgls,bgrcls,bcsgrp->bclgrp', cb, lmat, xd)
    y_off = jnp.einsum('bclgn,cbgrpn,bgrcl->bclgrp', cc, prev, jnp.exp(a_cum))
    return (y_diag + y_off).reshape(b, l, nh, hp), final


def retention_scan(q, k, v, lg, init, return_y):
    b, l, nh, dk = k.shape
    dv = v.shape[-1]
    L = RET_CHUNK
    nc = l // L
    kc = k.astype(jnp.float32).reshape(b, nc, L, nh, dk)
    vc = v.astype(jnp.float32).reshape(b, nc, L, nh, dv)
    pos = jnp.arange(L, dtype=jnp.float32)
    k_decay = jnp.exp((L - 1 - pos)[:, None] * lg)
    states = jnp.einsum('bcshk,sh,bcshv->cbhkv', kc, k_decay, vc)
    chunk_decay = jnp.broadcast_to(jnp.exp(L * lg)[None, None, :, None, None], (nc, 1, nh, 1, 1))
    final, prev = chunk_scan(init, states, chunk_decay)
    if not return_y:
        return None, final
    qc = q.astype(jnp.float32).reshape(b, nc, L, nh, dk)
    diff = pos[:, None] - pos[None, :]
    dmat = jnp.where(diff[None] >= 0, jnp.exp(jnp.maximum(diff, 0.0)[None] * lg[:, None, None]), 0.0)
    s = jnp.einsum('bclhk,bcshk->bchls', qc, kc) * dmat
    y_in = jnp.einsum('bchls,bcshv->bclhv', s, vc)
    q_decay = jnp.exp((pos + 1)[:, None] * lg)
    y_x = jnp.einsum('bclhk,cbhkv,lh->bclhv', qc, prev, q_decay)
    return (y_in + y_x).reshape(b, l, nh, dv), final


def attn_q(aq, q_norm, rope):
    b, l, _ = aq.shape
    q = rms_norm(aq.reshape(b, l, ATTN_HEADS, ATTN_HEAD_DIM), q_norm)
    return q if rope is None else rope_apply(q, *rope)


def attn_kv(ak, av, k_norm, rope):
    b, l, _ = ak.shape
    k = rms_norm(ak.reshape(b, l, ATTN_KV_HEADS, ATTN_HEAD_DIM), k_norm)
    if rope is not None:
        k = rope_apply(k, *rope)
    return k, av.reshape(b, l, ATTN_KV_HEADS, ATTN_HEAD_DIM)


def ssd_inputs(xbc_raw, dt_raw, conv_w, conv_b, dt_bias):
    b, l, _ = xbc_raw.shape
    gn = SSD_GROUPS * SSD_STATE
    xbc = jax.nn.silu(dwconv_centred(xbc_raw, conv_w, conv_b))
    xs = xbc[..., :SSD_D_INNER].reshape(b, l, SSD_HEADS, SSD_HEAD_DIM)
    bm = xbc[..., SSD_D_INNER:SSD_D_INNER + gn].reshape(b, l, SSD_GROUPS, SSD_STATE)
    cm = xbc[..., SSD_D_INNER + gn:].reshape(b, l, SSD_GROUPS, SSD_STATE)
    dt = jax.nn.softplus(dt_raw.astype(jnp.float32).reshape(b, l, 2, SSD_HEADS) + dt_bias.astype(jnp.float32))
    return xs, bm, cm, dt[:, :, 0], dt[:, :, 1]


def ssd_finish(y, xh, z, d_skip, norm_w):
    b, l, nh, hp = xh.shape
    y = y + d_skip.astype(jnp.float32)[:, None] * xh.astype(jnp.float32)
    y = y.reshape(b, l, nh * hp) * jax.nn.silu(z.astype(jnp.float32))
    return rms_norm(y, norm_w).astype(z.dtype)


def ret_q(rq, rope):
    b, l, _ = rq.shape
    return rope_apply(rq.reshape(b, l, RET_HEADS, RET_DK), *rope)


def ret_kv(rk, rv, rope):
    b, l, _ = rk.shape
    k = rope_apply(rk.reshape(b, l, RET_HEADS, RET_DK), *rope) * (RET_DK ** -0.5)
    return k, rv.reshape(b, l, RET_HEADS, RET_DV)


def ret_finish(y, g, gn_w):
    b, l, nh, dv = y.shape
    mu = jnp.mean(y, axis=-1, keepdims=True)
    yc = y - mu
    var = jnp.mean(yc * yc, axis=-1, keepdims=True)
    yn = (yc * lax.rsqrt(var + NORM_EPS)).reshape(b, l, nh * dv) * gn_w.astype(jnp.float32)
    return (yn * jax.nn.silu(g.astype(jnp.float32))).astype(g.dtype)


def merge_branches(br_attn, br_ssd, br_ret, gate_logits, w_branch, w_out):
    b, l, _ = gate_logits.shape
    gates = jax.nn.sigmoid(gate_logits.reshape(b, l, N_BRANCH, D_MODEL))
    merged = (gates[:, :, 0] * (br_attn @ w_branch[0])
              + gates[:, :, 1] * (br_ssd @ w_branch[1])
              + gates[:, :, 2] * (br_ret @ w_branch[2]))
    return merged @ w_out


def sq_relu_mlp(x, w1, w2):
    h = jax.nn.relu(x @ w1)
    return (h * h) @ w2


def hybrid_mixer(u_lat, u_ctx, w_in, q_norm, k_norm, conv_w, conv_b, dt_bias, a_log, d_skip,
                 ssd_norm_w, ret_log_decay, ret_gn_w, w_branch, w_out,
                 rope_lat, ret_rope_ctx, ret_rope_lat, need_ctx):
    b = u_lat.shape[0]
    pl = split_cols(u_lat @ w_in)
    pc = split_cols(u_ctx @ w_in)

    k_c, v_c = attn_kv(pc[1], pc[2], k_norm, None)
    k_l, v_l = attn_kv(pl[1], pl[2], k_norm, rope_lat)
    q_l = attn_q(pl[0], q_norm, rope_lat)
    attn_l = gqa_attend(q_l, jnp.concatenate([k_c, k_l], axis=1), jnp.concatenate([v_c, v_l], axis=1))

    x_c, b_c, c_c, dtf_c, dtb_c = ssd_inputs(pc[4], pc[5], conv_w, conv_b, dt_bias)
    x_l, b_l, c_l, dtf_l, dtb_l = ssd_inputs(pl[4], pl[5], conv_w, conv_b, dt_bias)
    a_neg = -jnp.exp(a_log.astype(jnp.float32))
    s_init = jnp.zeros((b, SSD_GROUPS, SSD_HEADS // SSD_GROUPS, SSD_HEAD_DIM, SSD_STATE), jnp.float32)
    yc_f, st_f = ssd_scan(x_c, dtf_c, a_neg[0], b_c, c_c, s_init, need_ctx)
    yc_b, st_b = ssd_scan(flip(x_c), flip(dtb_c), a_neg[1], flip(b_c), flip(c_c), s_init, need_ctx)
    yl_f, _ = ssd_scan(x_l, dtf_l, a_neg[0], b_l, c_l, st_f, True)
    yl_b, _ = ssd_scan(flip(x_l), flip(dtb_l), a_neg[1], flip(b_l), flip(c_l), st_b, True)
    ssd_l = ssd_finish(yl_f + flip(yl_b), x_l, pl[3], d_skip, ssd_norm_w)

    lg = -jnp.exp(ret_log_decay.astype(jnp.float32))
    r_init = jnp.zeros((b, RET_HEADS, RET_DK, RET_DV), jnp.float32)
    rk_c, rv_c = ret_kv(pc[7], pc[8], ret_rope_ctx)
    rq_c = ret_q(pc[6], ret_rope_ctx) if need_ctx else None
    rk_l, rv_l = ret_kv(pl[7], pl[8], ret_rope_lat)
    rq_l = ret_q(pl[6], ret_rope_lat)
    rc_f, rs_f = retention_scan(rq_c, rk_c, rv_c, lg[0], r_init, need_ctx)
    rc_b, rs_b = retention_scan(None if rq_c is None else flip(rq_c), flip(rk_c), flip(rv_c), lg[1], r_init, need_ctx)
    rl_f, _ = retention_scan(rq_l, rk_l, rv_l, lg[0], rs_f, True)
    rl_b, _ = retention_scan(flip(rq_l), flip(rk_l), flip(rv_l), lg[1], rs_b, True)
    ret_l = ret_finish(rl_f + flip(rl_b), pl[9], ret_gn_w)

    out_l = merge_branches(attn_l, ssd_l, ret_l, pl[10], w_branch, w_out)
    if not need_ctx:
        return out_l, None

    attn_c = gqa_attend(attn_q(pc[0], q_norm, None), k_c, v_c)
    ssd_c = ssd_finish(yc_f + flip(yc_b), x_c, pc[3], d_skip, ssd_norm_w)
    ret_c = ret_finish(rc_f + flip(rc_b), pc[9], ret_gn_w)
    out_c = merge_branches(attn_c, ssd_c, ret_c, pc[10], w_branch, w_out)
    return out_l, out_c


def setup_inputs(seed: int = 0) -> dict:
    key = jax.random.key(seed)
    ks = jax.random.split(key, 24)
    f32 = jnp.float32

    def nrm(k, shape, scale):
        return jax.random.normal(k, shape, f32) * scale

    dt = jnp.exp(jax.random.uniform(ks[13], (DEPTH, 2, SSD_HEADS), f32, math.log(1e-3), math.log(1e-1)))
    return {
        'x': nrm(ks[0], (BATCH, SEQ, D_MODEL), 1.0),
        'c': nrm(ks[1], (BATCH, D_MODEL), 1.0),
        'ctx': nrm(ks[2], (BATCH, CTX_LEN, D_MODEL), 1.0),
        'c_ctx': nrm(ks[3], (D_MODEL,), 1.0),
        'w_mod': nrm(ks[4], (DEPTH, D_MODEL, 6 * D_MODEL), 0.5 * D_MODEL ** -0.5),
        'b_mod': nrm(ks[5], (DEPTH, 6 * D_MODEL), 0.01),
        'norm1_w': 1.0 + nrm(ks[6], (DEPTH, D_MODEL), 0.02),
        'norm2_w': 1.0 + nrm(ks[7], (DEPTH, D_MODEL), 0.02),
        'w_in': nrm(ks[8], (DEPTH, D_MODEL, IN_DIM), D_MODEL ** -0.5),
        'attn_q_norm': 1.0 + nrm(ks[9], (DEPTH, ATTN_HEAD_DIM), 0.02),
        'attn_k_norm': 1.0 + nrm(ks[10], (DEPTH, ATTN_HEAD_DIM), 0.02),
        'ssd_conv_w': nrm(ks[11], (DEPTH, SSD_CONV_K, SSD_CONV_DIM), SSD_CONV_K ** -0.5),
        'ssd_conv_b': nrm(ks[12], (DEPTH, SSD_CONV_DIM), 0.01),
        'ssd_dt_bias': dt + jnp.log(-jnp.expm1(-dt)),
        'ssd_a_log': jnp.log(jax.random.uniform(ks[14], (DEPTH, 2, SSD_HEADS), f32, 1.0, 16.0)),
        'ssd_d': 1.0 + nrm(ks[15], (DEPTH, SSD_HEADS), 0.1),
        'ssd_norm_w': 1.0 + nrm(ks[16], (DEPTH, SSD_D_INNER), 0.02),
        'ret_log_decay': (-5.0 - jnp.arange(RET_HEADS, dtype=f32)) * math.log(2.0)
                         + nrm(ks[17], (DEPTH, 2, RET_HEADS), 0.1),
        'ret_gn_w': 1.0 + nrm(ks[18], (DEPTH, RET_HEADS * RET_DV), 0.02),
        'w_branch': nrm(ks[19], (DEPTH, N_BRANCH, BRANCH_W, D_MODEL), BRANCH_W ** -0.5),
        'w_out': nrm(ks[20], (DEPTH, D_MODEL, D_MODEL), D_MODEL ** -0.5),
        'w_mlp1': nrm(ks[21], (DEPTH, D_MODEL, MLP_HIDDEN), D_MODEL ** -0.5),
        'w_mlp2': nrm(ks[22], (DEPTH, MLP_HIDDEN, D_MODEL), MLP_HIDDEN ** -0.5),
        'final_norm_w': 1.0 + nrm(ks[23], (D_MODEL,), 0.02),
    }


def reference(x, c, ctx, c_ctx, w_mod, b_mod, norm1_w, norm2_w, w_in, attn_q_norm, attn_k_norm,
              ssd_conv_w, ssd_conv_b, ssd_dt_bias, ssd_a_log, ssd_d, ssd_norm_w, ret_log_decay,
              ret_gn_w, w_branch, w_out, w_mlp1, w_mlp2, final_norm_w):
    n = x.shape[1]
    m = ctx.shape[1]
    ROWS = n // GRID_W
    rope_lat = axial_angles(ROWS)
    ret_rope_ctx = seq_angles(0, m)
    ret_rope_lat = seq_angles(m, n)
    h_lat, h_ctx = x, ctx
    for layer in range(DEPTH):
        need_ctx = layer < DEPTH - 1
        mod_lat = jnp.split(jax.nn.silu(c) @ w_mod[layer] + b_mod[layer], 6, axis=-1)
        mod_ctx = jnp.split(jax.nn.silu(c_ctx)[None, :] @ w_mod[layer] + b_mod[layer], 6, axis=-1)
        u_lat = modulate(rms_norm(h_lat, norm1_w[layer]), mod_lat[0], mod_lat[1])
        u_ctx = modulate(rms_norm(h_ctx, norm1_w[layer]), mod_ctx[0], mod_ctx[1])
        mix_lat, mix_ctx = hybrid_mixer(
            u_lat, u_ctx, w_in[layer], attn_q_norm[layer], attn_k_norm[layer],
            ssd_conv_w[layer], ssd_conv_b[layer], ssd_dt_bias[layer], ssd_a_log[layer], ssd_d[layer],
            ssd_norm_w[layer], ret_log_decay[layer], ret_gn_w[layer], w_branch[layer], w_out[layer],
            rope_lat, ret_rope_ctx, ret_rope_lat, need_ctx)
        h_lat = h_lat + mod_lat[2][:, None, :] * mix_lat
        v_lat = modulate(rms_norm(h_lat, norm2_w[layer]), mod_lat[3], mod_lat[4])
        h_lat = h_lat + mod_lat[5][:, None, :] * sq_relu_mlp(v_lat, w_mlp1[layer], w_mlp2[layer])
        if need_ctx:
            h_ctx = h_ctx + mod_ctx[2][:, None, :] * mix_ctx
            v_ctx = modulate(rms_norm(h_ctx, norm2_w[layer]), mod_ctx[3], mod_ctx[4])
            h_ctx = h_ctx + mod_ctx[5][:, None, :] * sq_relu_mlp(v_ctx, w_mlp1[layer], w_mlp2[layer])
    return rms_norm(h_lat, final_norm_w)
```

```python
import functools
import math

import jax
import jax.numpy as jnp
from jax import lax
from jax.experimental import pallas as pl
from jax.experimental.pallas import tpu as pltpu

F32 = jnp.float32
BF16 = jnp.bfloat16

D_MODEL = 1024
GRID_W = 64
NORM_EPS = 1e-6
ROPE_THETA = 10000.0

ATTN_HEADS = 8
ATTN_KV_HEADS = 2
ATTN_GROUP = ATTN_HEADS // ATTN_KV_HEADS
ATTN_HEAD_DIM = 64
ATTN_HALF = ATTN_HEAD_DIM // 2
ATTN_AXIS_FREQS = ATTN_HEAD_DIM // 4
ATTN_Q_DIM = ATTN_HEADS * ATTN_HEAD_DIM
ATTN_KV_DIM = ATTN_KV_HEADS * ATTN_HEAD_DIM

SSD_HEADS = 8
SSD_HEAD_DIM = 64
SSD_D_INNER = SSD_HEADS * SSD_HEAD_DIM
SSD_GROUPS = 2
SSD_STATE = 128
SSD_CONV_K = 3
SSD_CONV_DIM = SSD_D_INNER + 2 * SSD_GROUPS * SSD_STATE
SSD_HEADS_PER_GROUP = SSD_HEADS // SSD_GROUPS

RET_HEADS = 4
RET_DK = 128
RET_DV = 128
RET_DIM = RET_HEADS * RET_DK

N_BRANCH = 3
BRANCH_W = 512
MLP_HIDDEN = 4 * D_MODEL

IN_SPLITS = (ATTN_Q_DIM, ATTN_KV_DIM, ATTN_KV_DIM, SSD_D_INNER, SSD_CONV_DIM, 2 * SSD_HEADS,
             RET_DIM, RET_DIM, RET_HEADS * RET_DV, RET_HEADS * RET_DV, N_BRANCH * D_MODEL)

V7X_LANES = 128
V7X_VMEM_BYTES = 64 * 1024 * 1024

TILE = 256
CHUNK = 128
DT_PAD = V7X_LANES
NEG_BIG = -1e30


def _vmem_limit(resident_bytes):
    return int(min(V7X_VMEM_BYTES - 8 * 1024 * 1024, 2 * resident_bytes + 16 * 1024 * 1024))


def _silu(x):
    return x * jax.nn.sigmoid(x)


def _rms_rows(x, w):
    return x * lax.rsqrt(jnp.mean(x * x, axis=-1, keepdims=True) + NORM_EPS) * w


def _dot(a, b):
    return jnp.dot(a, b, preferred_element_type=F32)


def _dot_nt(a, b):
    return lax.dot_general(a, b, (((1,), (1,)), ((), ())), preferred_element_type=F32)


def _dot_tn(a, b):
    return lax.dot_general(a, b, (((0,), (0,)), ((), ())), preferred_element_type=F32)


def _split3(x):
    hi = x.astype(BF16)
    r1 = x - hi.astype(F32)
    mid = r1.astype(BF16)
    lo = (r1 - mid.astype(F32)).astype(BF16)
    return hi, mid, lo


def _mod_kernel(c_ref, w_ref, b_ref, o_ref):
    s = _silu(c_ref[...])
    o_ref[...] = jnp.dot(s, w_ref[...], preferred_element_type=F32,
                         precision=lax.Precision.HIGHEST) + b_ref[...]


def _mod_call(cc, w_mod, b_mod):
    nblk = w_mod.shape[1] // D_MODEL
    out = pl.pallas_call(
        _mod_kernel,
        out_shape=jax.ShapeDtypeStruct((8, nblk * D_MODEL), F32),
        grid=(nblk,),
        in_specs=[pl.BlockSpec((8, D_MODEL), lambda j: (0, 0)),
                  pl.BlockSpec((D_MODEL, D_MODEL), lambda j: (0, j)),
                  pl.BlockSpec((1, D_MODEL), lambda j: (0, j))],
        out_specs=pl.BlockSpec((8, D_MODEL), lambda j: (0, j)),
        compiler_params=pltpu.CompilerParams(dimension_semantics=("arbitrary",)),
        name="mod",
    )(cc, w_mod, b_mod.reshape(1, -1))
    return out.reshape(8, nblk, D_MODEL)


def _inproj_kernel(h_ref, mod_ref, n1w_ref, wqkv_ref, qkw_ref, cos_ref, sin_ref, wxbc_ref, wdt_ref,
                   dtb_ref, wr_ref, cosr_ref, sinr_ref,
                   qt_ref, k_ref, vt_ref, xbc_ref, dt_ref, rq_ref, rk_ref, rv_ref):
    h = h_ref[0]
    u = _rms_rows(h, n1w_ref[...]) * (1.0 + mod_ref[0, 1:2, :]) + mod_ref[0, 0:1, :]
    ub = u.astype(BF16)

    qkv_t = _dot_nt(wqkv_ref[...], ub)
    cos = cos_ref[...]
    sin = sin_ref[...]
    k_rows = []
    for hd in range(ATTN_HEADS + ATTN_KV_HEADS):
        r0 = hd * ATTN_HEAD_DIM
        xh = qkv_t[r0:r0 + ATTN_HEAD_DIM]
        yh = xh * lax.rsqrt(jnp.mean(xh * xh, axis=0, keepdims=True) + NORM_EPS) * qkw_ref[r0:r0 + ATTN_HEAD_DIM, :]
        y1 = yh[:ATTN_HALF]
        y2 = yh[ATTN_HALF:]
        o1 = y1 * cos - y2 * sin
        o2 = y1 * sin + y2 * cos
        if hd < ATTN_HEADS:
            scale = ATTN_HEAD_DIM ** -0.5
            qt_ref[0, r0:r0 + ATTN_HALF, :] = (o1 * scale).astype(BF16)
            qt_ref[0, r0 + ATTN_HALF:r0 + ATTN_HEAD_DIM, :] = (o2 * scale).astype(BF16)
        else:
            k_rows += [o1, o2]
    k_t = jnp.concatenate(k_rows, axis=0)
    k_ref[0] = jnp.transpose(k_t).astype(BF16)
    vt_ref[0] = qkv_t[ATTN_Q_DIM + ATTN_KV_DIM:].astype(BF16)

    xbc_ref[0] = _dot(ub, wxbc_ref[...])
    dt_raw = _dot(ub, wdt_ref[...]) + dtb_ref[...]
    dt_ref[0] = jnp.maximum(dt_raw, 0.0) + jnp.log1p(jnp.exp(-jnp.abs(dt_raw)))

    r = _dot(ub, wr_ref[...])
    cosr = cosr_ref[...]
    sinr = sinr_ref[...]
    for hd in range(RET_HEADS):
        c0 = hd * RET_DK
        qh = r[:, c0:c0 + RET_DK]
        kh = r[:, RET_DIM + c0:RET_DIM + c0 + RET_DK]
        rq_ref[0, :, c0:c0 + RET_DK] = (qh * cosr + pltpu.roll(qh, RET_DK // 2, 1) * sinr).astype(BF16)
        rk_ref[0, :, c0:c0 + RET_DK] = ((kh * cosr + pltpu.roll(kh, RET_DK // 2, 1) * sinr)
                                        * (RET_DK ** -0.5)).astype(BF16)
    rv_ref[0] = r[:, 2 * RET_DIM:].astype(BF16)


def _inproj_call(h, mods, n1w, wts, tabs, nl):
    b, l, _ = h.shape
    nt = l // TILE
    wqkv, qkw, wxbc, wdt, dtb, wr = wts
    cos_a, sin_a, cos_r, sin_r = tabs
    const = lambda shape: pl.BlockSpec(shape, lambda bi, i: (0,) * len(shape))
    tok = lambda c: pl.BlockSpec((1, TILE, c), lambda bi, i: (bi, i, 0))
    tok_t = lambda r: pl.BlockSpec((1, r, TILE), lambda bi, i: (bi, 0, i))
    weights_bytes = 2 * (wqkv.size + wxbc.size + wdt.size + wr.size) + 4 * qkw.size
    return pl.pallas_call(
        _inproj_kernel,
        out_shape=(jax.ShapeDtypeStruct((b, ATTN_Q_DIM, l), BF16),
                   jax.ShapeDtypeStruct((b, l, ATTN_KV_DIM), BF16),
                   jax.ShapeDtypeStruct((b, ATTN_KV_DIM, l), BF16),
                   jax.ShapeDtypeStruct((b, l, SSD_CONV_DIM), F32),
                   jax.ShapeDtypeStruct((b, l, DT_PAD), F32),
                   jax.ShapeDtypeStruct((b, l, RET_DIM), BF16),
                   jax.ShapeDtypeStruct((b, l, RET_DIM), BF16),
                   jax.ShapeDtypeStruct((b, l, RET_DIM), BF16)),
        grid=(b, nt),
        in_specs=[tok(D_MODEL),
                  pl.BlockSpec((1, 6, D_MODEL), lambda bi, i: (jnp.where(i == nl, 4, bi), 0, 0)),
                  const((1, D_MODEL)),
                  const(wqkv.shape), const(qkw.shape),
                  pl.BlockSpec((ATTN_HALF, TILE), lambda bi, i: (0, i)),
                  pl.BlockSpec((ATTN_HALF, TILE), lambda bi, i: (0, i)),
                  const(wxbc.shape), const(wdt.shape), const(dtb.shape), const(wr.shape),
                  pl.BlockSpec((TILE, RET_DK), lambda bi, i: (i, 0)),
                  pl.BlockSpec((TILE, RET_DK), lambda bi, i: (i, 0))],
        out_specs=(tok_t(ATTN_Q_DIM), tok(ATTN_KV_DIM), tok_t(ATTN_KV_DIM), tok(SSD_CONV_DIM), tok(DT_PAD),
                   tok(RET_DIM), tok(RET_DIM), tok(RET_DIM)),
        compiler_params=pltpu.CompilerParams(dimension_semantics=("arbitrary", "arbitrary"),
                                             vmem_limit_bytes=_vmem_limit(weights_bytes)),
        name="inproj",
    )(h, mods, n1w, wqkv, qkw, cos_a, sin_a, wxbc, wdt, dtb, wr, cos_r, sin_r)


def _attn_kernel(qt_ref, k_ref, vt_ref, o_ref, qpad_sc, m_sc, l_sc, acc_sc, *, nl, nt):
    kv = pl.program_id(1)
    i = pl.program_id(2)

    row = lax.broadcasted_iota(jnp.int32, (2 * ATTN_HEAD_DIM, TILE), 0)
    keep = (row >= ATTN_HEAD_DIM).astype(jnp.int32) == kv
    for g in range(ATTN_GROUP):
        qg = qt_ref[0, g * ATTN_HEAD_DIM:(g + 1) * ATTN_HEAD_DIM, :]
        qq = jnp.concatenate([qg, qg], axis=0)
        qpad_sc[g] = jnp.where(keep, qq, jnp.zeros_like(qq))
    m_sc[...] = jnp.full(m_sc.shape, NEG_BIG, F32)
    l_sc[...] = jnp.zeros(l_sc.shape, F32)
    acc_sc[...] = jnp.zeros(acc_sc.shape, F32)

    def body(c, carry):
        off = pl.multiple_of(c * TILE, TILE)
        kc = k_ref[0, pl.ds(off, TILE), :]
        vc = vt_ref[0, :, pl.ds(off, TILE)]
        for g in range(ATTN_GROUP):
            s = _dot(kc, qpad_sc[g])
            m_old = m_sc[g]
            m_new = jnp.maximum(m_old, jnp.max(s, axis=0, keepdims=True))
            p = jnp.exp(s - m_new)
            alpha = jnp.exp(m_old - m_new)
            l_sc[g] = alpha * l_sc[g] + jnp.sum(p, axis=0, keepdims=True)
            acc_sc[g] = alpha * acc_sc[g] + _dot(vc, p.astype(BF16))
            m_sc[g] = m_new
        return carry

    lax.fori_loop(jnp.where(i == nl, nl, 0), nt, body, 0)

    for g in range(ATTN_GROUP):
        o_ref[0, g * ATTN_HEAD_DIM:(g + 1) * ATTN_HEAD_DIM, :] = (acc_sc[g] / l_sc[g]).astype(BF16)


def _attn_call(qt, k, vt, nl):
    b, _, l = qt.shape
    nt = l // TILE
    gq = ATTN_GROUP * ATTN_HEAD_DIM
    resident = 2 * (l * ATTN_KV_DIM + ATTN_HEAD_DIM * l)
    return pl.pallas_call(
        functools.partial(_attn_kernel, nl=nl, nt=nt),
        out_shape=jax.ShapeDtypeStruct((b, ATTN_Q_DIM, l), BF16),
        grid=(b, ATTN_KV_HEADS, nt),
        in_specs=[pl.BlockSpec((1, gq, TILE), lambda bi, kv, i: (bi, kv, i)),
                  pl.BlockSpec((1, l, ATTN_KV_DIM), lambda bi, kv, i: (bi, 0, 0)),
                  pl.BlockSpec((1, ATTN_HEAD_DIM, l), lambda bi, kv, i: (bi, kv, 0))],
        out_specs=pl.BlockSpec((1, gq, TILE), lambda bi, kv, i: (bi, kv, i)),
        scratch_shapes=[pltpu.VMEM((ATTN_GROUP, 2 * ATTN_HEAD_DIM, TILE), BF16),
                        pltpu.VMEM((ATTN_GROUP, 1, TILE), F32),
                        pltpu.VMEM((ATTN_GROUP, 1, TILE), F32),
                        pltpu.VMEM((ATTN_GROUP, ATTN_HEAD_DIM, TILE), F32)],
        compiler_params=pltpu.CompilerParams(dimension_semantics=("arbitrary",) * 3,
                                             vmem_limit_bytes=_vmem_limit(resident)),
        name="attn",
    )(qt, k, vt)


def _scan_tile(s, nl, reverse):
    return jnp.where(s == 0, nl, nl - s) if reverse else jnp.where(s == 0, nl, s - 1)


def _tri(reverse):
    r = lax.broadcasted_iota(jnp.int32, (CHUNK, CHUNK), 0)
    c = lax.broadcasted_iota(jnp.int32, (CHUNK, CHUNK), 1)
    return (c >= r) if reverse else (c <= r)


def _lane_bcast(x, c):
    return jnp.broadcast_to(x[:, c:c + 1], (x.shape[0], V7X_LANES))


def _ssd_kernel(x_ref, xp_ref, xn_ref, dt_ref, cw_ref, cb_ref, alog_ref, dskip_ref, y_ref, st_sc,
                *, nl, reverse):
    s = pl.program_id(1)
    t = _scan_tile(s, nl, reverse)

    @pl.when(s == 0)
    def _():
        st_sc[...] = jnp.zeros(st_sc.shape, F32)

    x = x_ref[0]
    row = lax.broadcasted_iota(jnp.int32, (TILE, SSD_CONV_DIM), 0)
    has_prev = jnp.logical_and(t != nl, t != 0)
    has_next = t < nl - 1
    prev_row = jnp.where(has_prev, xp_ref[0, 7:8, :], 0.0)
    next_row = jnp.where(has_next, xn_ref[0, 0:1, :], 0.0)
    x_m1 = jnp.where(row == 0, prev_row, pltpu.roll(x, 1, 0))
    x_p1 = jnp.where(row == TILE - 1, next_row, pltpu.roll(x, TILE - 1, 0))
    xs = _silu(cw_ref[0:1, :] * x_m1 + cw_ref[1:2, :] * x + cw_ref[2:3, :] * x_p1 + cb_ref[...])

    tri = _tri(reverse)
    tri_b = tri.astype(BF16)
    lane = lax.broadcasted_iota(jnp.int32, (CHUNK, V7X_LANES), 1)
    left = lane < SSD_HEAD_DIM
    col0 = SSD_HEADS if reverse else 0
    a_neg = -jnp.exp(alog_ref[...])
    last = 0 if reverse else CHUNK - 1

    chunks = range(TILE // CHUNK)
    for ci in (reversed(chunks) if reverse else chunks):
        r0 = ci * CHUNK
        xc = xs[r0:r0 + CHUNK]
        dtc = dt_ref[0, r0:r0 + CHUNK, :]
        a = dtc * a_neg
        hi, mid, lo = _split3(a)
        a_cum = _dot(tri_b, hi) + _dot(tri_b, mid) + _dot(tri_b, lo)
        a_cum_t = jnp.transpose(a_cum)
        for g in range(SSD_GROUPS):
            bm = xc[:, SSD_D_INNER + g * SSD_STATE:SSD_D_INNER + (g + 1) * SSD_STATE]
            cm = xc[:, SSD_D_INNER + (SSD_GROUPS + g) * SSD_STATE:SSD_D_INNER + (SSD_GROUPS + g + 1) * SSD_STATE]
            bmb = bm.astype(BF16)
            cmb = cm.astype(BF16)
            cb = _dot_nt(cmb, bmb)
            st_prev = st_sc[g]
            y_off = _dot(cmb, st_prev.astype(BF16))
            xdd_pairs = []
            tot = []
            for pr in range(SSD_HEADS_PER_GROUP // 2):
                h0 = g * SSD_HEADS_PER_GROUP + 2 * pr
                lanes0 = h0 * SSD_HEAD_DIM
                x2 = xc[:, lanes0:lanes0 + V7X_LANES]
                acol = [_lane_bcast(a_cum, col0 + h0 + j) for j in range(2)]
                dcol = [_lane_bcast(dtc, col0 + h0 + j) for j in range(2)]
                a2 = jnp.where(left, acol[0], acol[1])
                xd2 = x2 * jnp.where(left, dcol[0], dcol[1])
                xd2b = xd2.astype(BF16)
                y_pair = []
                for j in range(2):
                    arow = a_cum_t[col0 + h0 + j:col0 + h0 + j + 1, :]
                    lmat = jnp.where(tri, jnp.exp(jnp.where(tri, acol[j] - arow, 0.0)), 0.0)
                    y_pair.append(_dot((cb * lmat).astype(BF16), xd2b))
                y2 = jnp.where(left, y_pair[0], y_pair[1])
                y2 = y2 + jnp.exp(a2) * y_off[:, pr * V7X_LANES:(pr + 1) * V7X_LANES]
                if not reverse:
                    y2 = y2 + dskip_ref[:, lanes0:lanes0 + V7X_LANES] * x2
                y_ref[0, r0:r0 + CHUNK, lanes0:lanes0 + V7X_LANES] = y2
                a_tot = a2[last:last + 1, :]
                xdd_pairs.append((xd2 * jnp.exp(a_tot - a2)).astype(BF16))
                tot.append(a_tot)
            xdd = jnp.concatenate(xdd_pairs, axis=1)
            st_sc[g] = jnp.exp(jnp.concatenate(tot, axis=1)) * st_prev + _dot_tn(bmb, xdd)


def _ssd_call(xbc, dt, cw, cb, alog, dskip, nl, reverse):
    b, l, _ = xbc.shape
    nt = l // TILE
    rows8 = TILE // 8
    tile_of = lambda s: _scan_tile(s, nl, reverse)
    const = lambda shape: pl.BlockSpec(shape, lambda bi, s: (0,) * len(shape))
    return pl.pallas_call(
        functools.partial(_ssd_kernel, nl=nl, reverse=reverse),
        out_shape=jax.ShapeDtypeStruct((b, l, SSD_D_INNER), F32),
        grid=(b, nt),
        in_specs=[pl.BlockSpec((1, TILE, SSD_CONV_DIM), lambda bi, s: (bi, tile_of(s), 0)),
                  pl.BlockSpec((1, 8, SSD_CONV_DIM), lambda bi, s: (bi, jnp.maximum(tile_of(s) * rows8 - 1, 0), 0)),
                  pl.BlockSpec((1, 8, SSD_CONV_DIM),
                               lambda bi, s: (bi, jnp.minimum((tile_of(s) + 1) * rows8, nt * rows8 - 1), 0)),
                  pl.BlockSpec((1, TILE, DT_PAD), lambda bi, s: (bi, tile_of(s), 0)),
                  const(cw.shape), const(cb.shape), const(alog.shape), const(dskip.shape)],
        out_specs=pl.BlockSpec((1, TILE, SSD_D_INNER), lambda bi, s: (bi, tile_of(s), 0)),
        scratch_shapes=[pltpu.VMEM((SSD_GROUPS, SSD_STATE, SSD_HEADS_PER_GROUP * SSD_HEAD_DIM), F32)],
        compiler_params=pltpu.CompilerParams(dimension_semantics=("arbitrary", "arbitrary")),
        name="ssd_bwd" if reverse else "ssd_fwd",
    )(xbc, xbc, xbc, dt, cw, cb, alog, dskip)


def _ret_kernel(q_ref, k_ref, v_ref, ld_ref, y_ref, st_sc, *, nl, reverse):
    s = pl.program_id(1)

    @pl.when(s == 0)
    def _():
        st_sc[...] = jnp.zeros(st_sc.shape, F32)

    tri = _tri(reverse)
    r = lax.broadcasted_iota(jnp.int32, (CHUNK, CHUNK), 0)
    c = lax.broadcasted_iota(jnp.int32, (CHUNK, CHUNK), 1)
    dist = jnp.maximum((c - r) if reverse else (r - c), 0).astype(F32)
    pos = ((CHUNK - 1 - r) if reverse else r).astype(F32)
    col0 = RET_HEADS if reverse else 0

    chunks = range(TILE // CHUNK)
    for ci in (reversed(chunks) if reverse else chunks):
        r0 = ci * CHUNK
        for hd in range(RET_HEADS):
            c0 = hd * RET_DK
            lg = jnp.broadcast_to(-jnp.exp(ld_ref[0:1, col0 + hd:col0 + hd + 1]), (CHUNK, CHUNK))
            q = q_ref[0, r0:r0 + CHUNK, c0:c0 + RET_DK]
            k = k_ref[0, r0:r0 + CHUNK, c0:c0 + RET_DK]
            v = v_ref[0, r0:r0 + CHUNK, c0:c0 + RET_DV]
            dmat = jnp.where(tri, jnp.exp(dist * lg), 0.0)
            sc = _dot_nt(q, k) * dmat
            st_prev = st_sc[hd]
            y = _dot(sc.astype(BF16), v) + _dot(q, st_prev.astype(BF16)) * jnp.exp((pos + 1.0) * lg)
            y_ref[0, r0:r0 + CHUNK, c0:c0 + RET_DV] = y
            k_dec = (k.astype(F32) * jnp.exp((CHUNK - 1.0 - pos) * lg)).astype(BF16)
            st_sc[hd] = jnp.exp(CHUNK * lg) * st_prev + _dot_tn(k_dec, v)


def _ret_call(rq, rk, rv, ld, nl, reverse):
    b, l, _ = rq.shape
    nt = l // TILE
    tok = pl.BlockSpec((1, TILE, RET_DIM), lambda bi, s: (bi, _scan_tile(s, nl, reverse), 0))
    return pl.pallas_call(
        functools.partial(_ret_kernel, nl=nl, reverse=reverse),
        out_shape=jax.ShapeDtypeStruct((b, l, RET_DIM), F32),
        grid=(b, nt),
        in_specs=[tok, tok, tok, pl.BlockSpec(ld.shape, lambda bi, s: (0, 0))],
        out_specs=tok,
        scratch_shapes=[pltpu.VMEM((RET_HEADS, RET_DK, RET_DV), F32)],
        compiler_params=pltpu.CompilerParams(dimension_semantics=("arbitrary", "arbitrary")),
        name="ret_bwd" if reverse else "ret_fwd",
    )(rq, rk, rv, ld)


def _merge_kernel(h_ref, mod_ref, n1w_ref, ot_ref, sf_ref, sb_ref, rf_ref, rb_ref, wzg_ref, wgate_ref,
                  snw_ref, gnw_ref, wb_ref, wout_ref, o_ref):
    h = h_ref[0]
    u = _rms_rows(h, n1w_ref[...]) * (1.0 + mod_ref[0, 1:2, :]) + mod_ref[0, 0:1, :]
    ub = u.astype(BF16)
    zg = _dot(ub, wzg_ref[...])
    gates = jax.nn.sigmoid(_dot(ub, wgate_ref[...]))

    br_attn = _dot_tn(ot_ref[0], wb_ref[0])

    y = (sf_ref[0] + sb_ref[0]) * _silu(zg[:, :SSD_D_INNER])
    br_ssd = _dot(_rms_rows(y, snw_ref[...]).astype(BF16), wb_ref[1])

    yr = rf_ref[0] + rb_ref[0]
    heads = []
    for hd in range(RET_HEADS):
        yh = yr[:, hd * RET_DV:(hd + 1) * RET_DV]
        yc = yh - jnp.mean(yh, axis=-1, keepdims=True)
        heads.append(yc * lax.rsqrt(jnp.mean(yc * yc, axis=-1, keepdims=True) + NORM_EPS))
    yn = jnp.concatenate(heads, axis=1) * gnw_ref[...] * _silu(zg[:, SSD_D_INNER:])
    br_ret = _dot(yn.astype(BF16), wb_ref[2])

    merged = (gates[:, :D_MODEL] * br_attn + gates[:, D_MODEL:2 * D_MODEL] * br_ssd
              + gates[:, 2 * D_MODEL:] * br_ret)
    o_ref[0] = h + mod_ref[0, 2:3, :] * _dot(merged.astype(BF16), wout_ref[...])


def _merge_call(h, mods, n1w, ot, sf, sb, rf, rb, wts, nl):
    b, l, _ = h.shape
    nt = l // TILE
    wzg, wgate, snw, gnw, wb, wout = wts
    const = lambda shape: pl.BlockSpec(shape, lambda bi, i: (0,) * len(shape))
    tok = lambda c: pl.BlockSpec((1, TILE, c), lambda bi, i: (bi, i, 0))
    weights_bytes = 2 * (wzg.size + wgate.size + wb.size + wout.size)
    return pl.pallas_call(
        _merge_kernel,
        out_shape=jax.ShapeDtypeStruct((b, l, D_MODEL), F32),
        grid=(b, nt),
        in_specs=[tok(D_MODEL),
                  pl.BlockSpec((1, 6, D_MODEL), lambda bi, i: (jnp.where(i == nl, 4, bi), 0, 0)),
                  const((1, D_MODEL)),
                  pl.BlockSpec((1, ATTN_Q_DIM, TILE), lambda bi, i: (bi, 0, i)),
                  tok(SSD_D_INNER), tok(SSD_D_INNER), tok(RET_DIM), tok(RET_DIM),
                  const(wzg.shape), const(wgate.shape), const(snw.shape), const(gnw.shape),
                  const(wb.shape), const(wout.shape)],
        out_specs=tok(D_MODEL),
        compiler_params=pltpu.CompilerParams(dimension_semantics=("arbitrary", "arbitrary"),
                                             vmem_limit_bytes=_vmem_limit(weights_bytes)),
        name="merge",
    )(h, mods, n1w, ot, sf, sb, rf, rb, wzg, wgate, snw, gnw, wb, wout)


def _mlp_kernel(h_ref, mod_ref, n2w_ref, w1_ref, w2_ref, fw_ref, o_ref, *, final):
    h = h_ref[0]
    v = _rms_rows(h, n2w_ref[...]) * (1.0 + mod_ref[0, 4:5, :]) + mod_ref[0, 3:4, :]
    a = jnp.maximum(_dot(v.astype(BF16), w1_ref[...]), 0.0)
    out = h + mod_ref[0, 5:6, :] * _dot((a * a).astype(BF16), w2_ref[...])
    o_ref[0] = _rms_rows(out, fw_ref[...]) if final else out


def _mlp_call(h, mods, n2w, w1, w2, fw, nl, final):
    b, l, _ = h.shape
    nt = nl if final else l // TILE
    const = lambda shape: pl.BlockSpec(shape, lambda bi, i: (0,) * len(shape))
    tok = pl.BlockSpec((1, TILE, D_MODEL), lambda bi, i: (bi, i, 0))
    return pl.pallas_call(
        functools.partial(_mlp_kernel, final=final),
        out_shape=jax.ShapeDtypeStruct((b, nt * TILE, D_MODEL), F32),
        grid=(b, nt),
        in_specs=[tok,
                  pl.BlockSpec((1, 6, D_MODEL), lambda bi, i: (jnp.where(i == nl, 4, bi), 0, 0)),
                  const((1, D_MODEL)), const(w1.shape), const(w2.shape), const((1, D_MODEL))],
        out_specs=tok,
        compiler_params=pltpu.CompilerParams(dimension_semantics=("arbitrary", "arbitrary"),
                                             vmem_limit_bytes=_vmem_limit(2 * (w1.size + w2.size))),
        name="mlp_final" if final else "mlp",
    )(h, mods, n2w, w1, w2, fw)


def _rope_tables(n, m):
    rows = n // GRID_W
    row = jnp.repeat(jnp.arange(rows, dtype=F32), GRID_W)
    col = jnp.tile(jnp.arange(GRID_W, dtype=F32), rows)
    inv = ROPE_THETA ** (-jnp.arange(ATTN_AXIS_FREQS, dtype=F32) / ATTN_AXIS_FREQS)
    ang = jnp.concatenate([row[:, None] * inv, col[:, None] * inv], axis=-1)
    cos_a = jnp.concatenate([jnp.cos(ang), jnp.ones((m, ATTN_HALF), F32)], axis=0).T
    sin_a = jnp.concatenate([jnp.sin(ang), jnp.zeros((m, ATTN_HALF), F32)], axis=0).T
    pos = jnp.concatenate([jnp.arange(n, dtype=F32) + m, jnp.arange(m, dtype=F32)])
    inv_r = ROPE_THETA ** (-jnp.linspace(0.0, 1.0, RET_DK // 2, dtype=F32))
    ang_r = pos[:, None] * inv_r
    cos_r = jnp.concatenate([jnp.cos(ang_r), jnp.cos(ang_r)], axis=-1)
    sin_r = jnp.concatenate([-jnp.sin(ang_r), jnp.sin(ang_r)], axis=-1)
    return cos_a, sin_a, cos_r, sin_r


def _layer_weights(w_in, q_norm, k_norm, dt_bias):
    offs = [0]
    for sz in IN_SPLITS:
        offs.append(offs[-1] + sz)
    col = lambda j: w_in[:, offs[j]:offs[j + 1]]
    wqkv = jnp.concatenate([col(0), col(1), col(2)], axis=1).T.astype(BF16)
    qkw = jnp.concatenate([jnp.tile(q_norm, ATTN_HEADS), jnp.tile(k_norm, ATTN_KV_HEADS)])
    qkw = jnp.broadcast_to(qkw[:, None], (ATTN_Q_DIM + ATTN_KV_DIM, TILE)).astype(F32)
    wxbc = col(4).astype(BF16)
    wdt = jnp.pad(col(5), ((0, 0), (0, DT_PAD - 2 * SSD_HEADS))).astype(BF16)
    dtb = jnp.pad(dt_bias.reshape(1, -1), ((0, 0), (0, DT_PAD - 2 * SSD_HEADS))).astype(F32)
    wr = jnp.concatenate([col(6), col(7), col(8)], axis=1).astype(BF16)
    wzg = jnp.concatenate([col(3), col(9)], axis=1).astype(BF16)
    wgate = col(10).astype(BF16)
    return (wqkv, qkw, wxbc, wdt, dtb, wr), (wzg, wgate)


def _pad_lanes(v):
    v = v.reshape(1, -1).astype(F32)
    return jnp.pad(v, ((0, 0), (0, V7X_LANES - v.shape[1])))


def kernel(x, c, ctx, c_ctx, w_mod, b_mod, norm1_w, norm2_w, w_in, attn_q_norm, attn_k_norm, ssd_conv_w,
           ssd_conv_b, ssd_dt_bias, ssd_a_log, ssd_d, ssd_norm_w, ret_log_decay, ret_gn_w, w_branch, w_out,
           w_mlp1, w_mlp2, final_norm_w):
    b, n, d = x.shape
    m = ctx.shape[1]
    depth = w_in.shape[0]
    assert d == D_MODEL and m == TILE and n % TILE == 0 and n % GRID_W == 0 and b <= 4
    nl = n // TILE

    tabs = _rope_tables(n, m)
    cc = jnp.zeros((8, D_MODEL), F32).at[:b].set(c).at[4].set(c_ctx)
    h = jnp.concatenate([x, ctx], axis=1)

    for layer in range(depth):
        final = layer == depth - 1
        in_w, (wzg, wgate) = _layer_weights(w_in[layer], attn_q_norm[layer], attn_k_norm[layer],
                                            ssd_dt_bias[layer])
        n1w = norm1_w[layer].reshape(1, -1)
        mods = _mod_call(cc, w_mod[layer], b_mod[layer])

        qt, k, vt, xbc, dt, rq, rk, rv = _inproj_call(h, mods, n1w, in_w, tabs, nl)
        ot = _attn_call(qt, k, vt, nl)

        cw = jnp.pad(ssd_conv_w[layer], ((0, 8 - SSD_CONV_K), (0, 0)))
        cb = ssd_conv_b[layer].reshape(1, -1)
        alog = _pad_lanes(ssd_a_log[layer])
        dskip = jnp.repeat(ssd_d[layer], SSD_HEAD_DIM).reshape(1, -1)
        sf = _ssd_call(xbc, dt, cw, cb, alog, dskip, nl, False)
        sb = _ssd_call(xbc, dt, cw, cb, alog, dskip, nl, True)

        ld = _pad_lanes(ret_log_decay[layer])
        rf = _ret_call(rq, rk, rv, ld, nl, False)
        rb = _ret_call(rq, rk, rv, ld, nl, True)

        merge_w = (wzg, wgate, ssd_norm_w[layer].reshape(1, -1), ret_gn_w[layer].reshape(1, -1),
                   w_branch[layer].astype(BF16), w_out[layer].astype(BF16))
        h = _merge_call(h, mods, n1w, ot, sf, sb, rf, rb, merge_w, nl)
        h = _mlp_call(h, mods, norm2_w[layer].reshape(1, -1), w_mlp1[layer].astype(BF16),
                      w_mlp2[layer].astype(BF16), final_norm_w.reshape(1, -1), nl, final)
    return h
```

```python
import functools
import math

import jax
import jax.numpy as jnp
from jax import lax
from jax.experimental import pallas as pl
from jax.experimental.pallas import tpu as pltpu

F32 = jnp.float32
BF16 = jnp.bfloat16

D_MODEL = 1024
GRID_W = 64
NORM_EPS = 1e-6
ROPE_THETA = 10000.0

ATTN_HEADS = 8
ATTN_KV_HEADS = 2
ATTN_GROUP = ATTN_HEADS // ATTN_KV_HEADS
ATTN_HEAD_DIM = 64
ATTN_HALF = ATTN_HEAD_DIM // 2
ATTN_AXIS_FREQS = ATTN_HEAD_DIM // 4
ATTN_Q_DIM = ATTN_HEADS * ATTN_HEAD_DIM
ATTN_KV_DIM = ATTN_KV_HEADS * ATTN_HEAD_DIM

SSD_HEADS = 8
SSD_HEAD_DIM = 64
SSD_D_INNER = SSD_HEADS * SSD_HEAD_DIM
SSD_GROUPS = 2
SSD_STATE = 128
SSD_CONV_K = 3
SSD_CONV_DIM = SSD_D_INNER + 2 * SSD_GROUPS * SSD_STATE
SSD_HEADS_PER_GROUP = SSD_HEADS // SSD_GROUPS

RET_HEADS = 4
RET_DK = 128
RET_DV = 128
RET_DIM = RET_HEADS * RET_DK

N_BRANCH = 3
BRANCH_W = 512
MLP_HIDDEN = 4 * D_MODEL

IN_SPLITS = (ATTN_Q_DIM, ATTN_KV_DIM, ATTN_KV_DIM, SSD_D_INNER, SSD_CONV_DIM, 2 * SSD_HEADS,
             RET_DIM, RET_DIM, RET_HEADS * RET_DV, RET_HEADS * RET_DV, N_BRANCH * D_MODEL)

V7X_LANES = 128
V7X_VMEM_BYTES = 64 * 1024 * 1024

TILE = 256
CHUNK = 128
DT_PAD = V7X_LANES
NEG_BIG = -1e30
ATTN_KEY_CHUNK = 512
ATTN_ONES_ROWS = 16


def _vmem_limit(resident_bytes):
    return int(min(V7X_VMEM_BYTES - 8 * 1024 * 1024, 2 * resident_bytes + 16 * 1024 * 1024))


def _silu(x):
    return x * jax.nn.sigmoid(x)


def _rms_rows(x, w):
    return x * lax.rsqrt(jnp.mean(x * x, axis=-1, keepdims=True) + NORM_EPS) * w


def _dot(a, b):
    return jnp.dot(a, b, preferred_element_type=F32)


def _dot_nt(a, b):
    return lax.dot_general(a, b, (((1,), (1,)), ((), ())), preferred_element_type=F32)


def _dot_tn(a, b):
    return lax.dot_general(a, b, (((0,), (0,)), ((), ())), preferred_element_type=F32)


def _split3(x):
    hi = x.astype(BF16)
    r1 = x - hi.astype(F32)
    mid = r1.astype(BF16)
    lo = (r1 - mid.astype(F32)).astype(BF16)
    return hi, mid, lo


def _mod_kernel(c_ref, w_ref, b_ref, o_ref):
    s = _silu(c_ref[...])
    o_ref[...] = jnp.dot(s, w_ref[...], preferred_element_type=F32,
                         precision=lax.Precision.HIGHEST) + b_ref[...]


def _mod_call(cc, w_mod, b_mod):
    nblk = w_mod.shape[1] // D_MODEL
    out = pl.pallas_call(
        _mod_kernel,
        out_shape=jax.ShapeDtypeStruct((8, nblk * D_MODEL), F32),
        grid=(nblk,),
        in_specs=[pl.BlockSpec((8, D_MODEL), lambda j: (0, 0)),
                  pl.BlockSpec((D_MODEL, D_MODEL), lambda j: (0, j)),
                  pl.BlockSpec((1, D_MODEL), lambda j: (0, j))],
        out_specs=pl.BlockSpec((8, D_MODEL), lambda j: (0, j)),
        compiler_params=pltpu.CompilerParams(dimension_semantics=("arbitrary",)),
        name="mod",
    )(cc, w_mod, b_mod.reshape(1, -1))
    return out.reshape(8, nblk, D_MODEL)


def _inproj_kernel(h_ref, mod_ref, n1w_ref, wqkv_ref, qkw_ref, cos_ref, sin_ref, wxbc_ref, wdt_ref,
                   dtb_ref, wr_ref, cosr_ref, sinr_ref,
                   qt_ref, k_ref, vt_ref, xbc_ref, dt_ref, rq_ref, rk_ref, rv_ref):
    h = h_ref[0]
    u = _rms_rows(h, n1w_ref[...]) * (1.0 + mod_ref[0, 1:2, :]) + mod_ref[0, 0:1, :]
    ub = u.astype(BF16)

    qkv_t = _dot_nt(wqkv_ref[...], ub)
    cos = cos_ref[...]
    sin = sin_ref[...]
    k_rows = []
    for hd in range(ATTN_HEADS + ATTN_KV_HEADS):
        r0 = hd * ATTN_HEAD_DIM
        xh = qkv_t[r0:r0 + ATTN_HEAD_DIM]
        yh = xh * lax.rsqrt(jnp.mean(xh * xh, axis=0, keepdims=True) + NORM_EPS) * qkw_ref[r0:r0 + ATTN_HEAD_DIM, :]
        y1 = yh[:ATTN_HALF]
        y2 = yh[ATTN_HALF:]
        o1 = y1 * cos - y2 * sin
        o2 = y1 * sin + y2 * cos
        if hd < ATTN_HEADS:
            scale = ATTN_HEAD_DIM ** -0.5 * math.log2(math.e)
            qt_ref[0, r0:r0 + ATTN_HALF, :] = (o1 * scale).astype(BF16)
            qt_ref[0, r0 + ATTN_HALF:r0 + ATTN_HEAD_DIM, :] = (o2 * scale).astype(BF16)
        else:
            k_rows += [o1, o2]
    k_t = jnp.concatenate(k_rows, axis=0)
    k_ref[0] = jnp.transpose(k_t).astype(BF16)
    vt_ref[0] = qkv_t[ATTN_Q_DIM + ATTN_KV_DIM:].astype(BF16)

    xbc_ref[0] = _dot(ub, wxbc_ref[...])
    dt_raw = _dot(ub, wdt_ref[...]) + dtb_ref[...]
    dt_ref[0] = jnp.maximum(dt_raw, 0.0) + jnp.log1p(jnp.exp(-jnp.abs(dt_raw)))

    r = _dot(ub, wr_ref[...])
    cosr = cosr_ref[...]
    sinr = sinr_ref[...]
    for hd in range(RET_HEADS):
        c0 = hd * RET_DK
        qh = r[:, c0:c0 + RET_DK]
        kh = r[:, RET_DIM + c0:RET_DIM + c0 + RET_DK]
        rq_ref[0, :, c0:c0 + RET_DK] = (qh * cosr + pltpu.roll(qh, RET_DK // 2, 1) * sinr).astype(BF16)
        rk_ref[0, :, c0:c0 + RET_DK] = ((kh * cosr + pltpu.roll(kh, RET_DK // 2, 1) * sinr)
                                        * (RET_DK ** -0.5)).astype(BF16)
    rv_ref[0] = r[:, 2 * RET_DIM:].astype(BF16)


def _inproj_call(h, mods, n1w, wts, tabs, nl):
    b, l, _ = h.shape
    nt = l // TILE
    wqkv, qkw, wxbc, wdt, dtb, wr = wts
    cos_a, sin_a, cos_r, sin_r = tabs
    const = lambda shape: pl.BlockSpec(shape, lambda bi, i: (0,) * len(shape))
    tok = lambda c: pl.BlockSpec((1, TILE, c), lambda bi, i: (bi, i, 0))
    tok_t = lambda r: pl.BlockSpec((1, r, TILE), lambda bi, i: (bi, 0, i))
    weights_bytes = 2 * (wqkv.size + wxbc.size + wdt.size + wr.size) + 4 * qkw.size
    return pl.pallas_call(
        _inproj_kernel,
        out_shape=(jax.ShapeDtypeStruct((b, ATTN_Q_DIM, l), BF16),
                   jax.ShapeDtypeStruct((b, l, ATTN_KV_DIM), BF16),
                   jax.ShapeDtypeStruct((b, ATTN_KV_DIM, l), BF16),
                   jax.ShapeDtypeStruct((b, l, SSD_CONV_DIM), F32),
                   jax.ShapeDtypeStruct((b, l, DT_PAD), F32),
                   jax.ShapeDtypeStruct((b, l, RET_DIM), BF16),
                   jax.ShapeDtypeStruct((b, l, RET_DIM), BF16),
                   jax.ShapeDtypeStruct((b, l, RET_DIM), BF16)),
        grid=(b, nt),
        in_specs=[tok(D_MODEL),
                  pl.BlockSpec((1, 6, D_MODEL), lambda bi, i: (jnp.where(i == nl, 4, bi), 0, 0)),
                  const((1, D_MODEL)),
                  const(wqkv.shape), const(qkw.shape),
                  pl.BlockSpec((ATTN_HALF, TILE), lambda bi, i: (0, i)),
                  pl.BlockSpec((ATTN_HALF, TILE), lambda bi, i: (0, i)),
                  const(wxbc.shape), const(wdt.shape), const(dtb.shape), const(wr.shape),
                  pl.BlockSpec((TILE, RET_DK), lambda bi, i: (i, 0)),
                  pl.BlockSpec((TILE, RET_DK), lambda bi, i: (i, 0))],
        out_specs=(tok_t(ATTN_Q_DIM), tok(ATTN_KV_DIM), tok_t(ATTN_KV_DIM), tok(SSD_CONV_DIM), tok(DT_PAD),
                   tok(RET_DIM), tok(RET_DIM), tok(RET_DIM)),
        compiler_params=pltpu.CompilerParams(dimension_semantics=("arbitrary", "arbitrary"),
                                             vmem_limit_bytes=_vmem_limit(weights_bytes)),
        name="inproj",
    )(h, mods, n1w, wqkv, qkw, cos_a, sin_a, wxbc, wdt, dtb, wr, cos_r, sin_r)


def _attn_kernel(qt_ref, k_ref, vt_ref, o_ref, qpad_sc, m_sc, acc_sc, *, nl):
    kv = pl.program_id(1)
    i = pl.program_id(2)

    row = lax.broadcasted_iota(jnp.int32, (2 * ATTN_HEAD_DIM, TILE), 0)
    keep = (row >= ATTN_HEAD_DIM).astype(jnp.int32) == kv
    for g in range(ATTN_GROUP):
        qg = qt_ref[0, g * ATTN_HEAD_DIM:(g + 1) * ATTN_HEAD_DIM, :]
        qq = jnp.concatenate([qg, qg], axis=0)
        qpad_sc[:, g * TILE:(g + 1) * TILE] = jnp.where(keep, qq, jnp.zeros_like(qq))
    m_sc[...] = jnp.full(m_sc.shape, NEG_BIG, F32)
    acc_sc[...] = jnp.zeros(acc_sc.shape, F32)

    def chunk(off, size):
        kc = k_ref[0, pl.ds(off, size), :]
        vc = vt_ref[0, :, pl.ds(off, size)]
        s = _dot(kc, qpad_sc[...])
        m_old = m_sc[...]
        m_new = jnp.maximum(m_old, jnp.max(s, axis=0, keepdims=True))
        p = jnp.exp2(s - m_new).astype(BF16)
        v_aug = jnp.concatenate([vc, jnp.ones((ATTN_ONES_ROWS, size), BF16)], axis=0)
        acc_sc[...] = jnp.exp2(m_old - m_new) * acc_sc[...] + _dot(v_aug, p)
        m_sc[...] = m_new

    @pl.when(i != nl)
    def _():
        def body(c, carry):
            chunk(pl.multiple_of(c * ATTN_KEY_CHUNK, ATTN_KEY_CHUNK), ATTN_KEY_CHUNK)
            return carry
        lax.fori_loop(0, nl * TILE // ATTN_KEY_CHUNK, body, 0)

    chunk(nl * TILE, TILE)

    acc = acc_sc[...]
    out = acc[:ATTN_HEAD_DIM] / acc[ATTN_HEAD_DIM:ATTN_HEAD_DIM + 1]
    for g in range(ATTN_GROUP):
        o_ref[0, g * ATTN_HEAD_DIM:(g + 1) * ATTN_HEAD_DIM, :] = out[:, g * TILE:(g + 1) * TILE].astype(BF16)


def _attn_call(qt, k, vt, nl):
    b, _, l = qt.shape
    nt = l // TILE
    assert (nl * TILE) % ATTN_KEY_CHUNK == 0
    gq = ATTN_GROUP * ATTN_HEAD_DIM
    wide = ATTN_GROUP * TILE
    resident = 2 * (l * ATTN_KV_DIM + ATTN_HEAD_DIM * l) + 6 * ATTN_KEY_CHUNK * wide
    return pl.pallas_call(
        functools.partial(_attn_kernel, nl=nl),
        out_shape=jax.ShapeDtypeStruct((b, ATTN_Q_DIM, l), BF16),
        grid=(b, ATTN_KV_HEADS, nt),
        in_specs=[pl.BlockSpec((1, gq, TILE), lambda bi, kv, i: (bi, kv, i)),
                  pl.BlockSpec((1, l, ATTN_KV_DIM), lambda bi, kv, i: (bi, 0, 0)),
                  pl.BlockSpec((1, ATTN_HEAD_DIM, l), lambda bi, kv, i: (bi, kv, 0))],
        out_specs=pl.BlockSpec((1, gq, TILE), lambda bi, kv, i: (bi, kv, i)),
        scratch_shapes=[pltpu.VMEM((2 * ATTN_HEAD_DIM, wide), BF16),
                        pltpu.VMEM((1, wide), F32),
                        pltpu.VMEM((ATTN_HEAD_DIM + ATTN_ONES_ROWS, wide), F32)],
        compiler_params=pltpu.CompilerParams(dimension_semantics=("arbitrary",) * 3,
                                             vmem_limit_bytes=_vmem_limit(resident)),
        name="attn",
    )(qt, k, vt)


def _scan_tile(s, nl, reverse):
    return jnp.where(s == 0, nl, nl - s) if reverse else jnp.where(s == 0, nl, s - 1)


def _tri(reverse):
    r = lax.broadcasted_iota(jnp.int32, (CHUNK, CHUNK), 0)
    c = lax.broadcasted_iota(jnp.int32, (CHUNK, CHUNK), 1)
    return (c >= r) if reverse else (c <= r)


def _lane_bcast(x, c):
    return jnp.broadcast_to(x[:, c:c + 1], (x.shape[0], V7X_LANES))


def _ssd_kernel(x_ref, xp_ref, xn_ref, dt_ref, cw_ref, cb_ref, alog_ref, dskip_ref, y_ref, st_sc,
                *, nl, reverse):
    s = pl.program_id(1)
    t = _scan_tile(s, nl, reverse)

    @pl.when(s == 0)
    def _():
        st_sc[...] = jnp.zeros(st_sc.shape, F32)

    x = x_ref[0]
    row = lax.broadcasted_iota(jnp.int32, (TILE, SSD_CONV_DIM), 0)
    has_prev = jnp.logical_and(t != nl, t != 0)
    has_next = t < nl - 1
    prev_row = jnp.where(has_prev, xp_ref[0, 7:8, :], 0.0)
    next_row = jnp.where(has_next, xn_ref[0, 0:1, :], 0.0)
    x_m1 = jnp.where(row == 0, prev_row, pltpu.roll(x, 1, 0))
    x_p1 = jnp.where(row == TILE - 1, next_row, pltpu.roll(x, TILE - 1, 0))
    xs = _silu(cw_ref[0:1, :] * x_m1 + cw_ref[1:2, :] * x + cw_ref[2:3, :] * x_p1 + cb_ref[...])

    tri = _tri(reverse)
    tri_b = tri.astype(BF16)
    lane = lax.broadcasted_iota(jnp.int32, (CHUNK, V7X_LANES), 1)
    left = lane < SSD_HEAD_DIM
    col0 = SSD_HEADS if reverse else 0
    a_neg = -jnp.exp(alog_ref[...])
    last = 0 if reverse else CHUNK - 1

    chunks = range(TILE // CHUNK)
    for ci in (reversed(chunks) if reverse else chunks):
        r0 = ci * CHUNK
        xc = xs[r0:r0 + CHUNK]
        dtc = dt_ref[0, r0:r0 + CHUNK, :]
        a = dtc * a_neg
        hi, mid, lo = _split3(a)
        a_cum = _dot(tri_b, hi) + _dot(tri_b, mid) + _dot(tri_b, lo)
        a_cum_t = jnp.transpose(a_cum)
        for g in range(SSD_GROUPS):
            bm = xc[:, SSD_D_INNER + g * SSD_STATE:SSD_D_INNER + (g + 1) * SSD_STATE]
            cm = xc[:, SSD_D_INNER + (SSD_GROUPS + g) * SSD_STATE:SSD_D_INNER + (SSD_GROUPS + g + 1) * SSD_STATE]
            bmb = bm.astype(BF16)
            cmb = cm.astype(BF16)
            cb = _dot_nt(cmb, bmb)
            st_prev = st_sc[g]
            y_off = _dot(cmb, st_prev.astype(BF16))
            xdd_pairs = []
            tot = []
            for pr in range(SSD_HEADS_PER_GROUP // 2):
                h0 = g * SSD_HEADS_PER_GROUP + 2 * pr
                lanes0 = h0 * SSD_HEAD_DIM
                x2 = xc[:, lanes0:lanes0 + V7X_LANES]
                acol = [_lane_bcast(a_cum, col0 + h0 + j) for j in range(2)]
                dcol = [_lane_bcast(dtc, col0 + h0 + j) for j in range(2)]
                a2 = jnp.where(left, acol[0], acol[1])
                xd2 = x2 * jnp.where(left, dcol[0], dcol[1])
                xd2b = xd2.astype(BF16)
                y_pair = []
                for j in range(2):
                    arow = a_cum_t[col0 + h0 + j:col0 + h0 + j + 1, :]
                    lmat = jnp.where(tri, jnp.exp(jnp.where(tri, acol[j] - arow, 0.0)), 0.0)
                    y_pair.append(_dot((cb * lmat).astype(BF16), xd2b))
                y2 = jnp.where(left, y_pair[0], y_pair[1])
                y2 = y2 + jnp.exp(a2) * y_off[:, pr * V7X_LANES:(pr + 1) * V7X_LANES]
                if not reverse:
                    y2 = y2 + dskip_ref[:, lanes0:lanes0 + V7X_LANES] * x2
                y_ref[0, r0:r0 + CHUNK, lanes0:lanes0 + V7X_LANES] = y2
                a_tot = a2[last:last + 1, :]
                xdd_pairs.append((xd2 * jnp.exp(a_tot - a2)).astype(BF16))
                tot.append(a_tot)
            xdd = jnp.concatenate(xdd_pairs, axis=1)
            st_sc[g] = jnp.exp(jnp.concatenate(tot, axis=1)) * st_prev + _dot_tn(bmb, xdd)


def _ssd_call(xbc, dt, cw, cb, alog, dskip, nl, reverse):
    b, l, _ = xbc.shape
    nt = l // TILE
    rows8 = TILE // 8
    tile_of = lambda s: _scan_tile(s, nl, reverse)
    const = lambda shape: pl.BlockSpec(shape, lambda bi, s: (0,) * len(shape))
    return pl.pallas_call(
        functools.partial(_ssd_kernel, nl=nl, reverse=reverse),
        out_shape=jax.ShapeDtypeStruct((b, l, SSD_D_INNER), F32),
        grid=(b, nt),
        in_specs=[pl.BlockSpec((1, TILE, SSD_CONV_DIM), lambda bi, s: (bi, tile_of(s), 0)),
                  pl.BlockSpec((1, 8, SSD_CONV_DIM), lambda bi, s: (bi, jnp.maximum(tile_of(s) * rows8 - 1, 0), 0)),
                  pl.BlockSpec((1, 8, SSD_CONV_DIM),
                               lambda bi, s: (bi, jnp.minimum((tile_of(s) + 1) * rows8, nt * rows8 - 1), 0)),
                  pl.BlockSpec((1, TILE, DT_PAD), lambda bi, s: (bi, tile_of(s), 0)),
                  const(cw.shape), const(cb.shape), const(alog.shape), const(dskip.shape)],
        out_specs=pl.BlockSpec((1, TILE, SSD_D_INNER), lambda bi, s: (bi, tile_of(s), 0)),
        scratch_shapes=[pltpu.VMEM((SSD_GROUPS, SSD_STATE, SSD_HEADS_PER_GROUP * SSD_HEAD_DIM), F32)],
        compiler_params=pltpu.CompilerParams(dimension_semantics=("arbitrary", "arbitrary")),
        name="ssd_bwd" if reverse else "ssd_fwd",
    )(xbc, xbc, xbc, dt, cw, cb, alog, dskip)


def _ret_kernel(q_ref, k_ref, v_ref, ld_ref, y_ref, st_sc, *, nl, reverse):
    s = pl.program_id(1)

    @pl.when(s == 0)
    def _():
        st_sc[...] = jnp.zeros(st_sc.shape, F32)

    tri = _tri(reverse)
    r = lax.broadcasted_iota(jnp.int32, (CHUNK, CHUNK), 0)
    c = lax.broadcasted_iota(jnp.int32, (CHUNK, CHUNK), 1)
    dist = jnp.maximum((c - r) if reverse else (r - c), 0).astype(F32)
    pos = ((CHUNK - 1 - r) if reverse else r).astype(F32)
    col0 = RET_HEADS if reverse else 0

    chunks = range(TILE // CHUNK)
    for ci in (reversed(chunks) if reverse else chunks):
        r0 = ci * CHUNK
        for hd in range(RET_HEADS):
            c0 = hd * RET_DK
            lg = jnp.broadcast_to(-jnp.exp(ld_ref[0:1, col0 + hd:col0 + hd + 1]), (CHUNK, CHUNK))
            q = q_ref[0, r0:r0 + CHUNK, c0:c0 + RET_DK]
            k = k_ref[0, r0:r0 + CHUNK, c0:c0 + RET_DK]
            v = v_ref[0, r0:r0 + CHUNK, c0:c0 + RET_DV]
            dmat = jnp.where(tri, jnp.exp(dist * lg), 0.0)
            sc = _dot_nt(q, k) * dmat
            st_prev = st_sc[hd]
            y = _dot(sc.astype(BF16), v) + _dot(q, st_prev.astype(BF16)) * jnp.exp((pos + 1.0) * lg)
            y_ref[0, r0:r0 + CHUNK, c0:c0 + RET_DV] = y
            k_dec = (k.astype(F32) * jnp.exp((CHUNK - 1.0 - pos) * lg)).astype(BF16)
            st_sc[hd] = jnp.exp(CHUNK * lg) * st_prev + _dot_tn(k_dec, v)


def _ret_call(rq, rk, rv, ld, nl, reverse):
    b, l, _ = rq.shape
    nt = l // TILE
    tok = pl.BlockSpec((1, TILE, RET_DIM), lambda bi, s: (bi, _scan_tile(s, nl, reverse), 0))
    return pl.pallas_call(
        functools.partial(_ret_kernel, nl=nl, reverse=reverse),
        out_shape=jax.ShapeDtypeStruct((b, l, RET_DIM), F32),
        grid=(b, nt),
        in_specs=[tok, tok, tok, pl.BlockSpec(ld.shape, lambda bi, s: (0, 0))],
        out_specs=tok,
        scratch_shapes=[pltpu.VMEM((RET_HEADS, RET_DK, RET_DV), F32)],
        compiler_params=pltpu.CompilerParams(dimension_semantics=("arbitrary", "arbitrary")),
        name="ret_bwd" if reverse else "ret_fwd",
    )(rq, rk, rv, ld)


def _merge_kernel(h_ref, mod_ref, n1w_ref, ot_ref, sf_ref, sb_ref, rf_ref, rb_ref, wzg_ref, wgate_ref,
                  snw_ref, gnw_ref, wb_ref, wout_ref, o_ref):
    h = h_ref[0]
    u = _rms_rows(h, n1w_ref[...]) * (1.0 + mod_ref[0, 1:2, :]) + mod_ref[0, 0:1, :]
    ub = u.astype(BF16)
    zg = _dot(ub, wzg_ref[...])
    gates = jax.nn.sigmoid(_dot(ub, wgate_ref[...]))

    br_attn = _dot_tn(ot_ref[0], wb_ref[0])

    y = (sf_ref[0] + sb_ref[0]) * _silu(zg[:, :SSD_D_INNER])
    br_ssd = _dot(_rms_rows(y, snw_ref[...]).astype(BF16), wb_ref[1])

    yr = rf_ref[0] + rb_ref[0]
    heads = []
    for hd in range(RET_HEADS):
        yh = yr[:, hd * RET_DV:(hd + 1) * RET_DV]
        yc = yh - jnp.mean(yh, axis=-1, keepdims=True)
        heads.append(yc * lax.rsqrt(jnp.mean(yc * yc, axis=-1, keepdims=True) + NORM_EPS))
    yn = jnp.concatenate(heads, axis=1) * gnw_ref[...] * _silu(zg[:, SSD_D_INNER:])
    br_ret = _dot(yn.astype(BF16), wb_ref[2])

    merged = (gates[:, :D_MODEL] * br_attn + gates[:, D_MODEL:2 * D_MODEL] * br_ssd
              + gates[:, 2 * D_MODEL:] * br_ret)
    o_ref[0] = h + mod_ref[0, 2:3, :] * _dot(merged.astype(BF16), wout_ref[...])


def _merge_call(h, mods, n1w, ot, sf, sb, rf, rb, wts, nl):
    b, l, _ = h.shape
    nt = l // TILE
    wzg, wgate, snw, gnw, wb, wout = wts
    const = lambda shape: pl.BlockSpec(shape, lambda bi, i: (0,) * len(shape))
    tok = lambda c: pl.BlockSpec((1, TILE, c), lambda bi, i: (bi, i, 0))
    weights_bytes = 2 * (wzg.size + wgate.size + wb.size + wout.size)
    return pl.pallas_call(
        _merge_kernel,
        out_shape=jax.ShapeDtypeStruct((b, l, D_MODEL), F32),
        grid=(b, nt),
        in_specs=[tok(D_MODEL),
                  pl.BlockSpec((1, 6, D_MODEL), lambda bi, i: (jnp.where(i == nl, 4, bi), 0, 0)),
                  const((1, D_MODEL)),
                  pl.BlockSpec((1, ATTN_Q_DIM, TILE), lambda bi, i: (bi, 0, i)),
                  tok(SSD_D_INNER), tok(SSD_D_INNER), tok(RET_DIM), tok(RET_DIM),
                  const(wzg.shape), const(wgate.shape), const(snw.shape), const(gnw.shape),
                  const(wb.shape), const(wout.shape)],
        out_specs=tok(D_MODEL),
        compiler_params=pltpu.CompilerParams(dimension_semantics=("arbitrary", "arbitrary"),
                                             vmem_limit_bytes=_vmem_limit(weights_bytes)),
        name="merge",
    )(h, mods, n1w, ot, sf, sb, rf, rb, wzg, wgate, snw, gnw, wb, wout)


def _mlp_kernel(h_ref, mod_ref, n2w_ref, w1_ref, w2_ref, fw_ref, o_ref, *, final):
    h = h_ref[0]
    v = _rms_rows(h, n2w_ref[...]) * (1.0 + mod_ref[0, 4:5, :]) + mod_ref[0, 3:4, :]
    a = jnp.maximum(_dot(v.astype(BF16), w1_ref[...]), 0.0)
    out = h + mod_ref[0, 5:6, :] * _dot((a * a).astype(BF16), w2_ref[...])
    o_ref[0] = _rms_rows(out, fw_ref[...]) if final else out


def _mlp_call(h, mods, n2w, w1, w2, fw, nl, final):
    b, l, _ = h.shape
    nt = nl if final else l // TILE
    const = lambda shape: pl.BlockSpec(shape, lambda bi, i: (0,) * len(shape))
    tok = pl.BlockSpec((1, TILE, D_MODEL), lambda bi, i: (bi, i, 0))
    return pl.pallas_call(
        functools.partial(_mlp_kernel, final=final),
        out_shape=jax.ShapeDtypeStruct((b, nt * TILE, D_MODEL), F32),
        grid=(b, nt),
        in_specs=[tok,
                  pl.BlockSpec((1, 6, D_MODEL), lambda bi, i: (jnp.where(i == nl, 4, bi), 0, 0)),
                  const((1, D_MODEL)), const(w1.shape), const(w2.shape), const((1, D_MODEL))],
        out_specs=tok,
        compiler_params=pltpu.CompilerParams(dimension_semantics=("arbitrary", "arbitrary"),
                                             vmem_limit_bytes=_vmem_limit(2 * (w1.size + w2.size))),
        name="mlp_final" if final else "mlp",
    )(h, mods, n2w, w1, w2, fw)


def _rope_tables(n, m):
    rows = n // GRID_W
    row = jnp.repeat(jnp.arange(rows, dtype=F32), GRID_W)
    col = jnp.tile(jnp.arange(GRID_W, dtype=F32), rows)
    inv = ROPE_THETA ** (-jnp.arange(ATTN_AXIS_FREQS, dtype=F32) / ATTN_AXIS_FREQS)
    ang = jnp.concatenate([row[:, None] * inv, col[:, None] * inv], axis=-1)
    cos_a = jnp.concatenate([jnp.cos(ang), jnp.ones((m, ATTN_HALF), F32)], axis=0).T
    sin_a = jnp.concatenate([jnp.sin(ang), jnp.zeros((m, ATTN_HALF), F32)], axis=0).T
    pos = jnp.concatenate([jnp.arange(n, dtype=F32) + m, jnp.arange(m, dtype=F32)])
    inv_r = ROPE_THETA ** (-jnp.linspace(0.0, 1.0, RET_DK // 2, dtype=F32))
    ang_r = pos[:, None] * inv_r
    cos_r = jnp.concatenate([jnp.cos(ang_r), jnp.cos(ang_r)], axis=-1)
    sin_r = jnp.concatenate([-jnp.sin(ang_r), jnp.sin(ang_r)], axis=-1)
    return cos_a, sin_a, cos_r, sin_r


def _layer_weights(w_in, q_norm, k_norm, dt_bias):
    offs = [0]
    for sz in IN_SPLITS:
        offs.append(offs[-1] + sz)
    col = lambda j: w_in[:, offs[j]:offs[j + 1]]
    wqkv = jnp.concatenate([col(0), col(1), col(2)], axis=1).T.astype(BF16)
    qkw = jnp.concatenate([jnp.tile(q_norm, ATTN_HEADS), jnp.tile(k_norm, ATTN_KV_HEADS)])
    qkw = jnp.broadcast_to(qkw[:, None], (ATTN_Q_DIM + ATTN_KV_DIM, TILE)).astype(F32)
    wxbc = col(4).astype(BF16)
    wdt = jnp.pad(col(5), ((0, 0), (0, DT_PAD - 2 * SSD_HEADS))).astype(BF16)
    dtb = jnp.pad(dt_bias.reshape(1, -1), ((0, 0), (0, DT_PAD - 2 * SSD_HEADS))).astype(F32)
    wr = jnp.concatenate([col(6), col(7), col(8)], axis=1).astype(BF16)
    wzg = jnp.concatenate([col(3), col(9)], axis=1).astype(BF16)
    wgate = col(10).astype(BF16)
    return (wqkv, qkw, wxbc, wdt, dtb, wr), (wzg, wgate)


def _pad_lanes(v):
    v = v.reshape(1, -1).astype(F32)
    return jnp.pad(v, ((0, 0), (0, V7X_LANES - v.shape[1])))


def kernel(x, c, ctx, c_ctx, w_mod, b_mod, norm1_w, norm2_w, w_in, attn_q_norm, attn_k_norm, ssd_conv_w,
           ssd_conv_b, ssd_dt_bias, ssd_a_log, ssd_d, ssd_norm_w, ret_log_decay, ret_gn_w, w_branch, w_out,
           w_mlp1, w_mlp2, final_norm_w):
    b, n, d = x.shape
    m = ctx.shape[1]
    depth = w_in.shape[0]
    assert d == D_MODEL and m == TILE and n % TILE == 0 and n % GRID_W == 0 and b <= 4
    nl = n // TILE

    tabs = _rope_tables(n, m)
    cc = jnp.zeros((8, D_MODEL), F32).at[:b].set(c).at[4].set(c_ctx)
    h = jnp.concatenate([x, ctx], axis=1)

    for layer in range(depth):
        final = layer == depth - 1
        in_w, (wzg, wgate) = _layer_weights(w_in[layer], attn_q_norm[layer], attn_k_norm[layer],
                                            ssd_dt_bias[layer])
        n1w = norm1_w[layer].reshape(1, -1)
        mods = _mod_call(cc, w_mod[layer], b_mod[layer])

        qt, k, vt, xbc, dt, rq, rk, rv = _inproj_call(h, mods, n1w, in_w, tabs, nl)
        ot = _attn_call(qt, k, vt, nl)

        cw = jnp.pad(ssd_conv_w[layer], ((0, 8 - SSD_CONV_K), (0, 0)))
        cb = ssd_conv_b[layer].reshape(1, -1)
        alog = _pad_lanes(ssd_a_log[layer])
        dskip = jnp.repeat(ssd_d[layer], SSD_HEAD_DIM).reshape(1, -1)
        sf = _ssd_call(xbc, dt, cw, cb, alog, dskip, nl, False)
        sb = _ssd_call(xbc, dt, cw, cb, alog, dskip, nl, True)

        ld = _pad_lanes(ret_log_decay[layer])
        rf = _ret_call(rq, rk, rv, ld, nl, False)
        rb = _ret_call(rq, rk, rv, ld, nl, True)

        merge_w = (wzg, wgate, ssd_norm_w[layer].reshape(1, -1), ret_gn_w[layer].reshape(1, -1),
                   w_branch[layer].astype(BF16), w_out[layer].astype(BF16))
        h = _merge_call(h, mods, n1w, ot, sf, sb, rf, rb, merge_w, nl)
        h = _mlp_call(h, mods, norm2_w[layer].reshape(1, -1), w_mlp1[layer].astype(BF16),
                      w_mlp2[layer].astype(BF16), final_norm_w.reshape(1, -1), nl, final)
    return h
```

```python
import functools
import math

import jax
import jax.numpy as jnp
from jax import lax
from jax.experimental import pallas as pl
from jax.experimental.pallas import tpu as pltpu

F32 = jnp.float32
BF16 = jnp.bfloat16

D_MODEL = 1024
GRID_W = 64
NORM_EPS = 1e-6
ROPE_THETA = 10000.0

ATTN_HEADS = 8
ATTN_KV_HEADS = 2
ATTN_GROUP = ATTN_HEADS // ATTN_KV_HEADS
ATTN_HEAD_DIM = 64
ATTN_HALF = ATTN_HEAD_DIM // 2
ATTN_AXIS_FREQS = ATTN_HEAD_DIM // 4
ATTN_Q_DIM = ATTN_HEADS * ATTN_HEAD_DIM
ATTN_KV_DIM = ATTN_KV_HEADS * ATTN_HEAD_DIM

SSD_HEADS = 8
SSD_HEAD_DIM = 64
SSD_D_INNER = SSD_HEADS * SSD_HEAD_DIM
SSD_GROUPS = 2
SSD_STATE = 128
SSD_CONV_K = 3
SSD_CONV_DIM = SSD_D_INNER + 2 * SSD_GROUPS * SSD_STATE
SSD_HEADS_PER_GROUP = SSD_HEADS // SSD_GROUPS

RET_HEADS = 4
RET_DK = 128
RET_DV = 128
RET_DIM = RET_HEADS * RET_DK

N_BRANCH = 3
BRANCH_W = 512
MLP_HIDDEN = 4 * D_MODEL

IN_SPLITS = (ATTN_Q_DIM, ATTN_KV_DIM, ATTN_KV_DIM, SSD_D_INNER, SSD_CONV_DIM, 2 * SSD_HEADS,
             RET_DIM, RET_DIM, RET_HEADS * RET_DV, RET_HEADS * RET_DV, N_BRANCH * D_MODEL)

V7X_LANES = 128
V7X_VMEM_BYTES = 64 * 1024 * 1024

TILE = 256
CHUNK = 128
DT_PAD = V7X_LANES
NEG_BIG = -1e30
ATTN_KEY_CHUNK_MAX = 1024
ATTN_ONES_ROWS = 16


def _vmem_limit(resident_bytes):
    return int(min(V7X_VMEM_BYTES - 8 * 1024 * 1024, 2 * resident_bytes + 16 * 1024 * 1024))


def _silu(x):
    return x * jax.nn.sigmoid(x)


def _rms_rows(x, w):
    return x * lax.rsqrt(jnp.mean(x * x, axis=-1, keepdims=True) + NORM_EPS) * w


def _dot(a, b):
    return jnp.dot(a, b, preferred_element_type=F32)


def _dot_nt(a, b):
    return lax.dot_general(a, b, (((1,), (1,)), ((), ())), preferred_element_type=F32)


def _dot_tn(a, b):
    return lax.dot_general(a, b, (((0,), (0,)), ((), ())), preferred_element_type=F32)


def _split3(x):
    hi = x.astype(BF16)
    r1 = x - hi.astype(F32)
    mid = r1.astype(BF16)
    lo = (r1 - mid.astype(F32)).astype(BF16)
    return hi, mid, lo


def _mod_kernel(c_ref, w_ref, b_ref, o_ref):
    s = _silu(c_ref[...])
    o_ref[...] = jnp.dot(s, w_ref[...], preferred_element_type=F32,
                         precision=lax.Precision.HIGHEST) + b_ref[...]


def _mod_call(cc, w_mod, b_mod):
    nblk = w_mod.shape[1] // D_MODEL
    out = pl.pallas_call(
        _mod_kernel,
        out_shape=jax.ShapeDtypeStruct((8, nblk * D_MODEL), F32),
        grid=(nblk,),
        in_specs=[pl.BlockSpec((8, D_MODEL), lambda j: (0, 0)),
                  pl.BlockSpec((D_MODEL, D_MODEL), lambda j: (0, j)),
                  pl.BlockSpec((1, D_MODEL), lambda j: (0, j))],
        out_specs=pl.BlockSpec((8, D_MODEL), lambda j: (0, j)),
        compiler_params=pltpu.CompilerParams(dimension_semantics=("arbitrary",)),
        name="mod",
    )(cc, w_mod, b_mod.reshape(1, -1))
    return out.reshape(8, nblk, D_MODEL)


def _inproj_kernel(h_ref, mod_ref, n1w_ref, wqkv_ref, qkw_ref, cos_ref, sin_ref, wxbc_ref, wdt_ref,
                   dtb_ref, wr_ref, cosr_ref, sinr_ref,
                   qt_ref, k_ref, vt_ref, xbc_ref, dt_ref, rq_ref, rk_ref, rv_ref):
    h = h_ref[0]
    u = _rms_rows(h, n1w_ref[...]) * (1.0 + mod_ref[0, 1:2, :]) + mod_ref[0, 0:1, :]
    ub = u.astype(BF16)

    qkv_t = _dot_nt(wqkv_ref[...], ub)
    cos = cos_ref[...]
    sin = sin_ref[...]
    k_rows = []
    for hd in range(ATTN_HEADS + ATTN_KV_HEADS):
        r0 = hd * ATTN_HEAD_DIM
        xh = qkv_t[r0:r0 + ATTN_HEAD_DIM]
        yh = xh * lax.rsqrt(jnp.mean(xh * xh, axis=0, keepdims=True) + NORM_EPS) * qkw_ref[r0:r0 + ATTN_HEAD_DIM, :]
        y1 = yh[:ATTN_HALF]
        y2 = yh[ATTN_HALF:]
        o1 = y1 * cos - y2 * sin
        o2 = y1 * sin + y2 * cos
        if hd < ATTN_HEADS:
            scale = ATTN_HEAD_DIM ** -0.5 * math.log2(math.e)
            qt_ref[0, r0:r0 + ATTN_HALF, :] = (o1 * scale).astype(BF16)
            qt_ref[0, r0 + ATTN_HALF:r0 + ATTN_HEAD_DIM, :] = (o2 * scale).astype(BF16)
        else:
            k_rows += [o1, o2]
    k_t = jnp.concatenate(k_rows, axis=0)
    k_ref[0] = jnp.transpose(k_t).astype(BF16)
    vt_ref[0] = qkv_t[ATTN_Q_DIM + ATTN_KV_DIM:].astype(BF16)

    xbc_ref[0] = _dot(ub, wxbc_ref[...])
    dt_raw = _dot(ub, wdt_ref[...]) + dtb_ref[...]
    dt_ref[0] = jnp.maximum(dt_raw, 0.0) + jnp.log1p(jnp.exp(-jnp.abs(dt_raw)))

    r = _dot(ub, wr_ref[...])
    cosr = cosr_ref[...]
    sinr = sinr_ref[...]
    for hd in range(RET_HEADS):
        c0 = hd * RET_DK
        qh = r[:, c0:c0 + RET_DK]
        kh = r[:, RET_DIM + c0:RET_DIM + c0 + RET_DK]
        rq_ref[0, :, c0:c0 + RET_DK] = (qh * cosr + pltpu.roll(qh, RET_DK // 2, 1) * sinr).astype(BF16)
        rk_ref[0, :, c0:c0 + RET_DK] = ((kh * cosr + pltpu.roll(kh, RET_DK // 2, 1) * sinr)
                                        * (RET_DK ** -0.5)).astype(BF16)
    rv_ref[0] = r[:, 2 * RET_DIM:].astype(BF16)


def _inproj_call(h, mods, n1w, wts, tabs, nl):
    b, l, _ = h.shape
    nt = l // TILE
    wqkv, qkw, wxbc, wdt, dtb, wr = wts
    cos_a, sin_a, cos_r, sin_r = tabs
    const = lambda shape: pl.BlockSpec(shape, lambda bi, i: (0,) * len(shape))
    tok = lambda c: pl.BlockSpec((1, TILE, c), lambda bi, i: (bi, i, 0))
    tok_t = lambda r: pl.BlockSpec((1, r, TILE), lambda bi, i: (bi, 0, i))
    weights_bytes = 2 * (wqkv.size + wxbc.size + wdt.size + wr.size) + 4 * qkw.size
    return pl.pallas_call(
        _inproj_kernel,
        out_shape=(jax.ShapeDtypeStruct((b, ATTN_Q_DIM, l), BF16),
                   jax.ShapeDtypeStruct((b, l, ATTN_KV_DIM), BF16),
                   jax.ShapeDtypeStruct((b, ATTN_KV_DIM, l), BF16),
                   jax.ShapeDtypeStruct((b, l, SSD_CONV_DIM), F32),
                   jax.ShapeDtypeStruct((b, l, DT_PAD), F32),
                   jax.ShapeDtypeStruct((b, l, RET_DIM), BF16),
                   jax.ShapeDtypeStruct((b, l, RET_DIM), BF16),
                   jax.ShapeDtypeStruct((b, l, RET_DIM), BF16)),
        grid=(b, nt),
        in_specs=[tok(D_MODEL),
                  pl.BlockSpec((1, 6, D_MODEL), lambda bi, i: (jnp.where(i == nl, 4, bi), 0, 0)),
                  const((1, D_MODEL)),
                  const(wqkv.shape), const(qkw.shape),
                  pl.BlockSpec((ATTN_HALF, TILE), lambda bi, i: (0, i)),
                  pl.BlockSpec((ATTN_HALF, TILE), lambda bi, i: (0, i)),
                  const(wxbc.shape), const(wdt.shape), const(dtb.shape), const(wr.shape),
                  pl.BlockSpec((TILE, RET_DK), lambda bi, i: (i, 0)),
                  pl.BlockSpec((TILE, RET_DK), lambda bi, i: (i, 0))],
        out_specs=(tok_t(ATTN_Q_DIM), tok(ATTN_KV_DIM), tok_t(ATTN_KV_DIM), tok(SSD_CONV_DIM), tok(DT_PAD),
                   tok(RET_DIM), tok(RET_DIM), tok(RET_DIM)),
        compiler_params=pltpu.CompilerParams(dimension_semantics=("arbitrary", "arbitrary"),
                                             vmem_limit_bytes=_vmem_limit(weights_bytes)),
        name="inproj",
    )(h, mods, n1w, wqkv, qkw, cos_a, sin_a, wxbc, wdt, dtb, wr, cos_r, sin_r)


def _attn_kernel(qt_ref, k_ref, vt_ref, o_ref, qpad_sc, m_sc, acc_sc, s0_sc, s1_sc, mx0_sc, mx1_sc, *, nl, kc):
    kv = pl.program_id(1)
    i = pl.program_id(2)
    nch = (nl + 1) * TILE // kc

    row = lax.broadcasted_iota(jnp.int32, (2 * ATTN_HEAD_DIM, TILE), 0)
    keep = (row >= ATTN_HEAD_DIM).astype(jnp.int32) == kv
    for g in range(ATTN_GROUP):
        qg = qt_ref[0, g * ATTN_HEAD_DIM:(g + 1) * ATTN_HEAD_DIM, :]
        qq = jnp.concatenate([qg, qg], axis=0)
        qpad_sc[:, g * TILE:(g + 1) * TILE] = jnp.where(keep, qq, jnp.zeros_like(qq))
    m_sc[...] = jnp.full(m_sc.shape, NEG_BIG, F32)
    acc_sc[...] = jnp.zeros(acc_sc.shape, F32)

    def scores(off, size):
        return _dot(k_ref[0, pl.ds(off, size), :], qpad_sc[...])

    def accumulate(off, size, s, mx):
        m_old = m_sc[...]
        m_new = jnp.maximum(m_old, mx)
        p = jnp.exp2(s - m_new).astype(BF16)
        v_aug = jnp.concatenate([vt_ref[0, :, pl.ds(off, size)], jnp.ones((ATTN_ONES_ROWS, size), BF16)], axis=0)
        acc_sc[...] = jnp.exp2(m_old - m_new) * acc_sc[...] + _dot(v_aug, p)
        m_sc[...] = m_new

    def qk(c, s_sc, mx_sc):
        s = scores(pl.multiple_of(c * kc, kc), kc)
        s_sc[...] = s
        mx_sc[...] = jnp.max(s, axis=0, keepdims=True)

    def pv(c, s_sc, mx_sc):
        accumulate(pl.multiple_of(c * kc, kc), kc, s_sc[...], mx_sc[...])

    @pl.when(i != nl)
    def _():
        qk(0, s0_sc, mx0_sc)

        def body(j, carry):
            qk(2 * j + 1, s1_sc, mx1_sc)
            pv(2 * j, s0_sc, mx0_sc)
            qk(2 * j + 2, s0_sc, mx0_sc)
            pv(2 * j + 1, s1_sc, mx1_sc)
            return carry
        pairs = (nch - 1) // 2
        lax.fori_loop(0, pairs, body, 0)
        if nch - 1 == 2 * pairs:
            pv(nch - 1, s0_sc, mx0_sc)
        else:
            qk(nch - 1, s1_sc, mx1_sc)
            pv(nch - 2, s0_sc, mx0_sc)
            pv(nch - 1, s1_sc, mx1_sc)

    @pl.when(i == nl)
    def _():
        s = scores(nl * TILE, TILE)
        accumulate(nl * TILE, TILE, s, jnp.max(s, axis=0, keepdims=True))

    acc = acc_sc[...]
    out = acc[:ATTN_HEAD_DIM] / acc[ATTN_HEAD_DIM:ATTN_HEAD_DIM + 1]
    for g in range(ATTN_GROUP):
        o_ref[0, g * ATTN_HEAD_DIM:(g + 1) * ATTN_HEAD_DIM, :] = out[:, g * TILE:(g + 1) * TILE].astype(BF16)


def _attn_call(qt, k, vt, nl):
    b, _, l = qt.shape
    nt = l // TILE
    kc = max(d * TILE for d in range(1, ATTN_KEY_CHUNK_MAX // TILE + 1) if nt % d == 0)
    gq = ATTN_GROUP * ATTN_HEAD_DIM
    wide = ATTN_GROUP * TILE
    resident = 2 * (l * ATTN_KV_DIM + ATTN_HEAD_DIM * l) + 4 * kc * wide
    return pl.pallas_call(
        functools.partial(_attn_kernel, nl=nl, kc=kc),
        out_shape=jax.ShapeDtypeStruct((b, ATTN_Q_DIM, l), BF16),
        grid=(b, ATTN_KV_HEADS, nt),
        in_specs=[pl.BlockSpec((1, gq, TILE), lambda bi, kv, i: (bi, kv, i)),
                  pl.BlockSpec((1, l, ATTN_KV_DIM), lambda bi, kv, i: (bi, 0, 0)),
                  pl.BlockSpec((1, ATTN_HEAD_DIM, l), lambda bi, kv, i: (bi, kv, 0))],
        out_specs=pl.BlockSpec((1, gq, TILE), lambda bi, kv, i: (bi, kv, i)),
        scratch_shapes=[pltpu.VMEM((2 * ATTN_HEAD_DIM, wide), BF16),
                        pltpu.VMEM((1, wide), F32),
                        pltpu.VMEM((ATTN_HEAD_DIM + ATTN_ONES_ROWS, wide), F32),
                        pltpu.VMEM((kc, wide), F32), pltpu.VMEM((kc, wide), F32),
                        pltpu.VMEM((1, wide), F32), pltpu.VMEM((1, wide), F32)],
        compiler_params=pltpu.CompilerParams(dimension_semantics=("arbitrary",) * 3,
                                             vmem_limit_bytes=_vmem_limit(resident)),
        name="attn",
    )(qt, k, vt)


def _scan_tile(s, nl, reverse):
    return jnp.where(s == 0, nl, nl - s) if reverse else jnp.where(s == 0, nl, s - 1)


def _tri(reverse):
    r = lax.broadcasted_iota(jnp.int32, (CHUNK, CHUNK), 0)
    c = lax.broadcasted_iota(jnp.int32, (CHUNK, CHUNK), 1)
    return (c >= r) if reverse else (c <= r)


def _lane_bcast(x, c):
    return jnp.broadcast_to(x[:, c:c + 1], (x.shape[0], V7X_LANES))


def _ssd_kernel(x_ref, xp_ref, xn_ref, dt_ref, cw_ref, cb_ref, alog_ref, dskip_ref, y_ref, st_sc,
                *, nl, reverse):
    s = pl.program_id(1)
    t = _scan_tile(s, nl, reverse)

    @pl.when(s == 0)
    def _():
        st_sc[...] = jnp.zeros(st_sc.shape, F32)

    x = x_ref[0]
    row = lax.broadcasted_iota(jnp.int32, (TILE, SSD_CONV_DIM), 0)
    has_prev = jnp.logical_and(t != nl, t != 0)
    has_next = t < nl - 1
    prev_row = jnp.where(has_prev, xp_ref[0, 7:8, :], 0.0)
    next_row = jnp.where(has_next, xn_ref[0, 0:1, :], 0.0)
    x_m1 = jnp.where(row == 0, prev_row, pltpu.roll(x, 1, 0))
    x_p1 = jnp.where(row == TILE - 1, next_row, pltpu.roll(x, TILE - 1, 0))
    xs = _silu(cw_ref[0:1, :] * x_m1 + cw_ref[1:2, :] * x + cw_ref[2:3, :] * x_p1 + cb_ref[...])

    tri = _tri(reverse)
    tri_b = tri.astype(BF16)
    lane = lax.broadcasted_iota(jnp.int32, (CHUNK, V7X_LANES), 1)
    left = lane < SSD_HEAD_DIM
    col0 = SSD_HEADS if reverse else 0
    a_neg = -jnp.exp(alog_ref[...])
    last = 0 if reverse else CHUNK - 1

    chunks = range(TILE // CHUNK)
    for ci in (reversed(chunks) if reverse else chunks):
        r0 = ci * CHUNK
        xc = xs[r0:r0 + CHUNK]
        dtc = dt_ref[0, r0:r0 + CHUNK, :]
        a = dtc * a_neg
        hi, mid, lo = _split3(a)
        a_cum = _dot(tri_b, hi) + _dot(tri_b, mid) + _dot(tri_b, lo)
        a_cum_t = jnp.transpose(a_cum)
        for g in range(SSD_GROUPS):
            bm = xc[:, SSD_D_INNER + g * SSD_STATE:SSD_D_INNER + (g + 1) * SSD_STATE]
            cm = xc[:, SSD_D_INNER + (SSD_GROUPS + g) * SSD_STATE:SSD_D_INNER + (SSD_GROUPS + g + 1) * SSD_STATE]
            bmb = bm.astype(BF16)
            cmb = cm.astype(BF16)
            cb = _dot_nt(cmb, bmb)
            st_prev = st_sc[g]
            y_off = _dot(cmb, st_prev.astype(BF16))
            xdd_pairs = []
            tot = []
            for pr in range(SSD_HEADS_PER_GROUP // 2):
                h0 = g * SSD_HEADS_PER_GROUP + 2 * pr
                lanes0 = h0 * SSD_HEAD_DIM
                x2 = xc[:, lanes0:lanes0 + V7X_LANES]
                acol = [_lane_bcast(a_cum, col0 + h0 + j) for j in range(2)]
                dcol = [_lane_bcast(dtc, col0 + h0 + j) for j in range(2)]
                a2 = jnp.where(left, acol[0], acol[1])
                xd2 = x2 * jnp.where(left, dcol[0], dcol[1])
                xd2b = xd2.astype(BF16)
                y_pair = []
                for j in range(2):
                    arow = a_cum_t[col0 + h0 + j:col0 + h0 + j + 1, :]
                    lmat = jnp.where(tri, jnp.exp(jnp.where(tri, acol[j] - arow, 0.0)), 0.0)
                    y_pair.append(_dot((cb * lmat).astype(BF16), xd2b))
                y2 = jnp.where(left, y_pair[0], y_pair[1])
                y2 = y2 + jnp.exp(a2) * y_off[:, pr * V7X_LANES:(pr + 1) * V7X_LANES]
                if not reverse:
                    y2 = y2 + dskip_ref[:, lanes0:lanes0 + V7X_LANES] * x2
                y_ref[0, r0:r0 + CHUNK, lanes0:lanes0 + V7X_LANES] = y2
                a_tot = a2[last:last + 1, :]
                xdd_pairs.append((xd2 * jnp.exp(a_tot - a2)).astype(BF16))
                tot.append(a_tot)
            xdd = jnp.concatenate(xdd_pairs, axis=1)
            st_sc[g] = jnp.exp(jnp.concatenate(tot, axis=1)) * st_prev + _dot_tn(bmb, xdd)


def _ssd_call(xbc, dt, cw, cb, alog, dskip, nl, reverse):
    b, l, _ = xbc.shape
    nt = l // TILE
    rows8 = TILE // 8
    tile_of = lambda s: _scan_tile(s, nl, reverse)
    const = lambda shape: pl.BlockSpec(shape, lambda bi, s: (0,) * len(shape))
    return pl.pallas_call(
        functools.partial(_ssd_kernel, nl=nl, reverse=reverse),
        out_shape=jax.ShapeDtypeStruct((b, l, SSD_D_INNER), F32),
        grid=(b, nt),
        in_specs=[pl.BlockSpec((1, TILE, SSD_CONV_DIM), lambda bi, s: (bi, tile_of(s), 0)),
                  pl.BlockSpec((1, 8, SSD_CONV_DIM), lambda bi, s: (bi, jnp.maximum(tile_of(s) * rows8 - 1, 0), 0)),
                  pl.BlockSpec((1, 8, SSD_CONV_DIM),
                               lambda bi, s: (bi, jnp.minimum((tile_of(s) + 1) * rows8, nt * rows8 - 1), 0)),
                  pl.BlockSpec((1, TILE, DT_PAD), lambda bi, s: (bi, tile_of(s), 0)),
                  const(cw.shape), const(cb.shape), const(alog.shape), const(dskip.shape)],
        out_specs=pl.BlockSpec((1, TILE, SSD_D_INNER), lambda bi, s: (bi, tile_of(s), 0)),
        scratch_shapes=[pltpu.VMEM((SSD_GROUPS, SSD_STATE, SSD_HEADS_PER_GROUP * SSD_HEAD_DIM), F32)],
        compiler_params=pltpu.CompilerParams(dimension_semantics=("arbitrary", "arbitrary")),
        name="ssd_bwd" if reverse else "ssd_fwd",
    )(xbc, xbc, xbc, dt, cw, cb, alog, dskip)


def _ret_kernel(q_ref, k_ref, v_ref, ld_ref, y_ref, st_sc, *, nl, reverse):
    s = pl.program_id(1)

    @pl.when(s == 0)
    def _():
        st_sc[...] = jnp.zeros(st_sc.shape, F32)

    tri = _tri(reverse)
    r = lax.broadcasted_iota(jnp.int32, (CHUNK, CHUNK), 0)
    c = lax.broadcasted_iota(jnp.int32, (CHUNK, CHUNK), 1)
    dist = jnp.maximum((c - r) if reverse else (r - c), 0).astype(F32)
    pos = ((CHUNK - 1 - r) if reverse else r).astype(F32)
    col0 = RET_HEADS if reverse else 0

    chunks = range(TILE // CHUNK)
    for ci in (reversed(chunks) if reverse else chunks):
        r0 = ci * CHUNK
        for hd in range(RET_HEADS):
            c0 = hd * RET_DK
            lg = jnp.broadcast_to(-jnp.exp(ld_ref[0:1, col0 + hd:col0 + hd + 1]), (CHUNK, CHUNK))
            q = q_ref[0, r0:r0 + CHUNK, c0:c0 + RET_DK]
            k = k_ref[0, r0:r0 + CHUNK, c0:c0 + RET_DK]
            v = v_ref[0, r0:r0 + CHUNK, c0:c0 + RET_DV]
            dmat = jnp.where(tri, jnp.exp(dist * lg), 0.0)
            sc = _dot_nt(q, k) * dmat
            st_prev = st_sc[hd]
            y = _dot(sc.astype(BF16), v) + _dot(q, st_prev.astype(BF16)) * jnp.exp((pos + 1.0) * lg)
            y_ref[0, r0:r0 + CHUNK, c0:c0 + RET_DV] = y
            k_dec = (k.astype(F32) * jnp.exp((CHUNK - 1.0 - pos) * lg)).astype(BF16)
            st_sc[hd] = jnp.exp(CHUNK * lg) * st_prev + _dot_tn(k_dec, v)


def _ret_call(rq, rk, rv, ld, nl, reverse):
    b, l, _ = rq.shape
    nt = l // TILE
    tok = pl.BlockSpec((1, TILE, RET_DIM), lambda bi, s: (bi, _scan_tile(s, nl, reverse), 0))
    return pl.pallas_call(
        functools.partial(_ret_kernel, nl=nl, reverse=reverse),
        out_shape=jax.ShapeDtypeStruct((b, l, RET_DIM), F32),
        grid=(b, nt),
        in_specs=[tok, tok, tok, pl.BlockSpec(ld.shape, lambda bi, s: (0, 0))],
        out_specs=tok,
        scratch_shapes=[pltpu.VMEM((RET_HEADS, RET_DK, RET_DV), F32)],
        compiler_params=pltpu.CompilerParams(dimension_semantics=("arbitrary", "arbitrary")),
        name="ret_bwd" if reverse else "ret_fwd",
    )(rq, rk, rv, ld)


def _merge_kernel(h_ref, mod_ref, n1w_ref, ot_ref, sf_ref, sb_ref, rf_ref, rb_ref, wzg_ref, wgate_ref,
                  snw_ref, gnw_ref, wb_ref, wout_ref, o_ref):
    h = h_ref[0]
    u = _rms_rows(h, n1w_ref[...]) * (1.0 + mod_ref[0, 1:2, :]) + mod_ref[0, 0:1, :]
    ub = u.astype(BF16)
    zg = _dot(ub, wzg_ref[...])
    gates = jax.nn.sigmoid(_dot(ub, wgate_ref[...]))

    br_attn = _dot_tn(ot_ref[0], wb_ref[0])

    y = (sf_ref[0] + sb_ref[0]) * _silu(zg[:, :SSD_D_INNER])
    br_ssd = _dot(_rms_rows(y, snw_ref[...]).astype(BF16), wb_ref[1])

    yr = rf_ref[0] + rb_ref[0]
    heads = []
    for hd in range(RET_HEADS):
        yh = yr[:, hd * RET_DV:(hd + 1) * RET_DV]
        yc = yh - jnp.mean(yh, axis=-1, keepdims=True)
        heads.append(yc * lax.rsqrt(jnp.mean(yc * yc, axis=-1, keepdims=True) + NORM_EPS))
    yn = jnp.concatenate(heads, axis=1) * gnw_ref[...] * _silu(zg[:, SSD_D_INNER:])
    br_ret = _dot(yn.astype(BF16), wb_ref[2])

    merged = (gates[:, :D_MODEL] * br_attn + gates[:, D_MODEL:2 * D_MODEL] * br_ssd
              + gates[:, 2 * D_MODEL:] * br_ret)
    o_ref[0] = h + mod_ref[0, 2:3, :] * _dot(merged.astype(BF16), wout_ref[...])


def _merge_call(h, mods, n1w, ot, sf, sb, rf, rb, wts, nl):
    b, l, _ = h.shape
    nt = l // TILE
    wzg, wgate, snw, gnw, wb, wout = wts
    const = lambda shape: pl.BlockSpec(shape, lambda bi, i: (0,) * len(shape))
    tok = lambda c: pl.BlockSpec((1, TILE, c), lambda bi, i: (bi, i, 0))
    weights_bytes = 2 * (wzg.size + wgate.size + wb.size + wout.size)
    return pl.pallas_call(
        _merge_kernel,
        out_shape=jax.ShapeDtypeStruct((b, l, D_MODEL), F32),
        grid=(b, nt),
        in_specs=[tok(D_MODEL),
                  pl.BlockSpec((1, 6, D_MODEL), lambda bi, i: (jnp.where(i == nl, 4, bi), 0, 0)),
                  const((1, D_MODEL)),
                  pl.BlockSpec((1, ATTN_Q_DIM, TILE), lambda bi, i: (bi, 0, i)),
                  tok(SSD_D_INNER), tok(SSD_D_INNER), tok(RET_DIM), tok(RET_DIM),
                  const(wzg.shape), const(wgate.shape), const(snw.shape), const(gnw.shape),
                  const(wb.shape), const(wout.shape)],
        out_specs=tok(D_MODEL),
        compiler_params=pltpu.CompilerParams(dimension_semantics=("arbitrary", "arbitrary"),
                                             vmem_limit_bytes=_vmem_limit(weights_bytes)),
        name="merge",
    )(h, mods, n1w, ot, sf, sb, rf, rb, wzg, wgate, snw, gnw, wb, wout)


def _mlp_kernel(h_ref, mod_ref, n2w_ref, w1_ref, w2_ref, fw_ref, o_ref, *, final):
    h = h_ref[0]
    v = _rms_rows(h, n2w_ref[...]) * (1.0 + mod_ref[0, 4:5, :]) + mod_ref[0, 3:4, :]
    a = jnp.maximum(_dot(v.astype(BF16), w1_ref[...]), 0.0)
    out = h + mod_ref[0, 5:6, :] * _dot((a * a).astype(BF16), w2_ref[...])
    o_ref[0] = _rms_rows(out, fw_ref[...]) if final else out


def _mlp_call(h, mods, n2w, w1, w2, fw, nl, final):
    b, l, _ = h.shape
    nt = nl if final else l // TILE
    const = lambda shape: pl.BlockSpec(shape, lambda bi, i: (0,) * len(shape))
    tok = pl.BlockSpec((1, TILE, D_MODEL), lambda bi, i: (bi, i, 0))
    return pl.pallas_call(
        functools.partial(_mlp_kernel, final=final),
        out_shape=jax.ShapeDtypeStruct((b, nt * TILE, D_MODEL), F32),
        grid=(b, nt),
        in_specs=[tok,
                  pl.BlockSpec((1, 6, D_MODEL), lambda bi, i: (jnp.where(i == nl, 4, bi), 0, 0)),
                  const((1, D_MODEL)), const(w1.shape), const(w2.shape), const((1, D_MODEL))],
        out_specs=tok,
        compiler_params=pltpu.CompilerParams(dimension_semantics=("arbitrary", "arbitrary"),
                                             vmem_limit_bytes=_vmem_limit(2 * (w1.size + w2.size))),
        name="mlp_final" if final else "mlp",
    )(h, mods, n2w, w1, w2, fw)


def _rope_tables(n, m):
    rows = n // GRID_W
    row = jnp.repeat(jnp.arange(rows, dtype=F32), GRID_W)
    col = jnp.tile(jnp.arange(GRID_W, dtype=F32), rows)
    inv = ROPE_THETA ** (-jnp.arange(ATTN_AXIS_FREQS, dtype=F32) / ATTN_AXIS_FREQS)
    ang = jnp.concatenate([row[:, None] * inv, col[:, None] * inv], axis=-1)
    cos_a = jnp.concatenate([jnp.cos(ang), jnp.ones((m, ATTN_HALF), F32)], axis=0).T
    sin_a = jnp.concatenate([jnp.sin(ang), jnp.zeros((m, ATTN_HALF), F32)], axis=0).T
    pos = jnp.concatenate([jnp.arange(n, dtype=F32) + m, jnp.arange(m, dtype=F32)])
    inv_r = ROPE_THETA ** (-jnp.linspace(0.0, 1.0, RET_DK // 2, dtype=F32))
    ang_r = pos[:, None] * inv_r
    cos_r = jnp.concatenate([jnp.cos(ang_r), jnp.cos(ang_r)], axis=-1)
    sin_r = jnp.concatenate([-jnp.sin(ang_r), jnp.sin(ang_r)], axis=-1)
    return cos_a, sin_a, cos_r, sin_r


def _layer_weights(w_in, q_norm, k_norm, dt_bias):
    offs = [0]
    for sz in IN_SPLITS:
        offs.append(offs[-1] + sz)
    col = lambda j: w_in[:, offs[j]:offs[j + 1]]
    wqkv = jnp.concatenate([col(0), col(1), col(2)], axis=1).T.astype(BF16)
    qkw = jnp.concatenate([jnp.tile(q_norm, ATTN_HEADS), jnp.tile(k_norm, ATTN_KV_HEADS)])
    qkw = jnp.broadcast_to(qkw[:, None], (ATTN_Q_DIM + ATTN_KV_DIM, TILE)).astype(F32)
    wxbc = col(4).astype(BF16)
    wdt = jnp.pad(col(5), ((0, 0), (0, DT_PAD - 2 * SSD_HEADS))).astype(BF16)
    dtb = jnp.pad(dt_bias.reshape(1, -1), ((0, 0), (0, DT_PAD - 2 * SSD_HEADS))).astype(F32)
    wr = jnp.concatenate([col(6), col(7), col(8)], axis=1).astype(BF16)
    wzg = jnp.concatenate([col(3), col(9)], axis=1).astype(BF16)
    wgate = col(10).astype(BF16)
    return (wqkv, qkw, wxbc, wdt, dtb, wr), (wzg, wgate)


def _pad_lanes(v):
    v = v.reshape(1, -1).astype(F32)
    return jnp.pad(v, ((0, 0), (0, V7X_LANES - v.shape[1])))


def kernel(x, c, ctx, c_ctx, w_mod, b_mod, norm1_w, norm2_w, w_in, attn_q_norm, attn_k_norm, ssd_conv_w,
           ssd_conv_b, ssd_dt_bias, ssd_a_log, ssd_d, ssd_norm_w, ret_log_decay, ret_gn_w, w_branch, w_out,
           w_mlp1, w_mlp2, final_norm_w):
    b, n, d = x.shape
    m = ctx.shape[1]
    depth = w_in.shape[0]
    assert d == D_MODEL and m == TILE and n % TILE == 0 and n % GRID_W == 0 and b <= 4
    nl = n // TILE

    tabs = _rope_tables(n, m)
    cc = jnp.zeros((8, D_MODEL), F32).at[:b].set(c).at[4].set(c_ctx)
    h = jnp.concatenate([x, ctx], axis=1)

    for layer in range(depth):
        final = layer == depth - 1
        in_w, (wzg, wgate) = _layer_weights(w_in[layer], attn_q_norm[layer], attn_k_norm[layer],
                                            ssd_dt_bias[layer])
        n1w = norm1_w[layer].reshape(1, -1)
        mods = _mod_call(cc, w_mod[layer], b_mod[layer])

        qt, k, vt, xbc, dt, rq, rk, rv = _inproj_call(h, mods, n1w, in_w, tabs, nl)
        ot = _attn_call(qt, k, vt, nl)

        cw = jnp.pad(ssd_conv_w[layer], ((0, 8 - SSD_CONV_K), (0, 0)))
        cb = ssd_conv_b[layer].reshape(1, -1)
        alog = _pad_lanes(ssd_a_log[layer])
        dskip = jnp.repeat(ssd_d[layer], SSD_HEAD_DIM).reshape(1, -1)
        sf = _ssd_call(xbc, dt, cw, cb, alog, dskip, nl, False)
        sb = _ssd_call(xbc, dt, cw, cb, alog, dskip, nl, True)

        ld = _pad_lanes(ret_log_decay[layer])
        rf = _ret_call(rq, rk, rv, ld, nl, False)
        rb = _ret_call(rq, rk, rv, ld, nl, True)

        merge_w = (wzg, wgate, ssd_norm_w[layer].reshape(1, -1), ret_gn_w[layer].reshape(1, -1),
                   w_branch[layer].astype(BF16), w_out[layer].astype(BF16))
        h = _merge_call(h, mods, n1w, ot, sf, sb, rf, rb, merge_w, nl)
        h = _mlp_call(h, mods, norm2_w[layer].reshape(1, -1), w_mlp1[layer].astype(BF16),
                      w_mlp2[layer].astype(BF16), final_norm_w.reshape(1, -1), nl, final)
    return h
```

```python
import functools
import math

import jax
import jax.numpy as jnp
from jax import lax
from jax.experimental import pallas as pl
from jax.experimental.pallas import tpu as pltpu

F32 = jnp.float32
BF16 = jnp.bfloat16

D_MODEL = 1024
GRID_W = 64
NORM_EPS = 1e-6
ROPE_THETA = 10000.0

ATTN_HEADS = 8
ATTN_KV_HEADS = 2
ATTN_GROUP = ATTN_HEADS // ATTN_KV_HEADS
ATTN_HEAD_DIM = 64
ATTN_HALF = ATTN_HEAD_DIM // 2
ATTN_AXIS_FREQS = ATTN_HEAD_DIM // 4
ATTN_Q_DIM = ATTN_HEADS * ATTN_HEAD_DIM
ATTN_KV_DIM = ATTN_KV_HEADS * ATTN_HEAD_DIM

SSD_HEADS = 8
SSD_HEAD_DIM = 64
SSD_D_INNER = SSD_HEADS * SSD_HEAD_DIM
SSD_GROUPS = 2
SSD_STATE = 128
SSD_CONV_K = 3
SSD_CONV_DIM = SSD_D_INNER + 2 * SSD_GROUPS * SSD_STATE
SSD_HEADS_PER_GROUP = SSD_HEADS // SSD_GROUPS

RET_HEADS = 4
RET_DK = 128
RET_DV = 128
RET_DIM = RET_HEADS * RET_DK

N_BRANCH = 3
BRANCH_W = 512
MLP_HIDDEN = 4 * D_MODEL

IN_SPLITS = (ATTN_Q_DIM, ATTN_KV_DIM, ATTN_KV_DIM, SSD_D_INNER, SSD_CONV_DIM, 2 * SSD_HEADS,
             RET_DIM, RET_DIM, RET_HEADS * RET_DV, RET_HEADS * RET_DV, N_BRANCH * D_MODEL)

V7X_LANES = 128
V7X_VMEM_BYTES = 64 * 1024 * 1024

TILE = 256
CHUNK = 128
DT_PAD = V7X_LANES
NEG_BIG = -1e30
ATTN_KEY_CHUNK_MAX = 1024
ATTN_PV_TILE = 256
ATTN_ONES_ROWS = 16


def _vmem_limit(resident_bytes):
    return int(min(V7X_VMEM_BYTES - 8 * 1024 * 1024, 2 * resident_bytes + 16 * 1024 * 1024))


def _silu(x):
    return x * jax.nn.sigmoid(x)


def _rms_rows(x, w):
    return x * lax.rsqrt(jnp.mean(x * x, axis=-1, keepdims=True) + NORM_EPS) * w


def _dot(a, b):
    return jnp.dot(a, b, preferred_element_type=F32)


def _dot_nt(a, b):
    return lax.dot_general(a, b, (((1,), (1,)), ((), ())), preferred_element_type=F32)


def _dot_tn(a, b):
    return lax.dot_general(a, b, (((0,), (0,)), ((), ())), preferred_element_type=F32)


def _split3(x):
    hi = x.astype(BF16)
    r1 = x - hi.astype(F32)
    mid = r1.astype(BF16)
    lo = (r1 - mid.astype(F32)).astype(BF16)
    return hi, mid, lo


def _mod_kernel(c_ref, w_ref, b_ref, o_ref):
    s = _silu(c_ref[...])
    o_ref[...] = jnp.dot(s, w_ref[...], preferred_element_type=F32,
                         precision=lax.Precision.HIGHEST) + b_ref[...]


def _mod_call(cc, w_mod, b_mod):
    nblk = w_mod.shape[1] // D_MODEL
    out = pl.pallas_call(
        _mod_kernel,
        out_shape=jax.ShapeDtypeStruct((8, nblk * D_MODEL), F32),
        grid=(nblk,),
        in_specs=[pl.BlockSpec((8, D_MODEL), lambda j: (0, 0)),
                  pl.BlockSpec((D_MODEL, D_MODEL), lambda j: (0, j)),
                  pl.BlockSpec((1, D_MODEL), lambda j: (0, j))],
        out_specs=pl.BlockSpec((8, D_MODEL), lambda j: (0, j)),
        compiler_params=pltpu.CompilerParams(dimension_semantics=("arbitrary",)),
        name="mod",
    )(cc, w_mod, b_mod.reshape(1, -1))
    return out.reshape(8, nblk, D_MODEL)


def _inproj_kernel(h_ref, mod_ref, n1w_ref, wqkv_ref, qkw_ref, cos_ref, sin_ref, wxbc_ref, wdt_ref,
                   dtb_ref, wr_ref, cosr_ref, sinr_ref,
                   qt_ref, k_ref, vt_ref, xbc_ref, dt_ref, rq_ref, rk_ref, rv_ref):
    h = h_ref[0]
    u = _rms_rows(h, n1w_ref[...]) * (1.0 + mod_ref[0, 1:2, :]) + mod_ref[0, 0:1, :]
    ub = u.astype(BF16)

    qkv_t = _dot_nt(wqkv_ref[...], ub)
    cos = cos_ref[...]
    sin = sin_ref[...]
    k_rows = []
    for hd in range(ATTN_HEADS + ATTN_KV_HEADS):
        r0 = hd * ATTN_HEAD_DIM
        xh = qkv_t[r0:r0 + ATTN_HEAD_DIM]
        yh = xh * lax.rsqrt(jnp.mean(xh * xh, axis=0, keepdims=True) + NORM_EPS) * qkw_ref[r0:r0 + ATTN_HEAD_DIM, :]
        y1 = yh[:ATTN_HALF]
        y2 = yh[ATTN_HALF:]
        o1 = y1 * cos - y2 * sin
        o2 = y1 * sin + y2 * cos
        if hd < ATTN_HEADS:
            scale = ATTN_HEAD_DIM ** -0.5 * math.log2(math.e)
            qt_ref[0, r0:r0 + ATTN_HALF, :] = (o1 * scale).astype(BF16)
            qt_ref[0, r0 + ATTN_HALF:r0 + ATTN_HEAD_DIM, :] = (o2 * scale).astype(BF16)
        else:
            k_rows += [o1, o2]
    k_t = jnp.concatenate(k_rows, axis=0)
    k_ref[0] = jnp.transpose(k_t).astype(BF16)
    vt_ref[0] = qkv_t[ATTN_Q_DIM + ATTN_KV_DIM:].astype(BF16)

    xbc_ref[0] = _dot(ub, wxbc_ref[...])
    dt_raw = _dot(ub, wdt_ref[...]) + dtb_ref[...]
    dt_ref[0] = jnp.maximum(dt_raw, 0.0) + jnp.log1p(jnp.exp(-jnp.abs(dt_raw)))

    r = _dot(ub, wr_ref[...])
    cosr = cosr_ref[...]
    sinr = sinr_ref[...]
    for hd in range(RET_HEADS):
        c0 = hd * RET_DK
        qh = r[:, c0:c0 + RET_DK]
        kh = r[:, RET_DIM + c0:RET_DIM + c0 + RET_DK]
        rq_ref[0, :, c0:c0 + RET_DK] = (qh * cosr + pltpu.roll(qh, RET_DK // 2, 1) * sinr).astype(BF16)
        rk_ref[0, :, c0:c0 + RET_DK] = ((kh * cosr + pltpu.roll(kh, RET_DK // 2, 1) * sinr)
                                        * (RET_DK ** -0.5)).astype(BF16)
    rv_ref[0] = r[:, 2 * RET_DIM:].astype(BF16)


def _inproj_call(h, mods, n1w, wts, tabs, nl):
    b, l, _ = h.shape
    nt = l // TILE
    wqkv, qkw, wxbc, wdt, dtb, wr = wts
    cos_a, sin_a, cos_r, sin_r = tabs
    const = lambda shape: pl.BlockSpec(shape, lambda bi, i: (0,) * len(shape))
    tok = lambda c: pl.BlockSpec((1, TILE, c), lambda bi, i: (bi, i, 0))
    tok_t = lambda r: pl.BlockSpec((1, r, TILE), lambda bi, i: (bi, 0, i))
    weights_bytes = 2 * (wqkv.size + wxbc.size + wdt.size + wr.size) + 4 * qkw.size
    return pl.pallas_call(
        _inproj_kernel,
        out_shape=(jax.ShapeDtypeStruct((b, ATTN_Q_DIM, l), BF16),
                   jax.ShapeDtypeStruct((b, l, ATTN_KV_DIM), BF16),
                   jax.ShapeDtypeStruct((b, ATTN_KV_DIM, l), BF16),
                   jax.ShapeDtypeStruct((b, l, SSD_CONV_DIM), F32),
                   jax.ShapeDtypeStruct((b, l, DT_PAD), F32),
                   jax.ShapeDtypeStruct((b, l, RET_DIM), BF16),
                   jax.ShapeDtypeStruct((b, l, RET_DIM), BF16),
                   jax.ShapeDtypeStruct((b, l, RET_DIM), BF16)),
        grid=(b, nt),
        in_specs=[tok(D_MODEL),
                  pl.BlockSpec((1, 6, D_MODEL), lambda bi, i: (jnp.where(i == nl, 4, bi), 0, 0)),
                  const((1, D_MODEL)),
                  const(wqkv.shape), const(qkw.shape),
                  pl.BlockSpec((ATTN_HALF, TILE), lambda bi, i: (0, i)),
                  pl.BlockSpec((ATTN_HALF, TILE), lambda bi, i: (0, i)),
                  const(wxbc.shape), const(wdt.shape), const(dtb.shape), const(wr.shape),
                  pl.BlockSpec((TILE, RET_DK), lambda bi, i: (i, 0)),
                  pl.BlockSpec((TILE, RET_DK), lambda bi, i: (i, 0))],
        out_specs=(tok_t(ATTN_Q_DIM), tok(ATTN_KV_DIM), tok_t(ATTN_KV_DIM), tok(SSD_CONV_DIM), tok(DT_PAD),
                   tok(RET_DIM), tok(RET_DIM), tok(RET_DIM)),
        compiler_params=pltpu.CompilerParams(dimension_semantics=("arbitrary", "arbitrary"),
                                             vmem_limit_bytes=_vmem_limit(weights_bytes)),
        name="inproj",
    )(h, mods, n1w, wqkv, qkw, cos_a, sin_a, wxbc, wdt, dtb, wr, cos_r, sin_r)


def _attn_kernel(qt_ref, k_ref, vt_ref, o_ref, qpad_sc, m_sc, acc_sc, s0_sc, s1_sc, mx0_sc, mx1_sc, *, nl, kc):
    kv = pl.program_id(1)
    i = pl.program_id(2)
    nch = (nl + 1) * TILE // kc

    row = lax.broadcasted_iota(jnp.int32, (2 * ATTN_HEAD_DIM, TILE), 0)
    keep = (row >= ATTN_HEAD_DIM).astype(jnp.int32) == kv
    for g in range(ATTN_GROUP):
        qg = qt_ref[0, g * ATTN_HEAD_DIM:(g + 1) * ATTN_HEAD_DIM, :]
        qq = jnp.concatenate([qg, qg], axis=0)
        qpad_sc[:, g * TILE:(g + 1) * TILE] = jnp.where(keep, qq, jnp.zeros_like(qq))
    m_sc[...] = jnp.full(m_sc.shape, NEG_BIG, F32)
    acc_sc[...] = jnp.zeros(acc_sc.shape, F32)

    blocks = range(0, ATTN_GROUP * TILE, ATTN_PV_TILE)

    def scores(off, size, c0):
        return _dot(k_ref[0, pl.ds(off, size), :], qpad_sc[:, c0:c0 + ATTN_PV_TILE])

    def accumulate(off, size, c0, s, mx):
        cols = slice(c0, c0 + ATTN_PV_TILE)
        m_old = m_sc[:, cols]
        m_new = jnp.maximum(m_old, mx)
        m_sc[:, cols] = m_new
        p = jnp.exp2(s - m_new).astype(BF16)
        v_aug = jnp.concatenate([vt_ref[0, :, pl.ds(off, size)], jnp.ones((ATTN_ONES_ROWS, size), BF16)], axis=0)
        acc_sc[:, cols] = jnp.exp2(m_old - m_new) * acc_sc[:, cols] + _dot(v_aug, p)

    def qk(c, c0, s_sc, mx_sc):
        s = scores(pl.multiple_of(c * kc, kc), kc, c0)
        s_sc[:, c0:c0 + ATTN_PV_TILE] = s
        mx_sc[:, c0:c0 + ATTN_PV_TILE] = jnp.max(s, axis=0, keepdims=True)

    def pv(c, c0, s_sc, mx_sc):
        accumulate(pl.multiple_of(c * kc, kc), kc, c0, s_sc[:, c0:c0 + ATTN_PV_TILE],
                   mx_sc[:, c0:c0 + ATTN_PV_TILE])

    def step(c_qk, qk_bufs, c_pv, pv_bufs):
        for c0 in blocks:
            if c_qk is not None:
                qk(c_qk, c0, *qk_bufs)
            if c_pv is not None:
                pv(c_pv, c0, *pv_bufs)

    buf0 = (s0_sc, mx0_sc)
    buf1 = (s1_sc, mx1_sc)

    @pl.when(i != nl)
    def _():
        step(0, buf0, None, None)

        def body(j, carry):
            step(2 * j + 1, buf1, 2 * j, buf0)
            step(2 * j + 2, buf0, 2 * j + 1, buf1)
            return carry
        pairs = (nch - 1) // 2
        lax.fori_loop(0, pairs, body, 0)
        if nch - 1 == 2 * pairs:
            step(None, None, nch - 1, buf0)
        else:
            step(nch - 1, buf1, nch - 2, buf0)
            step(None, None, nch - 1, buf1)

    @pl.when(i == nl)
    def _():
        for c0 in blocks:
            s = scores(nl * TILE, TILE, c0)
            accumulate(nl * TILE, TILE, c0, s, jnp.max(s, axis=0, keepdims=True))

    acc = acc_sc[...]
    out = acc[:ATTN_HEAD_DIM] / acc[ATTN_HEAD_DIM:ATTN_HEAD_DIM + 1]
    for g in range(ATTN_GROUP):
        o_ref[0, g * ATTN_HEAD_DIM:(g + 1) * ATTN_HEAD_DIM, :] = out[:, g * TILE:(g + 1) * TILE].astype(BF16)


def _attn_call(qt, k, vt, nl):
    b, _, l = qt.shape
    nt = l // TILE
    kc = max(d * TILE for d in range(1, ATTN_KEY_CHUNK_MAX // TILE + 1) if nt % d == 0)
    gq = ATTN_GROUP * ATTN_HEAD_DIM
    wide = ATTN_GROUP * TILE
    resident = 2 * (l * ATTN_KV_DIM + ATTN_HEAD_DIM * l) + 4 * kc * wide
    return pl.pallas_call(
        functools.partial(_attn_kernel, nl=nl, kc=kc),
        out_shape=jax.ShapeDtypeStruct((b, ATTN_Q_DIM, l), BF16),
        grid=(b, ATTN_KV_HEADS, nt),
        in_specs=[pl.BlockSpec((1, gq, TILE), lambda bi, kv, i: (bi, kv, i)),
                  pl.BlockSpec((1, l, ATTN_KV_DIM), lambda bi, kv, i: (bi, 0, 0)),
                  pl.BlockSpec((1, ATTN_HEAD_DIM, l), lambda bi, kv, i: (bi, kv, 0))],
        out_specs=pl.BlockSpec((1, gq, TILE), lambda bi, kv, i: (bi, kv, i)),
        scratch_shapes=[pltpu.VMEM((2 * ATTN_HEAD_DIM, wide), BF16),
                        pltpu.VMEM((1, wide), F32),
                        pltpu.VMEM((ATTN_HEAD_DIM + ATTN_ONES_ROWS, wide), F32),
                        pltpu.VMEM((kc, wide), F32), pltpu.VMEM((kc, wide), F32),
                        pltpu.VMEM((1, wide), F32), pltpu.VMEM((1, wide), F32)],
        compiler_params=pltpu.CompilerParams(dimension_semantics=("arbitrary",) * 3,
                                             vmem_limit_bytes=_vmem_limit(resident)),
        name="attn",
    )(qt, k, vt)


def _scan_tile(s, nl, reverse):
    return jnp.where(s == 0, nl, nl - s) if reverse else jnp.where(s == 0, nl, s - 1)


def _tri(reverse):
    r = lax.broadcasted_iota(jnp.int32, (CHUNK, CHUNK), 0)
    c = lax.broadcasted_iota(jnp.int32, (CHUNK, CHUNK), 1)
    return (c >= r) if reverse else (c <= r)


def _lane_bcast(x, c):
    return jnp.broadcast_to(x[:, c:c + 1], (x.shape[0], V7X_LANES))


def _ssd_kernel(x_ref, xp_ref, xn_ref, dt_ref, cw_ref, cb_ref, alog_ref, dskip_ref, y_ref, st_sc,
                *, nl, reverse):
    s = pl.program_id(1)
    t = _scan_tile(s, nl, reverse)

    @pl.when(s == 0)
    def _():
        st_sc[...] = jnp.zeros(st_sc.shape, F32)

    x = x_ref[0]
    row = lax.broadcasted_iota(jnp.int32, (TILE, SSD_CONV_DIM), 0)
    has_prev = jnp.logical_and(t != nl, t != 0)
    has_next = t < nl - 1
    prev_row = jnp.where(has_prev, xp_ref[0, 7:8, :], 0.0)
    next_row = jnp.where(has_next, xn_ref[0, 0:1, :], 0.0)
    x_m1 = jnp.where(row == 0, prev_row, pltpu.roll(x, 1, 0))
    x_p1 = jnp.where(row == TILE - 1, next_row, pltpu.roll(x, TILE - 1, 0))
    xs = _silu(cw_ref[0:1, :] * x_m1 + cw_ref[1:2, :] * x + cw_ref[2:3, :] * x_p1 + cb_ref[...])

    tri = _tri(reverse)
    tri_b = tri.astype(BF16)
    lane = lax.broadcasted_iota(jnp.int32, (CHUNK, V7X_LANES), 1)
    left = lane < SSD_HEAD_DIM
    col0 = SSD_HEADS if reverse else 0
    a_neg = -jnp.exp(alog_ref[...])
    last = 0 if reverse else CHUNK - 1

    chunks = range(TILE // CHUNK)
    for ci in (reversed(chunks) if reverse else chunks):
        r0 = ci * CHUNK
        xc = xs[r0:r0 + CHUNK]
        dtc = dt_ref[0, r0:r0 + CHUNK, :]
        a = dtc * a_neg
        hi, mid, lo = _split3(a)
        a_cum = _dot(tri_b, hi) + _dot(tri_b, mid) + _dot(tri_b, lo)
        a_cum_t = jnp.transpose(a_cum)
        for g in range(SSD_GROUPS):
            bm = xc[:, SSD_D_INNER + g * SSD_STATE:SSD_D_INNER + (g + 1) * SSD_STATE]
            cm = xc[:, SSD_D_INNER + (SSD_GROUPS + g) * SSD_STATE:SSD_D_INNER + (SSD_GROUPS + g + 1) * SSD_STATE]
            bmb = bm.astype(BF16)
            cmb = cm.astype(BF16)
            cb = _dot_nt(cmb, bmb)
            st_prev = st_sc[g]
            y_off = _dot(cmb, st_prev.astype(BF16))
            xdd_pairs = []
            tot = []
            for pr in range(SSD_HEADS_PER_GROUP // 2):
                h0 = g * SSD_HEADS_PER_GROUP + 2 * pr
                lanes0 = h0 * SSD_HEAD_DIM
                x2 = xc[:, lanes0:lanes0 + V7X_LANES]
                acol = [_lane_bcast(a_cum, col0 + h0 + j) for j in range(2)]
                dcol = [_lane_bcast(dtc, col0 + h0 + j) for j in range(2)]
                a2 = jnp.where(left, acol[0], acol[1])
                xd2 = x2 * jnp.where(left, dcol[0], dcol[1])
                xd2b = xd2.astype(BF16)
                y_pair = []
                for j in range(2):
                    arow = a_cum_t[col0 + h0 + j:col0 + h0 + j + 1, :]
                    lmat = jnp.where(tri, jnp.exp(jnp.where(tri, acol[j] - arow, 0.0)), 0.0)
                    y_pair.append(_dot((cb * lmat).astype(BF16), xd2b))
                y2 = jnp.where(left, y_pair[0], y_pair[1])
                y2 = y2 + jnp.exp(a2) * y_off[:, pr * V7X_LANES:(pr + 1) * V7X_LANES]
                if not reverse:
                    y2 = y2 + dskip_ref[:, lanes0:lanes0 + V7X_LANES] * x2
                y_ref[0, r0:r0 + CHUNK, lanes0:lanes0 + V7X_LANES] = y2
                a_tot = a2[last:last + 1, :]
                xdd_pairs.append((xd2 * jnp.exp(a_tot - a2)).astype(BF16))
                tot.append(a_tot)
            xdd = jnp.concatenate(xdd_pairs, axis=1)
            st_sc[g] = jnp.exp(jnp.concatenate(tot, axis=1)) * st_prev + _dot_tn(bmb, xdd)


def _ssd_call(xbc, dt, cw, cb, alog, dskip, nl, reverse):
    b, l, _ = xbc.shape
    nt = l // TILE
    rows8 = TILE // 8
    tile_of = lambda s: _scan_tile(s, nl, reverse)
    const = lambda shape: pl.BlockSpec(shape, lambda bi, s: (0,) * len(shape))
    return pl.pallas_call(
        functools.partial(_ssd_kernel, nl=nl, reverse=reverse),
        out_shape=jax.ShapeDtypeStruct((b, l, SSD_D_INNER), F32),
        grid=(b, nt),
        in_specs=[pl.BlockSpec((1, TILE, SSD_CONV_DIM), lambda bi, s: (bi, tile_of(s), 0)),
                  pl.BlockSpec((1, 8, SSD_CONV_DIM), lambda bi, s: (bi, jnp.maximum(tile_of(s) * rows8 - 1, 0), 0)),
                  pl.BlockSpec((1, 8, SSD_CONV_DIM),
                               lambda bi, s: (bi, jnp.minimum((tile_of(s) + 1) * rows8, nt * rows8 - 1), 0)),
                  pl.BlockSpec((1, TILE, DT_PAD), lambda bi, s: (bi, tile_of(s), 0)),
                  const(cw.shape), const(cb.shape), const(alog.shape), const(dskip.shape)],
        out_specs=pl.BlockSpec((1, TILE, SSD_D_INNER), lambda bi, s: (bi, tile_of(s), 0)),
        scratch_shapes=[pltpu.VMEM((SSD_GROUPS, SSD_STATE, SSD_HEADS_PER_GROUP * SSD_HEAD_DIM), F32)],
        compiler_params=pltpu.CompilerParams(dimension_semantics=("arbitrary", "arbitrary")),
        name="ssd_bwd" if reverse else "ssd_fwd",
    )(xbc, xbc, xbc, dt, cw, cb, alog, dskip)


def _ret_kernel(q_ref, k_ref, v_ref, ld_ref, y_ref, st_sc, *, nl, reverse):
    s = pl.program_id(1)

    @pl.when(s == 0)
    def _():
        st_sc[...] = jnp.zeros(st_sc.shape, F32)

    tri = _tri(reverse)
    r = lax.broadcasted_iota(jnp.int32, (CHUNK, CHUNK), 0)
    c = lax.broadcasted_iota(jnp.int32, (CHUNK, CHUNK), 1)
    dist = jnp.maximum((c - r) if reverse else (r - c), 0).astype(F32)
    pos = ((CHUNK - 1 - r) if reverse else r).astype(F32)
    col0 = RET_HEADS if reverse else 0

    chunks = range(TILE // CHUNK)
    for ci in (reversed(chunks) if reverse else chunks):
        r0 = ci * CHUNK
        for hd in range(RET_HEADS):
            c0 = hd * RET_DK
            lg = jnp.broadcast_to(-jnp.exp(ld_ref[0:1, col0 + hd:col0 + hd + 1]), (CHUNK, CHUNK))
            q = q_ref[0, r0:r0 + CHUNK, c0:c0 + RET_DK]
            k = k_ref[0, r0:r0 + CHUNK, c0:c0 + RET_DK]
            v = v_ref[0, r0:r0 + CHUNK, c0:c0 + RET_DV]
            dmat = jnp.where(tri, jnp.exp(dist * lg), 0.0)
            sc = _dot_nt(q, k) * dmat
            st_prev = st_sc[hd]
            y = _dot(sc.astype(BF16), v) + _dot(q, st_prev.astype(BF16)) * jnp.exp((pos + 1.0) * lg)
            y_ref[0, r0:r0 + CHUNK, c0:c0 + RET_DV] = y
            k_dec = (k.astype(F32) * jnp.exp((CHUNK - 1.0 - pos) * lg)).astype(BF16)
            st_sc[hd] = jnp.exp(CHUNK * lg) * st_prev + _dot_tn(k_dec, v)


def _ret_call(rq, rk, rv, ld, nl, reverse):
    b, l, _ = rq.shape
    nt = l // TILE
    tok = pl.BlockSpec((1, TILE, RET_DIM), lambda bi, s: (bi, _scan_tile(s, nl, reverse), 0))
    return pl.pallas_call(
        functools.partial(_ret_kernel, nl=nl, reverse=reverse),
        out_shape=jax.ShapeDtypeStruct((b, l, RET_DIM), F32),
        grid=(b, nt),
        in_specs=[tok, tok, tok, pl.BlockSpec(ld.shape, lambda bi, s: (0, 0))],
        out_specs=tok,
        scratch_shapes=[pltpu.VMEM((RET_HEADS, RET_DK, RET_DV), F32)],
        compiler_params=pltpu.CompilerParams(dimension_semantics=("arbitrary", "arbitrary")),
        name="ret_bwd" if reverse else "ret_fwd",
    )(rq, rk, rv, ld)


def _merge_kernel(h_ref, mod_ref, n1w_ref, ot_ref, sf_ref, sb_ref, rf_ref, rb_ref, wzg_ref, wgate_ref,
                  snw_ref, gnw_ref, wb_ref, wout_ref, o_ref):
    h = h_ref[0]
    u = _rms_rows(h, n1w_ref[...]) * (1.0 + mod_ref[0, 1:2, :]) + mod_ref[0, 0:1, :]
    ub = u.astype(BF16)
    zg = _dot(ub, wzg_ref[...])
    gates = jax.nn.sigmoid(_dot(ub, wgate_ref[...]))

    br_attn = _dot_tn(ot_ref[0], wb_ref[0])

    y = (sf_ref[0] + sb_ref[0]) * _silu(zg[:, :SSD_D_INNER])
    br_ssd = _dot(_rms_rows(y, snw_ref[...]).astype(BF16), wb_ref[1])

    yr = rf_ref[0] + rb_ref[0]
    heads = []
    for hd in range(RET_HEADS):
        yh = yr[:, hd * RET_DV:(hd + 1) * RET_DV]
        yc = yh - jnp.mean(yh, axis=-1, keepdims=True)
        heads.append(yc * lax.rsqrt(jnp.mean(yc * yc, axis=-1, keepdims=True) + NORM_EPS))
    yn = jnp.concatenate(heads, axis=1) * gnw_ref[...] * _silu(zg[:, SSD_D_INNER:])
    br_ret = _dot(yn.astype(BF16), wb_ref[2])

    merged = (gates[:, :D_MODEL] * br_attn + gates[:, D_MODEL:2 * D_MODEL] * br_ssd
              + gates[:, 2 * D_MODEL:] * br_ret)
    o_ref[0] = h + mod_ref[0, 2:3, :] * _dot(merged.astype(BF16), wout_ref[...])


def _merge_call(h, mods, n1w, ot, sf, sb, rf, rb, wts, nl):
    b, l, _ = h.shape
    nt = l // TILE
    wzg, wgate, snw, gnw, wb, wout = wts
    const = lambda shape: pl.BlockSpec(shape, lambda bi, i: (0,) * len(shape))
    tok = lambda c: pl.BlockSpec((1, TILE, c), lambda bi, i: (bi, i, 0))
    weights_bytes = 2 * (wzg.size + wgate.size + wb.size + wout.size)
    return pl.pallas_call(
        _merge_kernel,
        out_shape=jax.ShapeDtypeStruct((b, l, D_MODEL), F32),
        grid=(b, nt),
        in_specs=[tok(D_MODEL),
                  pl.BlockSpec((1, 6, D_MODEL), lambda bi, i: (jnp.where(i == nl, 4, bi), 0, 0)),
                  const((1, D_MODEL)),
                  pl.BlockSpec((1, ATTN_Q_DIM, TILE), lambda bi, i: (bi, 0, i)),
                  tok(SSD_D_INNER), tok(SSD_D_INNER), tok(RET_DIM), tok(RET_DIM),
                  const(wzg.shape), const(wgate.shape), const(snw.shape), const(gnw.shape),
                  const(wb.shape), const(wout.shape)],
        out_specs=tok(D_MODEL),
        compiler_params=pltpu.CompilerParams(dimension_semantics=("arbitrary", "arbitrary"),
                                             vmem_limit_bytes=_vmem_limit(weights_bytes)),
        name="merge",
    )(h, mods, n1w, ot, sf, sb, rf, rb, wzg, wgate, snw, gnw, wb, wout)


def _mlp_kernel(h_ref, mod_ref, n2w_ref, w1_ref, w2_ref, fw_ref, o_ref, *, final):
    h = h_ref[0]
    v = _rms_rows(h, n2w_ref[...]) * (1.0 + mod_ref[0, 4:5, :]) + mod_ref[0, 3:4, :]
    a = jnp.maximum(_dot(v.astype(BF16), w1_ref[...]), 0.0)
    out = h + mod_ref[0, 5:6, :] * _dot((a * a).astype(BF16), w2_ref[...])
    o_ref[0] = _rms_rows(out, fw_ref[...]) if final else out


def _mlp_call(h, mods, n2w, w1, w2, fw, nl, final):
    b, l, _ = h.shape
    nt = nl if final else l // TILE
    const = lambda shape: pl.BlockSpec(shape, lambda bi, i: (0,) * len(shape))
    tok = pl.BlockSpec((1, TILE, D_MODEL), lambda bi, i: (bi, i, 0))
    return pl.pallas_call(
        functools.partial(_mlp_kernel, final=final),
        out_shape=jax.ShapeDtypeStruct((b, nt * TILE, D_MODEL), F32),
        grid=(b, nt),
        in_specs=[tok,
                  pl.BlockSpec((1, 6, D_MODEL), lambda bi, i: (jnp.where(i == nl, 4, bi), 0, 0)),
                  const((1, D_MODEL)), const(w1.shape), const(w2.shape), const((1, D_MODEL))],
        out_specs=tok,
        compiler_params=pltpu.CompilerParams(dimension_semantics=("arbitrary", "arbitrary"),
                                             vmem_limit_bytes=_vmem_limit(2 * (w1.size + w2.size))),
        name="mlp_final" if final else "mlp",
    )(h, mods, n2w, w1, w2, fw)


def _rope_tables(n, m):
    rows = n // GRID_W
    row = jnp.repeat(jnp.arange(rows, dtype=F32), GRID_W)
    col = jnp.tile(jnp.arange(GRID_W, dtype=F32), rows)
    inv = ROPE_THETA ** (-jnp.arange(ATTN_AXIS_FREQS, dtype=F32) / ATTN_AXIS_FREQS)
    ang = jnp.concatenate([row[:, None] * inv, col[:, None] * inv], axis=-1)
    cos_a = jnp.concatenate([jnp.cos(ang), jnp.ones((m, ATTN_HALF), F32)], axis=0).T
    sin_a = jnp.concatenate([jnp.sin(ang), jnp.zeros((m, ATTN_HALF), F32)], axis=0).T
    pos = jnp.concatenate([jnp.arange(n, dtype=F32) + m, jnp.arange(m, dtype=F32)])
    inv_r = ROPE_THETA ** (-jnp.linspace(0.0, 1.0, RET_DK // 2, dtype=F32))
    ang_r = pos[:, None] * inv_r
    cos_r = jnp.concatenate([jnp.cos(ang_r), jnp.cos(ang_r)], axis=-1)
    sin_r = jnp.concatenate([-jnp.sin(ang_r), jnp.sin(ang_r)], axis=-1)
    return cos_a, sin_a, cos_r, sin_r


def _layer_weights(w_in, q_norm, k_norm, dt_bias):
    offs = [0]
    for sz in IN_SPLITS:
        offs.append(offs[-1] + sz)
    col = lambda j: w_in[:, offs[j]:offs[j + 1]]
    wqkv = jnp.concatenate([col(0), col(1), col(2)], axis=1).T.astype(BF16)
    qkw = jnp.concatenate([jnp.tile(q_norm, ATTN_HEADS), jnp.tile(k_norm, ATTN_KV_HEADS)])
    qkw = jnp.broadcast_to(qkw[:, None], (ATTN_Q_DIM + ATTN_KV_DIM, TILE)).astype(F32)
    wxbc = col(4).astype(BF16)
    wdt = jnp.pad(col(5), ((0, 0), (0, DT_PAD - 2 * SSD_HEADS))).astype(BF16)
    dtb = jnp.pad(dt_bias.reshape(1, -1), ((0, 0), (0, DT_PAD - 2 * SSD_HEADS))).astype(F32)
    wr = jnp.concatenate([col(6), col(7), col(8)], axis=1).astype(BF16)
    wzg = jnp.concatenate([col(3), col(9)], axis=1).astype(BF16)
    wgate = col(10).astype(BF16)
    return (wqkv, qkw, wxbc, wdt, dtb, wr), (wzg, wgate)


def _pad_lanes(v):
    v = v.reshape(1, -1).astype(F32)
    return jnp.pad(v, ((0, 0), (0, V7X_LANES - v.shape[1])))


def kernel(x, c, ctx, c_ctx, w_mod, b_mod, norm1_w, norm2_w, w_in, attn_q_norm, attn_k_norm, ssd_conv_w,
           ssd_conv_b, ssd_dt_bias, ssd_a_log, ssd_d, ssd_norm_w, ret_log_decay, ret_gn_w, w_branch, w_out,
           w_mlp1, w_mlp2, final_norm_w):
    b, n, d = x.shape
    m = ctx.shape[1]
    depth = w_in.shape[0]
    assert d == D_MODEL and m == TILE and n % TILE == 0 and n % GRID_W == 0 and b <= 4
    nl = n // TILE

    tabs = _rope_tables(n, m)
    cc = jnp.zeros((8, D_MODEL), F32).at[:b].set(c).at[4].set(c_ctx)
    h = jnp.concatenate([x, ctx], axis=1)

    for layer in range(depth):
        final = layer == depth - 1
        in_w, (wzg, wgate) = _layer_weights(w_in[layer], attn_q_norm[layer], attn_k_norm[layer],
                                            ssd_dt_bias[layer])
        n1w = norm1_w[layer].reshape(1, -1)
        mods = _mod_call(cc, w_mod[layer], b_mod[layer])

        qt, k, vt, xbc, dt, rq, rk, rv = _inproj_call(h, mods, n1w, in_w, tabs, nl)
        ot = _attn_call(qt, k, vt, nl)

        cw = jnp.pad(ssd_conv_w[layer], ((0, 8 - SSD_CONV_K), (0, 0)))
        cb = ssd_conv_b[layer].reshape(1, -1)
        alog = _pad_lanes(ssd_a_log[layer])
        dskip = jnp.repeat(ssd_d[layer], SSD_HEAD_DIM).reshape(1, -1)
        sf = _ssd_call(xbc, dt, cw, cb, alog, dskip, nl, False)
        sb = _ssd_call(xbc, dt, cw, cb, alog, dskip, nl, True)

        ld = _pad_lanes(ret_log_decay[layer])
        rf = _ret_call(rq, rk, rv, ld, nl, False)
        rb = _ret_call(rq, rk, rv, ld, nl, True)

        merge_w = (wzg, wgate, ssd_norm_w[layer].reshape(1, -1), ret_gn_w[layer].reshape(1, -1),
                   w_branch[layer].astype(BF16), w_out[layer].astype(BF16))
        h = _merge_call(h, mods, n1w, ot, sf, sb, rf, rb, merge_w, nl)
        h = _mlp_call(h, mods, norm2_w[layer].reshape(1, -1), w_mlp1[layer].astype(BF16),
                      w_mlp2[layer].astype(BF16), final_norm_w.reshape(1, -1), nl, final)
    return h
```

```python
import functools
import math

import jax
import jax.numpy as jnp
from jax import lax
from jax.experimental import pallas as pl
from jax.experimental.pallas import tpu as pltpu

F32 = jnp.float32
BF16 = jnp.bfloat16

D_MODEL = 1024
GRID_W = 64
NORM_EPS = 1e-6
ROPE_THETA = 10000.0

ATTN_HEADS = 8
ATTN_KV_HEADS = 2
ATTN_GROUP = ATTN_HEADS // ATTN_KV_HEADS
ATTN_HEAD_DIM = 64
ATTN_HALF = ATTN_HEAD_DIM // 2
ATTN_AXIS_FREQS = ATTN_HEAD_DIM // 4
ATTN_Q_DIM = ATTN_HEADS * ATTN_HEAD_DIM
ATTN_KV_DIM = ATTN_KV_HEADS * ATTN_HEAD_DIM

SSD_HEADS = 8
SSD_HEAD_DIM = 64
SSD_D_INNER = SSD_HEADS * SSD_HEAD_DIM
SSD_GROUPS = 2
SSD_STATE = 128
SSD_CONV_K = 3
SSD_CONV_DIM = SSD_D_INNER + 2 * SSD_GROUPS * SSD_STATE
SSD_HEADS_PER_GROUP = SSD_HEADS // SSD_GROUPS

RET_HEADS = 4
RET_DK = 128
RET_DV = 128
RET_DIM = RET_HEADS * RET_DK

N_BRANCH = 3
BRANCH_W = 512
MLP_HIDDEN = 4 * D_MODEL

IN_SPLITS = (ATTN_Q_DIM, ATTN_KV_DIM, ATTN_KV_DIM, SSD_D_INNER, SSD_CONV_DIM, 2 * SSD_HEADS,
             RET_DIM, RET_DIM, RET_HEADS * RET_DV, RET_HEADS * RET_DV, N_BRANCH * D_MODEL)

V7X_LANES = 128
V7X_VMEM_BYTES = 64 * 1024 * 1024

TILE = 256
CHUNK = 128
DT_PAD = V7X_LANES
NEG_BIG = -1e30
ATTN_KEY_CHUNK_MAX = 256
ATTN_QK_TILE = 256
ATTN_PV_TILE = 256
ATTN_ONES_ROWS = 16


def _vmem_limit(resident_bytes):
    return int(min(V7X_VMEM_BYTES - 8 * 1024 * 1024, 2 * resident_bytes + 16 * 1024 * 1024))


def _sigmoid(x):
    return 0.5 * jnp.tanh(0.5 * x) + 0.5


def _silu(x):
    return x * _sigmoid(x)


def _rms_rows(x, w):
    return x * lax.rsqrt(jnp.mean(x * x, axis=-1, keepdims=True) + NORM_EPS) * w


def _dot(a, b):
    return jnp.dot(a, b, preferred_element_type=F32)


def _dot_nt(a, b):
    return lax.dot_general(a, b, (((1,), (1,)), ((), ())), preferred_element_type=F32)


def _dot_tn(a, b):
    return lax.dot_general(a, b, (((0,), (0,)), ((), ())), preferred_element_type=F32)


def _split3(x):
    hi = x.astype(BF16)
    r1 = x - hi.astype(F32)
    mid = r1.astype(BF16)
    lo = (r1 - mid.astype(F32)).astype(BF16)
    return hi, mid, lo


def _mod_kernel(c_ref, w_ref, b_ref, o_ref):
    s = _silu(c_ref[...])
    o_ref[...] = jnp.dot(s, w_ref[...], preferred_element_type=F32,
                         precision=lax.Precision.HIGHEST) + b_ref[...]


def _mod_call(cc, w_mod, b_mod):
    nblk = w_mod.shape[1] // D_MODEL
    out = pl.pallas_call(
        _mod_kernel,
        out_shape=jax.ShapeDtypeStruct((8, nblk * D_MODEL), F32),
        grid=(nblk,),
        in_specs=[pl.BlockSpec((8, D_MODEL), lambda j: (0, 0)),
                  pl.BlockSpec((D_MODEL, D_MODEL), lambda j: (0, j)),
                  pl.BlockSpec((1, D_MODEL), lambda j: (0, j))],
        out_specs=pl.BlockSpec((8, D_MODEL), lambda j: (0, j)),
        compiler_params=pltpu.CompilerParams(dimension_semantics=("arbitrary",)),
        name="mod",
    )(cc, w_mod, b_mod.reshape(1, -1))
    return out.reshape(8, nblk, D_MODEL)


def _inproj_kernel(h_ref, mod_ref, n1w_ref, wqkv_ref, qkw_ref, cos_ref, sin_ref, wxbc_ref, wdt_ref,
                   dtb_ref, wr_ref, cosr_ref, sinr_ref,
                   qt_ref, k_ref, vt_ref, xbc_ref, dt_ref, rq_ref, rk_ref, rv_ref):
    h = h_ref[0]
    u = _rms_rows(h, n1w_ref[...]) * (1.0 + mod_ref[0, 1:2, :]) + mod_ref[0, 0:1, :]
    ub = u.astype(BF16)

    qkv_t = _dot_nt(wqkv_ref[...], ub)
    cos = cos_ref[...]
    sin = sin_ref[...]
    k_rows = []
    for hd in range(ATTN_HEADS + ATTN_KV_HEADS):
        r0 = hd * ATTN_HEAD_DIM
        xh = qkv_t[r0:r0 + ATTN_HEAD_DIM]
        yh = xh * lax.rsqrt(jnp.mean(xh * xh, axis=0, keepdims=True) + NORM_EPS) * qkw_ref[r0:r0 + ATTN_HEAD_DIM, :]
        y1 = yh[:ATTN_HALF]
        y2 = yh[ATTN_HALF:]
        o1 = y1 * cos - y2 * sin
        o2 = y1 * sin + y2 * cos
        if hd < ATTN_HEADS:
            scale = ATTN_HEAD_DIM ** -0.5 * math.log2(math.e)
            qt_ref[0, r0:r0 + ATTN_HALF, :] = (o1 * scale).astype(BF16)
            qt_ref[0, r0 + ATTN_HALF:r0 + ATTN_HEAD_DIM, :] = (o2 * scale).astype(BF16)
        else:
            k_rows += [o1, o2]
    k_t = jnp.concatenate(k_rows, axis=0)
    k_ref[0] = jnp.transpose(k_t).astype(BF16)
    vt_ref[0] = qkv_t[ATTN_Q_DIM + ATTN_KV_DIM:].astype(BF16)

    xbc_ref[0] = _dot(ub, wxbc_ref[...])
    dt_raw = _dot(ub, wdt_ref[...]) + dtb_ref[...]
    dt_ref[0] = jnp.maximum(dt_raw, 0.0) + jnp.log1p(jnp.exp(-jnp.abs(dt_raw)))

    r = _dot(ub, wr_ref[...])
    cosr = cosr_ref[...]
    sinr = sinr_ref[...]
    for hd in range(RET_HEADS):
        c0 = hd * RET_DK
        qh = r[:, c0:c0 + RET_DK]
        kh = r[:, RET_DIM + c0:RET_DIM + c0 + RET_DK]
        rq_ref[0, :, c0:c0 + RET_DK] = (qh * cosr + pltpu.roll(qh, RET_DK // 2, 1) * sinr).astype(BF16)
        rk_ref[0, :, c0:c0 + RET_DK] = ((kh * cosr + pltpu.roll(kh, RET_DK // 2, 1) * sinr)
                                        * (RET_DK ** -0.5)).astype(BF16)
    rv_ref[0] = r[:, 2 * RET_DIM:].astype(BF16)


def _inproj_call(h, mods, n1w, wts, tabs, nl):
    b, l, _ = h.shape
    nt = l // TILE
    wqkv, qkw, wxbc, wdt, dtb, wr = wts
    cos_a, sin_a, cos_r, sin_r = tabs
    const = lambda shape: pl.BlockSpec(shape, lambda bi, i: (0,) * len(shape))
    tok = lambda c: pl.BlockSpec((1, TILE, c), lambda bi, i: (bi, i, 0))
    tok_t = lambda r: pl.BlockSpec((1, r, TILE), lambda bi, i: (bi, 0, i))
    weights_bytes = 2 * (wqkv.size + wxbc.size + wdt.size + wr.size) + 4 * qkw.size
    return pl.pallas_call(
        _inproj_kernel,
        out_shape=(jax.ShapeDtypeStruct((b, ATTN_Q_DIM, l), BF16),
                   jax.ShapeDtypeStruct((b, l, ATTN_KV_DIM), BF16),
                   jax.ShapeDtypeStruct((b, ATTN_KV_DIM, l), BF16),
                   jax.ShapeDtypeStruct((b, l, SSD_CONV_DIM), F32),
                   jax.ShapeDtypeStruct((b, l, DT_PAD), F32),
                   jax.ShapeDtypeStruct((b, l, RET_DIM), BF16),
                   jax.ShapeDtypeStruct((b, l, RET_DIM), BF16),
                   jax.ShapeDtypeStruct((b, l, RET_DIM), BF16)),
        grid=(b, nt),
        in_specs=[tok(D_MODEL),
                  pl.BlockSpec((1, 6, D_MODEL), lambda bi, i: (jnp.where(i == nl, 4, bi), 0, 0)),
                  const((1, D_MODEL)),
                  const(wqkv.shape), const(qkw.shape),
                  pl.BlockSpec((ATTN_HALF, TILE), lambda bi, i: (0, i)),
                  pl.BlockSpec((ATTN_HALF, TILE), lambda bi, i: (0, i)),
                  const(wxbc.shape), const(wdt.shape), const(dtb.shape), const(wr.shape),
                  pl.BlockSpec((TILE, RET_DK), lambda bi, i: (i, 0)),
                  pl.BlockSpec((TILE, RET_DK), lambda bi, i: (i, 0))],
        out_specs=(tok_t(ATTN_Q_DIM), tok(ATTN_KV_DIM), tok_t(ATTN_KV_DIM), tok(SSD_CONV_DIM), tok(DT_PAD),
                   tok(RET_DIM), tok(RET_DIM), tok(RET_DIM)),
        compiler_params=pltpu.CompilerParams(dimension_semantics=("arbitrary", "arbitrary"),
                                             vmem_limit_bytes=_vmem_limit(weights_bytes)),
        name="inproj",
    )(h, mods, n1w, wqkv, qkw, cos_a, sin_a, wxbc, wdt, dtb, wr, cos_r, sin_r)


def _attn_kernel(qt_ref, k_ref, vt_ref, o_ref, qpad_sc, m_sc, acc_sc, s0_sc, s1_sc, mx0_sc, mx1_sc, *, nl, kc):
    kv = pl.program_id(1)
    i = pl.program_id(2)
    nch = (nl + 1) * TILE // kc

    row = lax.broadcasted_iota(jnp.int32, (2 * ATTN_HEAD_DIM, TILE), 0)
    keep = (row >= ATTN_HEAD_DIM).astype(jnp.int32) == kv
    for g in range(ATTN_GROUP):
        qg = qt_ref[0, g * ATTN_HEAD_DIM:(g + 1) * ATTN_HEAD_DIM, :]
        qq = jnp.concatenate([qg, qg], axis=0)
        qpad_sc[:, g * TILE:(g + 1) * TILE] = jnp.where(keep, qq, jnp.zeros_like(qq))
    m_sc[...] = jnp.full(m_sc.shape, NEG_BIG, F32)
    acc_sc[...] = jnp.zeros(acc_sc.shape, F32)

    wide = ATTN_GROUP * TILE

    def scores(off, size, c0, width):
        return _dot(k_ref[0, pl.ds(off, size), :], qpad_sc[:, c0:c0 + width])

    def accumulate(off, size, c0, s, mx):
        cols = slice(c0, c0 + ATTN_PV_TILE)
        m_old = m_sc[:, cols]
        m_new = jnp.maximum(m_old, mx)
        m_sc[:, cols] = m_new
        p = jnp.exp2(s - m_new).astype(BF16)
        v_aug = jnp.concatenate([vt_ref[0, :, pl.ds(off, size)], jnp.ones((ATTN_ONES_ROWS, size), BF16)], axis=0)
        acc_sc[:, cols] = jnp.exp2(m_old - m_new) * acc_sc[:, cols] + _dot(v_aug, p)

    def qk(c, c0, s_sc, mx_sc):
        s = scores(pl.multiple_of(c * kc, kc), kc, c0, ATTN_QK_TILE)
        s_sc[:, c0:c0 + ATTN_QK_TILE] = s
        mx_sc[:, c0:c0 + ATTN_QK_TILE] = jnp.max(s, axis=0, keepdims=True)

    def pv(c, c0, s_sc, mx_sc):
        accumulate(pl.multiple_of(c * kc, kc), kc, c0, s_sc[:, c0:c0 + ATTN_PV_TILE],
                   mx_sc[:, c0:c0 + ATTN_PV_TILE])

    def step(c_qk, qk_bufs, c_pv, pv_bufs):
        for q0 in range(0, wide, ATTN_QK_TILE):
            if c_qk is not None:
                qk(c_qk, q0, *qk_bufs)
            if c_pv is not None:
                for c0 in range(q0, q0 + ATTN_QK_TILE, ATTN_PV_TILE):
                    pv(c_pv, c0, *pv_bufs)

    buf0 = (s0_sc, mx0_sc)
    buf1 = (s1_sc, mx1_sc)

    @pl.when(i != nl)
    def _():
        step(0, buf0, None, None)

        def body(j, carry):
            step(2 * j + 1, buf1, 2 * j, buf0)
            step(2 * j + 2, buf0, 2 * j + 1, buf1)
            return carry
        pairs = (nch - 1) // 2
        lax.fori_loop(0, pairs, body, 0)
        if nch - 1 == 2 * pairs:
            step(None, None, nch - 1, buf0)
        else:
            step(nch - 1, buf1, nch - 2, buf0)
            step(None, None, nch - 1, buf1)

    @pl.when(i == nl)
    def _():
        for c0 in range(0, wide, ATTN_PV_TILE):
            s = scores(nl * TILE, TILE, c0, ATTN_PV_TILE)
            accumulate(nl * TILE, TILE, c0, s, jnp.max(s, axis=0, keepdims=True))

    acc = acc_sc[...]
    out = acc[:ATTN_HEAD_DIM] / acc[ATTN_HEAD_DIM:ATTN_HEAD_DIM + 1]
    for g in range(ATTN_GROUP):
        o_ref[0, g * ATTN_HEAD_DIM:(g + 1) * ATTN_HEAD_DIM, :] = out[:, g * TILE:(g + 1) * TILE].astype(BF16)


def _attn_call(qt, k, vt, nl):
    b, _, l = qt.shape
    nt = l // TILE
    kc = max(d * TILE for d in range(1, ATTN_KEY_CHUNK_MAX // TILE + 1) if nt % d == 0)
    gq = ATTN_GROUP * ATTN_HEAD_DIM
    wide = ATTN_GROUP * TILE
    resident = 2 * (l * ATTN_KV_DIM + ATTN_HEAD_DIM * l) + 4 * kc * wide
    return pl.pallas_call(
        functools.partial(_attn_kernel, nl=nl, kc=kc),
        out_shape=jax.ShapeDtypeStruct((b, ATTN_Q_DIM, l), BF16),
        grid=(b, ATTN_KV_HEADS, nt),
        in_specs=[pl.BlockSpec((1, gq, TILE), lambda bi, kv, i: (bi, kv, i)),
                  pl.BlockSpec((1, l, ATTN_KV_DIM), lambda bi, kv, i: (bi, 0, 0)),
                  pl.BlockSpec((1, ATTN_HEAD_DIM, l), lambda bi, kv, i: (bi, kv, 0))],
        out_specs=pl.BlockSpec((1, gq, TILE), lambda bi, kv, i: (bi, kv, i)),
        scratch_shapes=[pltpu.VMEM((2 * ATTN_HEAD_DIM, wide), BF16),
                        pltpu.VMEM((1, wide), F32),
                        pltpu.VMEM((ATTN_HEAD_DIM + ATTN_ONES_ROWS, wide), F32),
                        pltpu.VMEM((kc, wide), F32), pltpu.VMEM((kc, wide), F32),
                        pltpu.VMEM((1, wide), F32), pltpu.VMEM((1, wide), F32)],
        compiler_params=pltpu.CompilerParams(dimension_semantics=("arbitrary",) * 3,
                                             vmem_limit_bytes=_vmem_limit(resident)),
        name="attn",
    )(qt, k, vt)


def _scan_tile(s, nl, reverse):
    return jnp.where(s == 0, nl, nl - s) if reverse else jnp.where(s == 0, nl, s - 1)


def _tri(reverse):
    r = lax.broadcasted_iota(jnp.int32, (CHUNK, CHUNK), 0)
    c = lax.broadcasted_iota(jnp.int32, (CHUNK, CHUNK), 1)
    return (c >= r) if reverse else (c <= r)


def _lane_bcast(x, c):
    return jnp.broadcast_to(x[:, c:c + 1], (x.shape[0], V7X_LANES))


def _ssd_tile(t, x_ref, xp_ref, xn_ref, dt_ref, cw_ref, cb_ref, alog_ref, dskip_ref, y_ref, st_sc,
              *, nl, reverse):
    x = x_ref[0]
    row = lax.broadcasted_iota(jnp.int32, (8, SSD_CONV_DIM), 0)
    has_prev = jnp.logical_and(t != nl, t != 0)
    has_next = t < nl - 1
    prev_row = jnp.where(has_prev, xp_ref[0, 7:8, :], 0.0)
    next_row = jnp.where(has_next, xn_ref[0, 0:1, :], 0.0)
    x_m1 = pltpu.roll(x, 1, 0)
    x_m1 = jnp.concatenate([jnp.where(row == 0, prev_row, x_m1[:8]), x_m1[8:]], axis=0)
    x_p1 = pltpu.roll(x, TILE - 1, 0)
    x_p1 = jnp.concatenate([x_p1[:TILE - 8], jnp.where(row == 7, next_row, x_p1[TILE - 8:])], axis=0)
    xs = _silu(cw_ref[0:1, :] * x_m1 + cw_ref[1:2, :] * x + cw_ref[2:3, :] * x_p1 + cb_ref[...])

    tri = _tri(reverse)
    tri_b = tri.astype(BF16)
    lane = lax.broadcasted_iota(jnp.int32, (CHUNK, V7X_LANES), 1)
    left = lane < SSD_HEAD_DIM
    col0 = SSD_HEADS if reverse else 0
    a_neg = -jnp.exp(alog_ref[...])
    last = 0 if reverse else CHUNK - 1

    chunks = range(TILE // CHUNK)
    for ci in (reversed(chunks) if reverse else chunks):
        r0 = ci * CHUNK
        xc = xs[r0:r0 + CHUNK]
        dtc = dt_ref[0, r0:r0 + CHUNK, :]
        a = dtc * a_neg
        hi, mid, lo = _split3(a)
        a_cum = _dot(tri_b, hi) + _dot(tri_b, mid) + _dot(tri_b, lo)
        a_cum_t = jnp.transpose(a_cum)
        for g in range(SSD_GROUPS):
            bm = xc[:, SSD_D_INNER + g * SSD_STATE:SSD_D_INNER + (g + 1) * SSD_STATE]
            cm = xc[:, SSD_D_INNER + (SSD_GROUPS + g) * SSD_STATE:SSD_D_INNER + (SSD_GROUPS + g + 1) * SSD_STATE]
            bmb = bm.astype(BF16)
            cmb = cm.astype(BF16)
            cb = _dot_nt(cmb, bmb)
            st_prev = st_sc[g]
            y_off = _dot(cmb, st_prev.astype(BF16))
            xdd_pairs = []
            tot = []
            for pr in range(SSD_HEADS_PER_GROUP // 2):
                h0 = g * SSD_HEADS_PER_GROUP + 2 * pr
                lanes0 = h0 * SSD_HEAD_DIM
                x2 = xc[:, lanes0:lanes0 + V7X_LANES]
                acol = [_lane_bcast(a_cum, col0 + h0 + j) for j in range(2)]
                dcol = [_lane_bcast(dtc, col0 + h0 + j) for j in range(2)]
                a2 = jnp.where(left, acol[0], acol[1])
                xd2 = x2 * jnp.where(left, dcol[0], dcol[1])
                xd2b = xd2.astype(BF16)
                y_pair = []
                for j in range(2):
                    arow = a_cum_t[col0 + h0 + j:col0 + h0 + j + 1, :]
                    lmat = jnp.where(tri, jnp.exp(jnp.where(tri, acol[j] - arow, 0.0)), 0.0)
                    y_pair.append(_dot((cb * lmat).astype(BF16), xd2b))
                y2 = jnp.where(left, y_pair[0], y_pair[1])
                y2 = y2 + jnp.exp(a2) * y_off[:, pr * V7X_LANES:(pr + 1) * V7X_LANES]
                if not reverse:
                    y2 = y2 + dskip_ref[:, lanes0:lanes0 + V7X_LANES] * x2
                y_ref[0, r0:r0 + CHUNK, lanes0:lanes0 + V7X_LANES] = y2
                a_tot = a2[last:last + 1, :]
                xdd_pairs.append((xd2 * jnp.exp(a_tot - a2)).astype(BF16))
                tot.append(a_tot)
            xdd = jnp.concatenate(xdd_pairs, axis=1)
            st_sc[g] = jnp.exp(jnp.concatenate(tot, axis=1)) * st_prev + _dot_tn(bmb, xdd)


def _ssd_kernel(xf_ref, xpf_ref, xnf_ref, dtf_ref, xb_ref, xpb_ref, xnb_ref, dtb_ref, cw_ref, cb_ref, alog_ref,
                dskip_ref, yf_ref, yb_ref, stf_sc, stb_sc, *, nl):
    s = pl.program_id(1)

    @pl.when(s == 0)
    def _():
        stf_sc[...] = jnp.zeros(stf_sc.shape, F32)
        stb_sc[...] = jnp.zeros(stb_sc.shape, F32)

    _ssd_tile(_scan_tile(s, nl, False), xf_ref, xpf_ref, xnf_ref, dtf_ref, cw_ref, cb_ref, alog_ref, dskip_ref,
              yf_ref, stf_sc, nl=nl, reverse=False)
    _ssd_tile(_scan_tile(s, nl, True), xb_ref, xpb_ref, xnb_ref, dtb_ref, cw_ref, cb_ref, alog_ref, dskip_ref,
              yb_ref, stb_sc, nl=nl, reverse=True)


def _ssd_call(xbc, dt, cw, cb, alog, dskip, nl):
    b, l, _ = xbc.shape
    nt = l // TILE
    rows8 = TILE // 8
    const = lambda shape: pl.BlockSpec(shape, lambda bi, s: (0,) * len(shape))

    def stream(reverse):
        tile_of = lambda s: _scan_tile(s, nl, reverse)
        return [pl.BlockSpec((1, TILE, SSD_CONV_DIM), lambda bi, s: (bi, tile_of(s), 0)),
                pl.BlockSpec((1, 8, SSD_CONV_DIM), lambda bi, s: (bi, jnp.maximum(tile_of(s) * rows8 - 1, 0), 0)),
                pl.BlockSpec((1, 8, SSD_CONV_DIM),
                             lambda bi, s: (bi, jnp.minimum((tile_of(s) + 1) * rows8, nt * rows8 - 1), 0)),
                pl.BlockSpec((1, TILE, DT_PAD), lambda bi, s: (bi, tile_of(s), 0))]

    out = lambda reverse: pl.BlockSpec((1, TILE, SSD_D_INNER), lambda bi, s: (bi, _scan_tile(s, nl, reverse), 0))
    state = pltpu.VMEM((SSD_GROUPS, SSD_STATE, SSD_HEADS_PER_GROUP * SSD_HEAD_DIM), F32)
    y = jax.ShapeDtypeStruct((b, l, SSD_D_INNER), F32)
    return pl.pallas_call(
        functools.partial(_ssd_kernel, nl=nl),
        out_shape=(y, y),
        grid=(b, nt),
        in_specs=stream(False) + stream(True) + [const(cw.shape), const(cb.shape), const(alog.shape),
                                                 const(dskip.shape)],
        out_specs=(out(False), out(True)),
        scratch_shapes=[state, state],
        compiler_params=pltpu.CompilerParams(dimension_semantics=("arbitrary", "arbitrary")),
        name="ssd",
    )(xbc, xbc, xbc, dt, xbc, xbc, xbc, dt, cw, cb, alog, dskip)


def _ret_tile(q_ref, k_ref, v_ref, ld_ref, y_ref, st_sc, *, reverse):
    tri = _tri(reverse)
    r = lax.broadcasted_iota(jnp.int32, (CHUNK, CHUNK), 0)
    c = lax.broadcasted_iota(jnp.int32, (CHUNK, CHUNK), 1)
    dist = jnp.maximum((c - r) if reverse else (r - c), 0).astype(F32)
    pos = ((CHUNK - 1 - r) if reverse else r).astype(F32)
    col0 = RET_HEADS if reverse else 0

    decay = []
    for hd in range(RET_HEADS):
        lg = jnp.broadcast_to(-jnp.exp(ld_ref[0:1, col0 + hd:col0 + hd + 1]), (CHUNK, CHUNK))
        decay.append((jnp.where(tri, jnp.exp(dist * lg), 0.0), jnp.exp((pos + 1.0) * lg),
                      jnp.exp((CHUNK - 1.0 - pos) * lg), jnp.exp(CHUNK * lg)))

    chunks = range(TILE // CHUNK)
    for ci in (reversed(chunks) if reverse else chunks):
        r0 = ci * CHUNK
        for hd in range(RET_HEADS):
            c0 = hd * RET_DK
            dmat, q_dec, k_dec, chunk_dec = decay[hd]
            q = q_ref[0, r0:r0 + CHUNK, c0:c0 + RET_DK]
            k = k_ref[0, r0:r0 + CHUNK, c0:c0 + RET_DK]
            v = v_ref[0, r0:r0 + CHUNK, c0:c0 + RET_DV]
            sc = _dot_nt(q, k) * dmat
            st_prev = st_sc[hd]
            y_ref[0, r0:r0 + CHUNK, c0:c0 + RET_DV] = (_dot(sc.astype(BF16), v)
                                                        + _dot(q, st_prev.astype(BF16)) * q_dec)
            st_sc[hd] = chunk_dec * st_prev + _dot_tn((k.astype(F32) * k_dec).astype(BF16), v)


def _ret_kernel(qf_ref, kf_ref, vf_ref, qb_ref, kb_ref, vb_ref, ld_ref, yf_ref, yb_ref, stf_sc, stb_sc):
    @pl.when(pl.program_id(1) == 0)
    def _():
        stf_sc[...] = jnp.zeros(stf_sc.shape, F32)
        stb_sc[...] = jnp.zeros(stb_sc.shape, F32)

    _ret_tile(qf_ref, kf_ref, vf_ref, ld_ref, yf_ref, stf_sc, reverse=False)
    _ret_tile(qb_ref, kb_ref, vb_ref, ld_ref, yb_ref, stb_sc, reverse=True)


def _ret_call(rq, rk, rv, ld, nl):
    b, l, _ = rq.shape
    nt = l // TILE
    tok = lambda reverse: pl.BlockSpec((1, TILE, RET_DIM), lambda bi, s: (bi, _scan_tile(s, nl, reverse), 0))
    state = pltpu.VMEM((RET_HEADS, RET_DK, RET_DV), F32)
    y = jax.ShapeDtypeStruct((b, l, RET_DIM), F32)
    return pl.pallas_call(
        _ret_kernel,
        out_shape=(y, y),
        grid=(b, nt),
        in_specs=[tok(False)] * 3 + [tok(True)] * 3 + [pl.BlockSpec(ld.shape, lambda bi, s: (0, 0))],
        out_specs=(tok(False), tok(True)),
        scratch_shapes=[state, state],
        compiler_params=pltpu.CompilerParams(dimension_semantics=("arbitrary", "arbitrary")),
        name="ret",
    )(rq, rk, rv, rq, rk, rv, ld)


def _merge_kernel(h_ref, mod_ref, n1w_ref, ot_ref, sf_ref, sb_ref, rf_ref, rb_ref, wzg_ref, wgate_ref,
                  snw_ref, gnw_ref, wb_ref, wout_ref, o_ref):
    h = h_ref[0]
    u = _rms_rows(h, n1w_ref[...]) * (1.0 + mod_ref[0, 1:2, :]) + mod_ref[0, 0:1, :]
    ub = u.astype(BF16)
    zg = _dot(ub, wzg_ref[...])
    gates = _sigmoid(_dot(ub, wgate_ref[...]))

    br_attn = _dot_tn(ot_ref[0], wb_ref[0])

    y = (sf_ref[0] + sb_ref[0]) * _silu(zg[:, :SSD_D_INNER])
    br_ssd = _dot(_rms_rows(y, snw_ref[...]).astype(BF16), wb_ref[1])

    yr = rf_ref[0] + rb_ref[0]
    heads = []
    for hd in range(RET_HEADS):
        yh = yr[:, hd * RET_DV:(hd + 1) * RET_DV]
        yc = yh - jnp.mean(yh, axis=-1, keepdims=True)
        heads.append(yc * lax.rsqrt(jnp.mean(yc * yc, axis=-1, keepdims=True) + NORM_EPS))
    yn = jnp.concatenate(heads, axis=1) * gnw_ref[...] * _silu(zg[:, SSD_D_INNER:])
    br_ret = _dot(yn.astype(BF16), wb_ref[2])

    merged = (gates[:, :D_MODEL] * br_attn + gates[:, D_MODEL:2 * D_MODEL] * br_ssd
              + gates[:, 2 * D_MODEL:] * br_ret)
    o_ref[0] = h + mod_ref[0, 2:3, :] * _dot(merged.astype(BF16), wout_ref[...])


def _merge_call(h, mods, n1w, ot, sf, sb, rf, rb, wts, nl):
    b, l, _ = h.shape
    nt = l // TILE
    wzg, wgate, snw, gnw, wb, wout = wts
    const = lambda shape: pl.BlockSpec(shape, lambda bi, i: (0,) * len(shape))
    tok = lambda c: pl.BlockSpec((1, TILE, c), lambda bi, i: (bi, i, 0))
    weights_bytes = 2 * (wzg.size + wgate.size + wb.size + wout.size)
    return pl.pallas_call(
        _merge_kernel,
        out_shape=jax.ShapeDtypeStruct((b, l, D_MODEL), F32),
        grid=(b, nt),
        in_specs=[tok(D_MODEL),
                  pl.BlockSpec((1, 6, D_MODEL), lambda bi, i: (jnp.where(i == nl, 4, bi), 0, 0)),
                  const((1, D_MODEL)),
                  pl.BlockSpec((1, ATTN_Q_DIM, TILE), lambda bi, i: (bi, 0, i)),
                  tok(SSD_D_INNER), tok(SSD_D_INNER), tok(RET_DIM), tok(RET_DIM),
                  const(wzg.shape), const(wgate.shape), const(snw.shape), const(gnw.shape),
                  const(wb.shape), const(wout.shape)],
        out_specs=tok(D_MODEL),
        compiler_params=pltpu.CompilerParams(dimension_semantics=("arbitrary", "arbitrary"),
                                             vmem_limit_bytes=_vmem_limit(weights_bytes)),
        name="merge",
    )(h, mods, n1w, ot, sf, sb, rf, rb, wzg, wgate, snw, gnw, wb, wout)


def _mlp_kernel(h_ref, mod_ref, n2w_ref, w1_ref, w2_ref, fw_ref, o_ref, *, final):
    h = h_ref[0]
    v = _rms_rows(h, n2w_ref[...]) * (1.0 + mod_ref[0, 4:5, :]) + mod_ref[0, 3:4, :]
    a = jnp.maximum(_dot(v.astype(BF16), w1_ref[...]), 0.0)
    out = h + mod_ref[0, 5:6, :] * _dot((a * a).astype(BF16), w2_ref[...])
    o_ref[0] = _rms_rows(out, fw_ref[...]) if final else out


def _mlp_call(h, mods, n2w, w1, w2, fw, nl, final):
    b, l, _ = h.shape
    nt = nl if final else l // TILE
    const = lambda shape: pl.BlockSpec(shape, lambda bi, i: (0,) * len(shape))
    tok = pl.BlockSpec((1, TILE, D_MODEL), lambda bi, i: (bi, i, 0))
    return pl.pallas_call(
        functools.partial(_mlp_kernel, final=final),
        out_shape=jax.ShapeDtypeStruct((b, nt * TILE, D_MODEL), F32),
        grid=(b, nt),
        in_specs=[tok,
                  pl.BlockSpec((1, 6, D_MODEL), lambda bi, i: (jnp.where(i == nl, 4, bi), 0, 0)),
                  const((1, D_MODEL)), const(w1.shape), const(w2.shape), const((1, D_MODEL))],
        out_specs=tok,
        compiler_params=pltpu.CompilerParams(dimension_semantics=("arbitrary", "arbitrary"),
                                             vmem_limit_bytes=_vmem_limit(2 * (w1.size + w2.size))),
        name="mlp_final" if final else "mlp",
    )(h, mods, n2w, w1, w2, fw)


def _rope_tables(n, m):
    rows = n // GRID_W
    row = jnp.repeat(jnp.arange(rows, dtype=F32), GRID_W)
    col = jnp.tile(jnp.arange(GRID_W, dtype=F32), rows)
    inv = ROPE_THETA ** (-jnp.arange(ATTN_AXIS_FREQS, dtype=F32) / ATTN_AXIS_FREQS)
    ang = jnp.concatenate([row[:, None] * inv, col[:, None] * inv], axis=-1)
    cos_a = jnp.concatenate([jnp.cos(ang), jnp.ones((m, ATTN_HALF), F32)], axis=0).T
    sin_a = jnp.concatenate([jnp.sin(ang), jnp.zeros((m, ATTN_HALF), F32)], axis=0).T
    pos = jnp.concatenate([jnp.arange(n, dtype=F32) + m, jnp.arange(m, dtype=F32)])
    inv_r = ROPE_THETA ** (-jnp.linspace(0.0, 1.0, RET_DK // 2, dtype=F32))
    ang_r = pos[:, None] * inv_r
    cos_r = jnp.concatenate([jnp.cos(ang_r), jnp.cos(ang_r)], axis=-1)
    sin_r = jnp.concatenate([-jnp.sin(ang_r), jnp.sin(ang_r)], axis=-1)
    return cos_a, sin_a, cos_r, sin_r


def _layer_weights(w_in, q_norm, k_norm, dt_bias):
    offs = [0]
    for sz in IN_SPLITS:
        offs.append(offs[-1] + sz)
    col = lambda j: w_in[:, offs[j]:offs[j + 1]]
    wqkv = jnp.concatenate([col(0), col(1), col(2)], axis=1).T.astype(BF16)
    qkw = jnp.concatenate([jnp.tile(q_norm, ATTN_HEADS), jnp.tile(k_norm, ATTN_KV_HEADS)])
    qkw = jnp.broadcast_to(qkw[:, None], (ATTN_Q_DIM + ATTN_KV_DIM, TILE)).astype(F32)
    wxbc = col(4).astype(BF16)
    wdt = jnp.pad(col(5), ((0, 0), (0, DT_PAD - 2 * SSD_HEADS))).astype(BF16)
    dtb = jnp.pad(dt_bias.reshape(1, -1), ((0, 0), (0, DT_PAD - 2 * SSD_HEADS))).astype(F32)
    wr = jnp.concatenate([col(6), col(7), col(8)], axis=1).astype(BF16)
    wzg = jnp.concatenate([col(3), col(9)], axis=1).astype(BF16)
    wgate = col(10).astype(BF16)
    return (wqkv, qkw, wxbc, wdt, dtb, wr), (wzg, wgate)


def _pad_lanes(v):
    v = v.reshape(1, -1).astype(F32)
    return jnp.pad(v, ((0, 0), (0, V7X_LANES - v.shape[1])))


def kernel(x, c, ctx, c_ctx, w_mod, b_mod, norm1_w, norm2_w, w_in, attn_q_norm, attn_k_norm, ssd_conv_w,
           ssd_conv_b, ssd_dt_bias, ssd_a_log, ssd_d, ssd_norm_w, ret_log_decay, ret_gn_w, w_branch, w_out,
           w_mlp1, w_mlp2, final_norm_w):
    b, n, d = x.shape
    m = ctx.shape[1]
    depth = w_in.shape[0]
    assert d == D_MODEL and m == TILE and n % TILE == 0 and n % GRID_W == 0 and b <= 4
    nl = n // TILE

    tabs = _rope_tables(n, m)
    cc = jnp.zeros((8, D_MODEL), F32).at[:b].set(c).at[4].set(c_ctx)
    h = jnp.concatenate([x, ctx], axis=1)

    for layer in range(depth):
        final = layer == depth - 1
        in_w, (wzg, wgate) = _layer_weights(w_in[layer], attn_q_norm[layer], attn_k_norm[layer],
                                            ssd_dt_bias[layer])
        n1w = norm1_w[layer].reshape(1, -1)
        mods = _mod_call(cc, w_mod[layer], b_mod[layer])

        qt, k, vt, xbc, dt, rq, rk, rv = _inproj_call(h, mods, n1w, in_w, tabs, nl)
        ot = _attn_call(qt, k, vt, nl)

        cw = jnp.pad(ssd_conv_w[layer], ((0, 8 - SSD_CONV_K), (0, 0)))
        cb = ssd_conv_b[layer].reshape(1, -1)
        alog = _pad_lanes(ssd_a_log[layer])
        dskip = jnp.repeat(ssd_d[layer], SSD_HEAD_DIM).reshape(1, -1)
        sf, sb = _ssd_call(xbc, dt, cw, cb, alog, dskip, nl)

        ld = _pad_lanes(ret_log_decay[layer])
        rf, rb = _ret_call(rq, rk, rv, ld, nl)

        merge_w = (wzg, wgate, ssd_norm_w[layer].reshape(1, -1), ret_gn_w[layer].reshape(1, -1),
                   w_branch[layer].astype(BF16), w_out[layer].astype(BF16))
        h = _merge_call(h, mods, n1w, ot, sf, sb, rf, rb, merge_w, nl)
        h = _mlp_call(h, mods, norm2_w[layer].reshape(1, -1), w_mlp1[layer].astype(BF16),
                      w_mlp2[layer].astype(BF16), final_norm_w.reshape(1, -1), nl, final)
    return h
```

```python
import functools
import math

import jax
import jax.numpy as jnp
from jax import lax
from jax.experimental import pallas as pl
from jax.experimental.pallas import tpu as pltpu

F32 = jnp.float32
BF16 = jnp.bfloat16

D_MODEL = 1024
GRID_W = 64
NORM_EPS = 1e-6
ROPE_THETA = 10000.0

ATTN_HEADS = 8
ATTN_KV_HEADS = 2
ATTN_GROUP = ATTN_HEADS // ATTN_KV_HEADS
ATTN_HEAD_DIM = 64
ATTN_HALF = ATTN_HEAD_DIM // 2
ATTN_AXIS_FREQS = ATTN_HEAD_DIM // 4
ATTN_Q_DIM = ATTN_HEADS * ATTN_HEAD_DIM
ATTN_KV_DIM = ATTN_KV_HEADS * ATTN_HEAD_DIM

SSD_HEADS = 8
SSD_HEAD_DIM = 64
SSD_D_INNER = SSD_HEADS * SSD_HEAD_DIM
SSD_GROUPS = 2
SSD_STATE = 128
SSD_CONV_K = 3
SSD_CONV_DIM = SSD_D_INNER + 2 * SSD_GROUPS * SSD_STATE
SSD_HEADS_PER_GROUP = SSD_HEADS // SSD_GROUPS

RET_HEADS = 4
RET_DK = 128
RET_DV = 128
RET_DIM = RET_HEADS * RET_DK

N_BRANCH = 3
BRANCH_W = 512
MLP_HIDDEN = 4 * D_MODEL

IN_SPLITS = (ATTN_Q_DIM, ATTN_KV_DIM, ATTN_KV_DIM, SSD_D_INNER, SSD_CONV_DIM, 2 * SSD_HEADS,
             RET_DIM, RET_DIM, RET_HEADS * RET_DV, RET_HEADS * RET_DV, N_BRANCH * D_MODEL)

V7X_LANES = 128
V7X_VMEM_BYTES = 64 * 1024 * 1024

TILE = 256
CHUNK = 128
DT_PAD = V7X_LANES
NEG_BIG = -1e30
ATTN_KEY_CHUNK = 768
ATTN_QK_TILE = 256
ATTN_PV_TILE = 256
ATTN_ONES_ROWS = 16


def _vmem_limit(resident_bytes):
    return int(min(V7X_VMEM_BYTES - 8 * 1024 * 1024, 2 * resident_bytes + 16 * 1024 * 1024))


def _sigmoid(x):
    return 0.5 * jnp.tanh(0.5 * x) + 0.5


def _silu(x):
    return x * _sigmoid(x)


def _rms_rows(x, w):
    return x * lax.rsqrt(jnp.mean(x * x, axis=-1, keepdims=True) + NORM_EPS) * w


def _dot(a, b):
    return jnp.dot(a, b, preferred_element_type=F32)


def _dot_nt(a, b):
    return lax.dot_general(a, b, (((1,), (1,)), ((), ())), preferred_element_type=F32)


def _dot_tn(a, b):
    return lax.dot_general(a, b, (((0,), (0,)), ((), ())), preferred_element_type=F32)


def _split3(x):
    hi = x.astype(BF16)
    r1 = x - hi.astype(F32)
    mid = r1.astype(BF16)
    lo = (r1 - mid.astype(F32)).astype(BF16)
    return hi, mid, lo


def _mod_kernel(c_ref, w_ref, b_ref, o_ref):
    s = _silu(c_ref[...])
    o_ref[...] = jnp.dot(s, w_ref[...], preferred_element_type=F32,
                         precision=lax.Precision.HIGHEST) + b_ref[...]


def _mod_call(cc, w_mod, b_mod):
    nblk = w_mod.shape[1] // D_MODEL
    out = pl.pallas_call(
        _mod_kernel,
        out_shape=jax.ShapeDtypeStruct((8, nblk * D_MODEL), F32),
        grid=(nblk,),
        in_specs=[pl.BlockSpec((8, D_MODEL), lambda j: (0, 0)),
                  pl.BlockSpec((D_MODEL, D_MODEL), lambda j: (0, j)),
                  pl.BlockSpec((1, D_MODEL), lambda j: (0, j))],
        out_specs=pl.BlockSpec((8, D_MODEL), lambda j: (0, j)),
        compiler_params=pltpu.CompilerParams(dimension_semantics=("arbitrary",)),
        name="mod",
    )(cc, w_mod, b_mod.reshape(1, -1))
    return out.reshape(8, nblk, D_MODEL)


def _inproj_kernel(h_ref, mod_ref, n1w_ref, wqkv_ref, qkw_ref, cos_ref, sin_ref, wxbc_ref, wdt_ref,
                   dtb_ref, wr_ref, cosr_ref, sinr_ref,
                   qt_ref, k_ref, vt_ref, xbc_ref, dt_ref, rq_ref, rk_ref, rv_ref):
    h = h_ref[0]
    u = _rms_rows(h, n1w_ref[...]) * (1.0 + mod_ref[0, 1:2, :]) + mod_ref[0, 0:1, :]
    ub = u.astype(BF16)

    qkv_t = _dot_nt(wqkv_ref[...], ub)
    cos = cos_ref[...]
    sin = sin_ref[...]
    k_rows = []
    for hd in range(ATTN_HEADS + ATTN_KV_HEADS):
        r0 = hd * ATTN_HEAD_DIM
        xh = qkv_t[r0:r0 + ATTN_HEAD_DIM]
        yh = xh * lax.rsqrt(jnp.mean(xh * xh, axis=0, keepdims=True) + NORM_EPS) * qkw_ref[r0:r0 + ATTN_HEAD_DIM, :]
        y1 = yh[:ATTN_HALF]
        y2 = yh[ATTN_HALF:]
        o1 = y1 * cos - y2 * sin
        o2 = y1 * sin + y2 * cos
        if hd < ATTN_HEADS:
            scale = ATTN_HEAD_DIM ** -0.5 * math.log2(math.e)
            qt_ref[0, r0:r0 + ATTN_HALF, :] = (o1 * scale).astype(BF16)
            qt_ref[0, r0 + ATTN_HALF:r0 + ATTN_HEAD_DIM, :] = (o2 * scale).astype(BF16)
        else:
            k_rows += [o1, o2]
    k_t = jnp.concatenate(k_rows, axis=0)
    k_ref[0] = jnp.transpose(k_t).astype(BF16)
    vt_ref[0] = qkv_t[ATTN_Q_DIM + ATTN_KV_DIM:].astype(BF16)

    xbc_ref[0] = _dot(ub, wxbc_ref[...])
    dt_raw = _dot(ub, wdt_ref[...]) + dtb_ref[...]
    dt_ref[0] = jnp.maximum(dt_raw, 0.0) + jnp.log1p(jnp.exp(-jnp.abs(dt_raw)))

    r = _dot(ub, wr_ref[...])
    cosr = cosr_ref[...]
    sinr = sinr_ref[...]
    for hd in range(RET_HEADS):
        c0 = hd * RET_DK
        qh = r[:, c0:c0 + RET_DK]
        kh = r[:, RET_DIM + c0:RET_DIM + c0 + RET_DK]
        rq_ref[0, :, c0:c0 + RET_DK] = (qh * cosr + pltpu.roll(qh, RET_DK // 2, 1) * sinr).astype(BF16)
        rk_ref[0, :, c0:c0 + RET_DK] = ((kh * cosr + pltpu.roll(kh, RET_DK // 2, 1) * sinr)
                                        * (RET_DK ** -0.5)).astype(BF16)
    rv_ref[0] = r[:, 2 * RET_DIM:].astype(BF16)


def _inproj_call(h, mods, n1w, wts, tabs, nl):
    b, l, _ = h.shape
    nt = l // TILE
    wqkv, qkw, wxbc, wdt, dtb, wr = wts
    cos_a, sin_a, cos_r, sin_r = tabs
    const = lambda shape: pl.BlockSpec(shape, lambda bi, i: (0,) * len(shape))
    tok = lambda c: pl.BlockSpec((1, TILE, c), lambda bi, i: (bi, i, 0))
    tok_t = lambda r: pl.BlockSpec((1, r, TILE), lambda bi, i: (bi, 0, i))
    weights_bytes = 2 * (wqkv.size + wxbc.size + wdt.size + wr.size) + 4 * qkw.size
    return pl.pallas_call(
        _inproj_kernel,
        out_shape=(jax.ShapeDtypeStruct((b, ATTN_Q_DIM, l), BF16),
                   jax.ShapeDtypeStruct((b, l, ATTN_KV_DIM), BF16),
                   jax.ShapeDtypeStruct((b, ATTN_KV_DIM, l), BF16),
                   jax.ShapeDtypeStruct((b, l, SSD_CONV_DIM), F32),
                   jax.ShapeDtypeStruct((b, l, DT_PAD), F32),
                   jax.ShapeDtypeStruct((b, l, RET_DIM), BF16),
                   jax.ShapeDtypeStruct((b, l, RET_DIM), BF16),
                   jax.ShapeDtypeStruct((b, l, RET_DIM), BF16)),
        grid=(b, nt),
        in_specs=[tok(D_MODEL),
                  pl.BlockSpec((1, 6, D_MODEL), lambda bi, i: (jnp.where(i == nl, 4, bi), 0, 0)),
                  const((1, D_MODEL)),
                  const(wqkv.shape), const(qkw.shape),
                  pl.BlockSpec((ATTN_HALF, TILE), lambda bi, i: (0, i)),
                  pl.BlockSpec((ATTN_HALF, TILE), lambda bi, i: (0, i)),
                  const(wxbc.shape), const(wdt.shape), const(dtb.shape), const(wr.shape),
                  pl.BlockSpec((TILE, RET_DK), lambda bi, i: (i, 0)),
                  pl.BlockSpec((TILE, RET_DK), lambda bi, i: (i, 0))],
        out_specs=(tok_t(ATTN_Q_DIM), tok(ATTN_KV_DIM), tok_t(ATTN_KV_DIM), tok(SSD_CONV_DIM), tok(DT_PAD),
                   tok(RET_DIM), tok(RET_DIM), tok(RET_DIM)),
        compiler_params=pltpu.CompilerParams(dimension_semantics=("arbitrary", "arbitrary"),
                                             vmem_limit_bytes=_vmem_limit(weights_bytes)),
        name="inproj",
    )(h, mods, n1w, wqkv, qkw, cos_a, sin_a, wxbc, wdt, dtb, wr, cos_r, sin_r)


def _attn_kernel(qt_ref, k_ref, vt_ref, o_ref, qpad_sc, m_sc, acc_sc, s0_sc, s1_sc, mx0_sc, mx1_sc, *, nl, kc):
    kv = pl.program_id(1)
    i = pl.program_id(2)

    row = lax.broadcasted_iota(jnp.int32, (2 * ATTN_HEAD_DIM, TILE), 0)
    keep = (row >= ATTN_HEAD_DIM).astype(jnp.int32) == kv
    for g in range(ATTN_GROUP):
        qg = qt_ref[0, g * ATTN_HEAD_DIM:(g + 1) * ATTN_HEAD_DIM, :]
        qq = jnp.concatenate([qg, qg], axis=0)
        qpad_sc[:, g * TILE:(g + 1) * TILE] = jnp.where(keep, qq, jnp.zeros_like(qq))
    m_sc[...] = jnp.full(m_sc.shape, NEG_BIG, F32)
    acc_sc[...] = jnp.zeros(acc_sc.shape, F32)

    wide = ATTN_GROUP * TILE

    def scores(off, size, c0, width):
        return _dot(k_ref[0, pl.ds(off, size), :], qpad_sc[:, c0:c0 + width])

    def accumulate(off, size, c0, s, mx):
        cols = slice(c0, c0 + ATTN_PV_TILE)
        m_old = m_sc[:, cols]
        m_new = jnp.maximum(m_old, mx)
        m_sc[:, cols] = m_new
        p = jnp.exp2(s - m_new).astype(BF16)
        v_aug = jnp.concatenate([vt_ref[0, :, pl.ds(off, size)], jnp.ones((ATTN_ONES_ROWS, size), BF16)], axis=0)
        acc_sc[:, cols] = jnp.exp2(m_old - m_new) * acc_sc[:, cols] + _dot(v_aug, p)

    def qk(off, size, c0, s_sc, mx_sc):
        s = scores(off, size, c0, ATTN_QK_TILE)
        s_sc[:size, c0:c0 + ATTN_QK_TILE] = s
        mx_sc[:, c0:c0 + ATTN_QK_TILE] = jnp.max(s, axis=0, keepdims=True)

    def pv(off, size, c0, s_sc, mx_sc):
        accumulate(off, size, c0, s_sc[:size, c0:c0 + ATTN_PV_TILE], mx_sc[:, c0:c0 + ATTN_PV_TILE])

    bufs = ((s0_sc, mx0_sc), (s1_sc, mx1_sc))

    def step(c, off_qk, off_pv):
        for q0 in range(0, wide, ATTN_QK_TILE):
            if c < len(sizes):
                qk(off_qk, sizes[c], q0, *bufs[c % 2])
            if c >= 1:
                for c0 in range(q0, q0 + ATTN_QK_TILE, ATTN_PV_TILE):
                    pv(off_pv, sizes[c - 1], c0, *bufs[(c - 1) % 2])

    n_mid, rest = divmod(nl * TILE, kc)
    sizes = [TILE] + [kc] * n_mid + ([rest] if rest else [])
    offs = [sum(sizes[:c]) for c in range(len(sizes) + 1)]
    steady = [c for c in range(1, len(sizes)) if sizes[c] == kc and sizes[c - 1] == kc]
    pairs = len(steady) // 2
    looped = steady[:2 * pairs]

    @pl.when(i != nl)
    def _():
        for c in range(len(sizes) + 1):
            if c in looped:
                if c == looped[0]:
                    def body(j, carry):
                        for d in range(2):
                            off = pl.multiple_of(offs[looped[0] + d] + 2 * j * kc, TILE)
                            step(looped[0] + d, off, off - kc)
                        return carry
                    lax.fori_loop(0, pairs, body, 0)
            else:
                step(c, offs[c], offs[c - 1] if c >= 1 else None)

    @pl.when(i == nl)
    def _():
        for c0 in range(0, wide, ATTN_PV_TILE):
            s = scores(nl * TILE, TILE, c0, ATTN_PV_TILE)
            accumulate(nl * TILE, TILE, c0, s, jnp.max(s, axis=0, keepdims=True))

    acc = acc_sc[...]
    out = acc[:ATTN_HEAD_DIM] / acc[ATTN_HEAD_DIM:ATTN_HEAD_DIM + 1]
    for g in range(ATTN_GROUP):
        o_ref[0, g * ATTN_HEAD_DIM:(g + 1) * ATTN_HEAD_DIM, :] = out[:, g * TILE:(g + 1) * TILE].astype(BF16)


def _attn_call(qt, k, vt, nl):
    b, _, l = qt.shape
    nt = l // TILE
    kc = ATTN_KEY_CHUNK
    gq = ATTN_GROUP * ATTN_HEAD_DIM
    wide = ATTN_GROUP * TILE
    resident = 2 * (l * ATTN_KV_DIM + ATTN_HEAD_DIM * l) + 4 * kc * wide
    return pl.pallas_call(
        functools.partial(_attn_kernel, nl=nl, kc=kc),
        out_shape=jax.ShapeDtypeStruct((b, ATTN_Q_DIM, l), BF16),
        grid=(b, ATTN_KV_HEADS, nt),
        in_specs=[pl.BlockSpec((1, gq, TILE), lambda bi, kv, i: (bi, kv, i)),
                  pl.BlockSpec((1, l, ATTN_KV_DIM), lambda bi, kv, i: (bi, 0, 0)),
                  pl.BlockSpec((1, ATTN_HEAD_DIM, l), lambda bi, kv, i: (bi, kv, 0))],
        out_specs=pl.BlockSpec((1, gq, TILE), lambda bi, kv, i: (bi, kv, i)),
        scratch_shapes=[pltpu.VMEM((2 * ATTN_HEAD_DIM, wide), BF16),
                        pltpu.VMEM((1, wide), F32),
                        pltpu.VMEM((ATTN_HEAD_DIM + ATTN_ONES_ROWS, wide), F32),
                        pltpu.VMEM((kc, wide), F32), pltpu.VMEM((kc, wide), F32),
                        pltpu.VMEM((1, wide), F32), pltpu.VMEM((1, wide), F32)],
        compiler_params=pltpu.CompilerParams(dimension_semantics=("arbitrary",) * 3,
                                             vmem_limit_bytes=_vmem_limit(resident)),
        name="attn",
    )(qt, k, vt)


def _scan_tile(s, nl, reverse):
    return jnp.where(s == 0, nl, nl - s) if reverse else jnp.where(s == 0, nl, s - 1)


def _tri(reverse):
    r = lax.broadcasted_iota(jnp.int32, (CHUNK, CHUNK), 0)
    c = lax.broadcasted_iota(jnp.int32, (CHUNK, CHUNK), 1)
    return (c >= r) if reverse else (c <= r)


def _lane_bcast(x, c):
    return jnp.broadcast_to(x[:, c:c + 1], (x.shape[0], V7X_LANES))


def _ssd_tile(t, x_ref, xp_ref, xn_ref, dt_ref, cw_ref, cb_ref, alog_ref, dskip_ref, y_ref, st_sc,
              *, nl, reverse):
    x = x_ref[0]
    row = lax.broadcasted_iota(jnp.int32, (8, SSD_CONV_DIM), 0)
    has_prev = jnp.logical_and(t != nl, t != 0)
    has_next = t < nl - 1
    prev_row = jnp.where(has_prev, xp_ref[0, 7:8, :], 0.0)
    next_row = jnp.where(has_next, xn_ref[0, 0:1, :], 0.0)
    x_m1 = pltpu.roll(x, 1, 0)
    x_m1 = jnp.concatenate([jnp.where(row == 0, prev_row, x_m1[:8]), x_m1[8:]], axis=0)
    x_p1 = pltpu.roll(x, TILE - 1, 0)
    x_p1 = jnp.concatenate([x_p1[:TILE - 8], jnp.where(row == 7, next_row, x_p1[TILE - 8:])], axis=0)
    xs = _silu(cw_ref[0:1, :] * x_m1 + cw_ref[1:2, :] * x + cw_ref[2:3, :] * x_p1 + cb_ref[...])

    tri = _tri(reverse)
    tri_b = tri.astype(BF16)
    lane = lax.broadcasted_iota(jnp.int32, (CHUNK, V7X_LANES), 1)
    left = lane < SSD_HEAD_DIM
    col0 = SSD_HEADS if reverse else 0
    a_neg = -jnp.exp(alog_ref[...])
    last = 0 if reverse else CHUNK - 1

    chunks = range(TILE // CHUNK)
    for ci in (reversed(chunks) if reverse else chunks):
        r0 = ci * CHUNK
        xc = xs[r0:r0 + CHUNK]
        dtc = dt_ref[0, r0:r0 + CHUNK, :]
        a = dtc * a_neg
        hi, mid, lo = _split3(a)
        a_cum = _dot(tri_b, hi) + _dot(tri_b, mid) + _dot(tri_b, lo)
        a_cum_t = jnp.transpose(a_cum)
        for g in range(SSD_GROUPS):
            bm = xc[:, SSD_D_INNER + g * SSD_STATE:SSD_D_INNER + (g + 1) * SSD_STATE]
            cm = xc[:, SSD_D_INNER + (SSD_GROUPS + g) * SSD_STATE:SSD_D_INNER + (SSD_GROUPS + g + 1) * SSD_STATE]
            bmb = bm.astype(BF16)
            cmb = cm.astype(BF16)
            cb = _dot_nt(cmb, bmb)
            st_prev = st_sc[g]
            y_off = _dot(cmb, st_prev.astype(BF16))
            xdd_pairs = []
            tot = []
            for pr in range(SSD_HEADS_PER_GROUP // 2):
                h0 = g * SSD_HEADS_PER_GROUP + 2 * pr
                lanes0 = h0 * SSD_HEAD_DIM
                x2 = xc[:, lanes0:lanes0 + V7X_LANES]
                acol = [_lane_bcast(a_cum, col0 + h0 + j) for j in range(2)]
                dcol = [_lane_bcast(dtc, col0 + h0 + j) for j in range(2)]
                a2 = jnp.where(left, acol[0], acol[1])
                xd2 = x2 * jnp.where(left, dcol[0], dcol[1])
                xd2b = xd2.astype(BF16)
                y_pair = []
                for j in range(2):
                    arow = a_cum_t[col0 + h0 + j:col0 + h0 + j + 1, :]
                    lmat = jnp.where(tri, jnp.exp(jnp.where(tri, acol[j] - arow, 0.0)), 0.0)
                    y_pair.append(_dot((cb * lmat).astype(BF16), xd2b))
                y2 = jnp.where(left, y_pair[0], y_pair[1])
                y2 = y2 + jnp.exp(a2) * y_off[:, pr * V7X_LANES:(pr + 1) * V7X_LANES]
                if not reverse:
                    y2 = y2 + dskip_ref[:, lanes0:lanes0 + V7X_LANES] * x2
                y_ref[0, r0:r0 + CHUNK, lanes0:lanes0 + V7X_LANES] = y2
                a_tot = a2[last:last + 1, :]
                xdd_pairs.append((xd2 * jnp.exp(a_tot - a2)).astype(BF16))
                tot.append(a_tot)
            xdd = jnp.concatenate(xdd_pairs, axis=1)
            st_sc[g] = jnp.exp(jnp.concatenate(tot, axis=1)) * st_prev + _dot_tn(bmb, xdd)


def _ssd_kernel(xf_ref, xpf_ref, xnf_ref, dtf_ref, xb_ref, xpb_ref, xnb_ref, dtb_ref, cw_ref, cb_ref, alog_ref,
                dskip_ref, yf_ref, yb_ref, stf_sc, stb_sc, *, nl):
    s = pl.program_id(1)

    @pl.when(s == 0)
    def _():
        stf_sc[...] = jnp.zeros(stf_sc.shape, F32)
        stb_sc[...] = jnp.zeros(stb_sc.shape, F32)

    _ssd_tile(_scan_tile(s, nl, False), xf_ref, xpf_ref, xnf_ref, dtf_ref, cw_ref, cb_ref, alog_ref, dskip_ref,
              yf_ref, stf_sc, nl=nl, reverse=False)
    _ssd_tile(_scan_tile(s, nl, True), xb_ref, xpb_ref, xnb_ref, dtb_ref, cw_ref, cb_ref, alog_ref, dskip_ref,
              yb_ref, stb_sc, nl=nl, reverse=True)


def _ssd_call(xbc, dt, cw, cb, alog, dskip, nl):
    b, l, _ = xbc.shape
    nt = l // TILE
    rows8 = TILE // 8
    const = lambda shape: pl.BlockSpec(shape, lambda bi, s: (0,) * len(shape))

    def stream(reverse):
        tile_of = lambda s: _scan_tile(s, nl, reverse)
        return [pl.BlockSpec((1, TILE, SSD_CONV_DIM), lambda bi, s: (bi, tile_of(s), 0)),
                pl.BlockSpec((1, 8, SSD_CONV_DIM), lambda bi, s: (bi, jnp.maximum(tile_of(s) * rows8 - 1, 0), 0)),
                pl.BlockSpec((1, 8, SSD_CONV_DIM),
                             lambda bi, s: (bi, jnp.minimum((tile_of(s) + 1) * rows8, nt * rows8 - 1), 0)),
                pl.BlockSpec((1, TILE, DT_PAD), lambda bi, s: (bi, tile_of(s), 0))]

    out = lambda reverse: pl.BlockSpec((1, TILE, SSD_D_INNER), lambda bi, s: (bi, _scan_tile(s, nl, reverse), 0))
    state = pltpu.VMEM((SSD_GROUPS, SSD_STATE, SSD_HEADS_PER_GROUP * SSD_HEAD_DIM), F32)
    y = jax.ShapeDtypeStruct((b, l, SSD_D_INNER), F32)
    return pl.pallas_call(
        functools.partial(_ssd_kernel, nl=nl),
        out_shape=(y, y),
        grid=(b, nt),
        in_specs=stream(False) + stream(True) + [const(cw.shape), const(cb.shape), const(alog.shape),
                                                 const(dskip.shape)],
        out_specs=(out(False), out(True)),
        scratch_shapes=[state, state],
        compiler_params=pltpu.CompilerParams(dimension_semantics=("arbitrary", "arbitrary")),
        name="ssd",
    )(xbc, xbc, xbc, dt, xbc, xbc, xbc, dt, cw, cb, alog, dskip)


def _ret_tile(q_ref, k_ref, v_ref, ld_ref, y_ref, st_sc, *, reverse):
    tri = _tri(reverse)
    r = lax.broadcasted_iota(jnp.int32, (CHUNK, CHUNK), 0)
    c = lax.broadcasted_iota(jnp.int32, (CHUNK, CHUNK), 1)
    dist = jnp.maximum((c - r) if reverse else (r - c), 0).astype(F32)
    pos = ((CHUNK - 1 - r) if reverse else r).astype(F32)
    col0 = RET_HEADS if reverse else 0

    decay = []
    for hd in range(RET_HEADS):
        lg = jnp.broadcast_to(-jnp.exp(ld_ref[0:1, col0 + hd:col0 + hd + 1]), (CHUNK, CHUNK))
        decay.append((jnp.where(tri, jnp.exp(dist * lg), 0.0), jnp.exp((pos + 1.0) * lg),
                      jnp.exp((CHUNK - 1.0 - pos) * lg), jnp.exp(CHUNK * lg)))

    chunks = range(TILE // CHUNK)
    for ci in (reversed(chunks) if reverse else chunks):
        r0 = ci * CHUNK
        for hd in range(RET_HEADS):
            c0 = hd * RET_DK
            dmat, q_dec, k_dec, chunk_dec = decay[hd]
            q = q_ref[0, r0:r0 + CHUNK, c0:c0 + RET_DK]
            k = k_ref[0, r0:r0 + CHUNK, c0:c0 + RET_DK]
            v = v_ref[0, r0:r0 + CHUNK, c0:c0 + RET_DV]
            sc = _dot_nt(q, k) * dmat
            st_prev = st_sc[hd]
            y_ref[0, r0:r0 + CHUNK, c0:c0 + RET_DV] = (_dot(sc.astype(BF16), v)
                                                        + _dot(q, st_prev.astype(BF16)) * q_dec)
            st_sc[hd] = chunk_dec * st_prev + _dot_tn((k.astype(F32) * k_dec).astype(BF16), v)


def _ret_kernel(qf_ref, kf_ref, vf_ref, qb_ref, kb_ref, vb_ref, ld_ref, yf_ref, yb_ref, stf_sc, stb_sc):
    @pl.when(pl.program_id(1) == 0)
    def _():
        stf_sc[...] = jnp.zeros(stf_sc.shape, F32)
        stb_sc[...] = jnp.zeros(stb_sc.shape, F32)

    _ret_tile(qf_ref, kf_ref, vf_ref, ld_ref, yf_ref, stf_sc, reverse=False)
    _ret_tile(qb_ref, kb_ref, vb_ref, ld_ref, yb_ref, stb_sc, reverse=True)


def _ret_call(rq, rk, rv, ld, nl):
    b, l, _ = rq.shape
    nt = l // TILE
    tok = lambda reverse: pl.BlockSpec((1, TILE, RET_DIM), lambda bi, s: (bi, _scan_tile(s, nl, reverse), 0))
    state = pltpu.VMEM((RET_HEADS, RET_DK, RET_DV), F32)
    y = jax.ShapeDtypeStruct((b, l, RET_DIM), F32)
    return pl.pallas_call(
        _ret_kernel,
        out_shape=(y, y),
        grid=(b, nt),
        in_specs=[tok(False)] * 3 + [tok(True)] * 3 + [pl.BlockSpec(ld.shape, lambda bi, s: (0, 0))],
        out_specs=(tok(False), tok(True)),
        scratch_shapes=[state, state],
        compiler_params=pltpu.CompilerParams(dimension_semantics=("arbitrary", "arbitrary")),
        name="ret",
    )(rq, rk, rv, rq, rk, rv, ld)


def _merge_kernel(h_ref, mod_ref, n1w_ref, ot_ref, sf_ref, sb_ref, rf_ref, rb_ref, wzg_ref, wgate_ref,
                  snw_ref, gnw_ref, wb_ref, wout_ref, o_ref):
    h = h_ref[0]
    u = _rms_rows(h, n1w_ref[...]) * (1.0 + mod_ref[0, 1:2, :]) + mod_ref[0, 0:1, :]
    ub = u.astype(BF16)
    zg = _dot(ub, wzg_ref[...])
    gates = _sigmoid(_dot(ub, wgate_ref[...]))

    br_attn = _dot_tn(ot_ref[0], wb_ref[0])

    y = (sf_ref[0] + sb_ref[0]) * _silu(zg[:, :SSD_D_INNER])
    br_ssd = _dot(_rms_rows(y, snw_ref[...]).astype(BF16), wb_ref[1])

    yr = rf_ref[0] + rb_ref[0]
    heads = []
    for hd in range(RET_HEADS):
        yh = yr[:, hd * RET_DV:(hd + 1) * RET_DV]
        yc = yh - jnp.mean(yh, axis=-1, keepdims=True)
        heads.append(yc * lax.rsqrt(jnp.mean(yc * yc, axis=-1, keepdims=True) + NORM_EPS))
    yn = jnp.concatenate(heads, axis=1) * gnw_ref[...] * _silu(zg[:, SSD_D_INNER:])
    br_ret = _dot(yn.astype(BF16), wb_ref[2])

    merged = (gates[:, :D_MODEL] * br_attn + gates[:, D_MODEL:2 * D_MODEL] * br_ssd
              + gates[:, 2 * D_MODEL:] * br_ret)
    o_ref[0] = h + mod_ref[0, 2:3, :] * _dot(merged.astype(BF16), wout_ref[...])


def _merge_call(h, mods, n1w, ot, sf, sb, rf, rb, wts, nl):
    b, l, _ = h.shape
    nt = l // TILE
    wzg, wgate, snw, gnw, wb, wout = wts
    const = lambda shape: pl.BlockSpec(shape, lambda bi, i: (0,) * len(shape))
    tok = lambda c: pl.BlockSpec((1, TILE, c), lambda bi, i: (bi, i, 0))
    weights_bytes = 2 * (wzg.size + wgate.size + wb.size + wout.size)
    return pl.pallas_call(
        _merge_kernel,
        out_shape=jax.ShapeDtypeStruct((b, l, D_MODEL), F32),
        grid=(b, nt),
        in_specs=[tok(D_MODEL),
                  pl.BlockSpec((1, 6, D_MODEL), lambda bi, i: (jnp.where(i == nl, 4, bi), 0, 0)),
                  const((1, D_MODEL)),
                  pl.BlockSpec((1, ATTN_Q_DIM, TILE), lambda bi, i: (bi, 0, i)),
                  tok(SSD_D_INNER), tok(SSD_D_INNER), tok(RET_DIM), tok(RET_DIM),
                  const(wzg.shape), const(wgate.shape), const(snw.shape), const(gnw.shape),
                  const(wb.shape), const(wout.shape)],
        out_specs=tok(D_MODEL),
        compiler_params=pltpu.CompilerParams(dimension_semantics=("arbitrary", "arbitrary"),
                                             vmem_limit_bytes=_vmem_limit(weights_bytes)),
        name="merge",
    )(h, mods, n1w, ot, sf, sb, rf, rb, wzg, wgate, snw, gnw, wb, wout)


def _mlp_kernel(h_ref, mod_ref, n2w_ref, w1_ref, w2_ref, fw_ref, o_ref, *, final):
    h = h_ref[0]
    v = _rms_rows(h, n2w_ref[...]) * (1.0 + mod_ref[0, 4:5, :]) + mod_ref[0, 3:4, :]
    a = jnp.maximum(_dot(v.astype(BF16), w1_ref[...]), 0.0)
    out = h + mod_ref[0, 5:6, :] * _dot((a * a).astype(BF16), w2_ref[...])
    o_ref[0] = _rms_rows(out, fw_ref[...]) if final else out


def _mlp_call(h, mods, n2w, w1, w2, fw, nl, final):
    b, l, _ = h.shape
    nt = nl if final else l // TILE
    const = lambda shape: pl.BlockSpec(shape, lambda bi, i: (0,) * len(shape))
    tok = pl.BlockSpec((1, TILE, D_MODEL), lambda bi, i: (bi, i, 0))
    return pl.pallas_call(
        functools.partial(_mlp_kernel, final=final),
        out_shape=jax.ShapeDtypeStruct((b, nt * TILE, D_MODEL), F32),
        grid=(b, nt),
        in_specs=[tok,
                  pl.BlockSpec((1, 6, D_MODEL), lambda bi, i: (jnp.where(i == nl, 4, bi), 0, 0)),
                  const((1, D_MODEL)), const(w1.shape), const(w2.shape), const((1, D_MODEL))],
        out_specs=tok,
        compiler_params=pltpu.CompilerParams(dimension_semantics=("arbitrary", "arbitrary"),
                                             vmem_limit_bytes=_vmem_limit(2 * (w1.size + w2.size))),
        name="mlp_final" if final else "mlp",
    )(h, mods, n2w, w1, w2, fw)


def _rope_tables(n, m):
    rows = n // GRID_W
    row = jnp.repeat(jnp.arange(rows, dtype=F32), GRID_W)
    col = jnp.tile(jnp.arange(GRID_W, dtype=F32), rows)
    inv = ROPE_THETA ** (-jnp.arange(ATTN_AXIS_FREQS, dtype=F32) / ATTN_AXIS_FREQS)
    ang = jnp.concatenate([row[:, None] * inv, col[:, None] * inv], axis=-1)
    cos_a = jnp.concatenate([jnp.cos(ang), jnp.ones((m, ATTN_HALF), F32)], axis=0).T
    sin_a = jnp.concatenate([jnp.sin(ang), jnp.zeros((m, ATTN_HALF), F32)], axis=0).T
    pos = jnp.concatenate([jnp.arange(n, dtype=F32) + m, jnp.arange(m, dtype=F32)])
    inv_r = ROPE_THETA ** (-jnp.linspace(0.0, 1.0, RET_DK // 2, dtype=F32))
    ang_r = pos[:, None] * inv_r
    cos_r = jnp.concatenate([jnp.cos(ang_r), jnp.cos(ang_r)], axis=-1)
    sin_r = jnp.concatenate([-jnp.sin(ang_r), jnp.sin(ang_r)], axis=-1)
    return cos_a, sin_a, cos_r, sin_r


def _layer_weights(w_in, q_norm, k_norm, dt_bias):
    offs = [0]
    for sz in IN_SPLITS:
        offs.append(offs[-1] + sz)
    col = lambda j: w_in[:, offs[j]:offs[j + 1]]
    wqkv = jnp.concatenate([col(0), col(1), col(2)], axis=1).T.astype(BF16)
    qkw = jnp.concatenate([jnp.tile(q_norm, ATTN_HEADS), jnp.tile(k_norm, ATTN_KV_HEADS)])
    qkw = jnp.broadcast_to(qkw[:, None], (ATTN_Q_DIM + ATTN_KV_DIM, TILE)).astype(F32)
    wxbc = col(4).astype(BF16)
    wdt = jnp.pad(col(5), ((0, 0), (0, DT_PAD - 2 * SSD_HEADS))).astype(BF16)
    dtb = jnp.pad(dt_bias.reshape(1, -1), ((0, 0), (0, DT_PAD - 2 * SSD_HEADS))).astype(F32)
    wr = jnp.concatenate([col(6), col(7), col(8)], axis=1).astype(BF16)
    wzg = jnp.concatenate([col(3), col(9)], axis=1).astype(BF16)
    wgate = col(10).astype(BF16)
    return (wqkv, qkw, wxbc, wdt, dtb, wr), (wzg, wgate)


def _pad_lanes(v):
    v = v.reshape(1, -1).astype(F32)
    return jnp.pad(v, ((0, 0), (0, V7X_LANES - v.shape[1])))


def kernel(x, c, ctx, c_ctx, w_mod, b_mod, norm1_w, norm2_w, w_in, attn_q_norm, attn_k_norm, ssd_conv_w,
           ssd_conv_b, ssd_dt_bias, ssd_a_log, ssd_d, ssd_norm_w, ret_log_decay, ret_gn_w, w_branch, w_out,
           w_mlp1, w_mlp2, final_norm_w):
    b, n, d = x.shape
    m = ctx.shape[1]
    depth = w_in.shape[0]
    assert d == D_MODEL and m == TILE and n % TILE == 0 and n % GRID_W == 0 and b <= 4
    nl = n // TILE

    tabs = _rope_tables(n, m)
    cc = jnp.zeros((8, D_MODEL), F32).at[:b].set(c).at[4].set(c_ctx)
    h = jnp.concatenate([x, ctx], axis=1)

    for layer in range(depth):
        final = layer == depth - 1
        in_w, (wzg, wgate) = _layer_weights(w_in[layer], attn_q_norm[layer], attn_k_norm[layer],
                                            ssd_dt_bias[layer])
        n1w = norm1_w[layer].reshape(1, -1)
        mods = _mod_call(cc, w_mod[layer], b_mod[layer])

        qt, k, vt, xbc, dt, rq, rk, rv = _inproj_call(h, mods, n1w, in_w, tabs, nl)
        ot = _attn_call(qt, k, vt, nl)

        cw = jnp.pad(ssd_conv_w[layer], ((0, 8 - SSD_CONV_K), (0, 0)))
        cb = ssd_conv_b[layer].reshape(1, -1)
        alog = _pad_lanes(ssd_a_log[layer])
        dskip = jnp.repeat(ssd_d[layer], SSD_HEAD_DIM).reshape(1, -1)
        sf, sb = _ssd_call(xbc, dt, cw, cb, alog, dskip, nl)

        ld = _pad_lanes(ret_log_decay[layer])
        rf, rb = _ret_call(rq, rk, rv, ld, nl)

        merge_w = (wzg, wgate, ssd_norm_w[layer].reshape(1, -1), ret_gn_w[layer].reshape(1, -1),
                   w_branch[layer].astype(BF16), w_out[layer].astype(BF16))
        h = _merge_call(h, mods, n1w, ot, sf, sb, rf, rb, merge_w, nl)
        h = _mlp_call(h, mods, norm2_w[layer].reshape(1, -1), w_mlp1[layer].astype(BF16),
                      w_mlp2[layer].astype(BF16), final_norm_w.reshape(1, -1), nl, final)
    return h
```

```python
import functools
import math

import jax
import jax.numpy as jnp
from jax import lax
from jax.experimental import pallas as pl
from jax.experimental.pallas import tpu as pltpu

F32 = jnp.float32
BF16 = jnp.bfloat16

D_MODEL = 1024
GRID_W = 64
NORM_EPS = 1e-6
ROPE_THETA = 10000.0

ATTN_HEADS = 8
ATTN_KV_HEADS = 2
ATTN_GROUP = ATTN_HEADS // ATTN_KV_HEADS
ATTN_HEAD_DIM = 64
ATTN_HALF = ATTN_HEAD_DIM // 2
ATTN_AXIS_FREQS = ATTN_HEAD_DIM // 4
ATTN_Q_DIM = ATTN_HEADS * ATTN_HEAD_DIM
ATTN_KV_DIM = ATTN_KV_HEADS * ATTN_HEAD_DIM

SSD_HEADS = 8
SSD_HEAD_DIM = 64
SSD_D_INNER = SSD_HEADS * SSD_HEAD_DIM
SSD_GROUPS = 2
SSD_STATE = 128
SSD_CONV_K = 3
SSD_CONV_DIM = SSD_D_INNER + 2 * SSD_GROUPS * SSD_STATE
SSD_HEADS_PER_GROUP = SSD_HEADS // SSD_GROUPS

RET_HEADS = 4
RET_DK = 128
RET_DV = 128
RET_DIM = RET_HEADS * RET_DK

N_BRANCH = 3
BRANCH_W = 512
MLP_HIDDEN = 4 * D_MODEL

IN_SPLITS = (ATTN_Q_DIM, ATTN_KV_DIM, ATTN_KV_DIM, SSD_D_INNER, SSD_CONV_DIM, 2 * SSD_HEADS,
             RET_DIM, RET_DIM, RET_HEADS * RET_DV, RET_HEADS * RET_DV, N_BRANCH * D_MODEL)

V7X_LANES = 128
V7X_VMEM_BYTES = 64 * 1024 * 1024

TILE = 256
CHUNK = 128
RET_SCAN_CHUNK = TILE
DT_PAD = V7X_LANES
NEG_BIG = -1e30
ATTN_KEY_CHUNK = 768
ATTN_QK_TILE = 256
ATTN_PV_TILE = 256
ATTN_ONES_ROWS = 16


def _vmem_limit(resident_bytes):
    return int(min(V7X_VMEM_BYTES - 8 * 1024 * 1024, 2 * resident_bytes + 16 * 1024 * 1024))


def _sigmoid(x):
    return 0.5 * jnp.tanh(0.5 * x) + 0.5


def _silu(x):
    return x * _sigmoid(x)


def _rms_rows(x, w):
    return x * lax.rsqrt(jnp.mean(x * x, axis=-1, keepdims=True) + NORM_EPS) * w


def _dot(a, b):
    return jnp.dot(a, b, preferred_element_type=F32)


def _dot_nt(a, b):
    return lax.dot_general(a, b, (((1,), (1,)), ((), ())), preferred_element_type=F32)


def _dot_tn(a, b):
    return lax.dot_general(a, b, (((0,), (0,)), ((), ())), preferred_element_type=F32)


def _split3(x):
    hi = x.astype(BF16)
    r1 = x - hi.astype(F32)
    mid = r1.astype(BF16)
    lo = (r1 - mid.astype(F32)).astype(BF16)
    return hi, mid, lo


def _mod_kernel(c_ref, w_ref, b_ref, o_ref):
    s = _silu(c_ref[...])
    o_ref[...] = jnp.dot(s, w_ref[...], preferred_element_type=F32,
                         precision=lax.Precision.HIGHEST) + b_ref[...]


def _mod_call(cc, w_mod, b_mod):
    nblk = w_mod.shape[1] // D_MODEL
    out = pl.pallas_call(
        _mod_kernel,
        out_shape=jax.ShapeDtypeStruct((8, nblk * D_MODEL), F32),
        grid=(nblk,),
        in_specs=[pl.BlockSpec((8, D_MODEL), lambda j: (0, 0)),
                  pl.BlockSpec((D_MODEL, D_MODEL), lambda j: (0, j)),
                  pl.BlockSpec((1, D_MODEL), lambda j: (0, j))],
        out_specs=pl.BlockSpec((8, D_MODEL), lambda j: (0, j)),
        compiler_params=pltpu.CompilerParams(dimension_semantics=("arbitrary",)),
        name="mod",
    )(cc, w_mod, b_mod.reshape(1, -1))
    return out.reshape(8, nblk, D_MODEL)


def _inproj_kernel(h_ref, mod_ref, n1w_ref, wqkv_ref, qkw_ref, cos_ref, sin_ref, wxbc_ref, wdt_ref,
                   dtb_ref, wr_ref, cosr_ref, sinr_ref,
                   qt_ref, k_ref, vt_ref, xbc_ref, dt_ref, rq_ref, rk_ref, rv_ref):
    h = h_ref[0]
    u = _rms_rows(h, n1w_ref[...]) * (1.0 + mod_ref[0, 1:2, :]) + mod_ref[0, 0:1, :]
    ub = u.astype(BF16)

    qkv_t = _dot_nt(wqkv_ref[...], ub)
    cos = cos_ref[...]
    sin = sin_ref[...]
    k_rows = []
    for hd in range(ATTN_HEADS + ATTN_KV_HEADS):
        r0 = hd * ATTN_HEAD_DIM
        xh = qkv_t[r0:r0 + ATTN_HEAD_DIM]
        yh = xh * lax.rsqrt(jnp.mean(xh * xh, axis=0, keepdims=True) + NORM_EPS) * qkw_ref[r0:r0 + ATTN_HEAD_DIM, :]
        y1 = yh[:ATTN_HALF]
        y2 = yh[ATTN_HALF:]
        o1 = y1 * cos - y2 * sin
        o2 = y1 * sin + y2 * cos
        if hd < ATTN_HEADS:
            scale = ATTN_HEAD_DIM ** -0.5 * math.log2(math.e)
            qt_ref[0, r0:r0 + ATTN_HALF, :] = (o1 * scale).astype(BF16)
            qt_ref[0, r0 + ATTN_HALF:r0 + ATTN_HEAD_DIM, :] = (o2 * scale).astype(BF16)
        else:
            k_rows += [o1, o2]
    k_t = jnp.concatenate(k_rows, axis=0)
    k_ref[0] = jnp.transpose(k_t).astype(BF16)
    vt_ref[0] = qkv_t[ATTN_Q_DIM + ATTN_KV_DIM:].astype(BF16)

    xbc_ref[0] = _dot(ub, wxbc_ref[...])
    dt_raw = _dot(ub, wdt_ref[...]) + dtb_ref[...]
    dt_ref[0] = jnp.maximum(dt_raw, 0.0) + jnp.log1p(jnp.exp(-jnp.abs(dt_raw)))

    r = _dot(ub, wr_ref[...])
    cosr = cosr_ref[...]
    sinr = sinr_ref[...]
    for hd in range(RET_HEADS):
        c0 = hd * RET_DK
        qh = r[:, c0:c0 + RET_DK]
        kh = r[:, RET_DIM + c0:RET_DIM + c0 + RET_DK]
        rq_ref[0, :, c0:c0 + RET_DK] = (qh * cosr + pltpu.roll(qh, RET_DK // 2, 1) * sinr).astype(BF16)
        rk_ref[0, :, c0:c0 + RET_DK] = ((kh * cosr + pltpu.roll(kh, RET_DK // 2, 1) * sinr)
                                        * (RET_DK ** -0.5)).astype(BF16)
    rv_ref[0] = r[:, 2 * RET_DIM:].astype(BF16)


def _inproj_call(h, mods, n1w, wts, tabs, nl):
    b, l, _ = h.shape
    nt = l // TILE
    wqkv, qkw, wxbc, wdt, dtb, wr = wts
    cos_a, sin_a, cos_r, sin_r = tabs
    const = lambda shape: pl.BlockSpec(shape, lambda bi, i: (0,) * len(shape))
    tok = lambda c: pl.BlockSpec((1, TILE, c), lambda bi, i: (bi, i, 0))
    tok_t = lambda r: pl.BlockSpec((1, r, TILE), lambda bi, i: (bi, 0, i))
    weights_bytes = 2 * (wqkv.size + wxbc.size + wdt.size + wr.size) + 4 * qkw.size
    return pl.pallas_call(
        _inproj_kernel,
        out_shape=(jax.ShapeDtypeStruct((b, ATTN_Q_DIM, l), BF16),
                   jax.ShapeDtypeStruct((b, l, ATTN_KV_DIM), BF16),
                   jax.ShapeDtypeStruct((b, ATTN_KV_DIM, l), BF16),
                   jax.ShapeDtypeStruct((b, l, SSD_CONV_DIM), F32),
                   jax.ShapeDtypeStruct((b, l, DT_PAD), F32),
                   jax.ShapeDtypeStruct((b, l, RET_DIM), BF16),
                   jax.ShapeDtypeStruct((b, l, RET_DIM), BF16),
                   jax.ShapeDtypeStruct((b, l, RET_DIM), BF16)),
        grid=(b, nt),
        in_specs=[tok(D_MODEL),
                  pl.BlockSpec((1, 6, D_MODEL), lambda bi, i: (jnp.where(i == nl, 4, bi), 0, 0)),
                  const((1, D_MODEL)),
                  const(wqkv.shape), const(qkw.shape),
                  pl.BlockSpec((ATTN_HALF, TILE), lambda bi, i: (0, i)),
                  pl.BlockSpec((ATTN_HALF, TILE), lambda bi, i: (0, i)),
                  const(wxbc.shape), const(wdt.shape), const(dtb.shape), const(wr.shape),
                  pl.BlockSpec((TILE, RET_DK), lambda bi, i: (i, 0)),
                  pl.BlockSpec((TILE, RET_DK), lambda bi, i: (i, 0))],
        out_specs=(tok_t(ATTN_Q_DIM), tok(ATTN_KV_DIM), tok_t(ATTN_KV_DIM), tok(SSD_CONV_DIM), tok(DT_PAD),
                   tok(RET_DIM), tok(RET_DIM), tok(RET_DIM)),
        compiler_params=pltpu.CompilerParams(dimension_semantics=("arbitrary", "arbitrary"),
                                             vmem_limit_bytes=_vmem_limit(weights_bytes)),
        name="inproj",
    )(h, mods, n1w, wqkv, qkw, cos_a, sin_a, wxbc, wdt, dtb, wr, cos_r, sin_r)


def _attn_kernel(qt_ref, k_ref, vt_ref, o_ref, qpad_sc, m_sc, acc_sc, s0_sc, s1_sc, mx0_sc, mx1_sc, *, nl, kc):
    kv = pl.program_id(1)
    i = pl.program_id(2)

    row = lax.broadcasted_iota(jnp.int32, (2 * ATTN_HEAD_DIM, TILE), 0)
    keep = (row >= ATTN_HEAD_DIM).astype(jnp.int32) == kv
    for g in range(ATTN_GROUP):
        qg = qt_ref[0, g * ATTN_HEAD_DIM:(g + 1) * ATTN_HEAD_DIM, :]
        qq = jnp.concatenate([qg, qg], axis=0)
        qpad_sc[:, g * TILE:(g + 1) * TILE] = jnp.where(keep, qq, jnp.zeros_like(qq))
    m_sc[...] = jnp.full(m_sc.shape, NEG_BIG, F32)
    acc_sc[...] = jnp.zeros(acc_sc.shape, F32)

    wide = ATTN_GROUP * TILE

    def scores(off, size, c0, width):
        return _dot(k_ref[0, pl.ds(off, size), :], qpad_sc[:, c0:c0 + width])

    def accumulate(off, size, c0, s, mx):
        cols = slice(c0, c0 + ATTN_PV_TILE)
        m_old = m_sc[:, cols]
        m_new = jnp.maximum(m_old, mx)
        m_sc[:, cols] = m_new
        p = jnp.exp2(s - m_new).astype(BF16)
        v_aug = jnp.concatenate([vt_ref[0, :, pl.ds(off, size)], jnp.ones((ATTN_ONES_ROWS, size), BF16)], axis=0)
        acc_sc[:, cols] = jnp.exp2(m_old - m_new) * acc_sc[:, cols] + _dot(v_aug, p)

    def qk(off, size, c0, s_sc, mx_sc):
        s = scores(off, size, c0, ATTN_QK_TILE)
        s_sc[:size, c0:c0 + ATTN_QK_TILE] = s
        mx_sc[:, c0:c0 + ATTN_QK_TILE] = jnp.max(s, axis=0, keepdims=True)

    def pv(off, size, c0, s_sc, mx_sc):
        accumulate(off, size, c0, s_sc[:size, c0:c0 + ATTN_PV_TILE], mx_sc[:, c0:c0 + ATTN_PV_TILE])

    bufs = ((s0_sc, mx0_sc), (s1_sc, mx1_sc))

    def step(c, off_qk, off_pv):
        for q0 in range(0, wide, ATTN_QK_TILE):
            if c < len(sizes):
                qk(off_qk, sizes[c], q0, *bufs[c % 2])
            if c >= 1:
                for c0 in range(q0, q0 + ATTN_QK_TILE, ATTN_PV_TILE):
                    pv(off_pv, sizes[c - 1], c0, *bufs[(c - 1) % 2])

    n_mid, rest = divmod(nl * TILE, kc)
    sizes = [TILE] + [kc] * n_mid + ([rest] if rest else [])
    offs = [sum(sizes[:c]) for c in range(len(sizes) + 1)]
    steady = [c for c in range(1, len(sizes)) if sizes[c] == kc and sizes[c - 1] == kc]
    pairs = len(steady) // 2
    looped = steady[:2 * pairs]

    @pl.when(i != nl)
    def _():
        for c in range(len(sizes) + 1):
            if c in looped:
                if c == looped[0]:
                    def body(j, carry):
                        for d in range(2):
                            off = pl.multiple_of(offs[looped[0] + d] + 2 * j * kc, TILE)
                            step(looped[0] + d, off, off - kc)
                        return carry
                    lax.fori_loop(0, pairs, body, 0)
            else:
                step(c, offs[c], offs[c - 1] if c >= 1 else None)

    @pl.when(i == nl)
    def _():
        for c0 in range(0, wide, ATTN_PV_TILE):
            s = scores(nl * TILE, TILE, c0, ATTN_PV_TILE)
            accumulate(nl * TILE, TILE, c0, s, jnp.max(s, axis=0, keepdims=True))

    acc = acc_sc[...]
    out = acc[:ATTN_HEAD_DIM] / acc[ATTN_HEAD_DIM:ATTN_HEAD_DIM + 1]
    for g in range(ATTN_GROUP):
        o_ref[0, g * ATTN_HEAD_DIM:(g + 1) * ATTN_HEAD_DIM, :] = out[:, g * TILE:(g + 1) * TILE].astype(BF16)


def _attn_call(qt, k, vt, nl):
    b, _, l = qt.shape
    nt = l // TILE
    kc = ATTN_KEY_CHUNK
    gq = ATTN_GROUP * ATTN_HEAD_DIM
    wide = ATTN_GROUP * TILE
    resident = 2 * (l * ATTN_KV_DIM + ATTN_HEAD_DIM * l) + 4 * kc * wide
    return pl.pallas_call(
        functools.partial(_attn_kernel, nl=nl, kc=kc),
        out_shape=jax.ShapeDtypeStruct((b, ATTN_Q_DIM, l), BF16),
        grid=(b, ATTN_KV_HEADS, nt),
        in_specs=[pl.BlockSpec((1, gq, TILE), lambda bi, kv, i: (bi, kv, i)),
                  pl.BlockSpec((1, l, ATTN_KV_DIM), lambda bi, kv, i: (bi, 0, 0)),
                  pl.BlockSpec((1, ATTN_HEAD_DIM, l), lambda bi, kv, i: (bi, kv, 0))],
        out_specs=pl.BlockSpec((1, gq, TILE), lambda bi, kv, i: (bi, kv, i)),
        scratch_shapes=[pltpu.VMEM((2 * ATTN_HEAD_DIM, wide), BF16),
                        pltpu.VMEM((1, wide), F32),
                        pltpu.VMEM((ATTN_HEAD_DIM + ATTN_ONES_ROWS, wide), F32),
                        pltpu.VMEM((kc, wide), F32), pltpu.VMEM((kc, wide), F32),
                        pltpu.VMEM((1, wide), F32), pltpu.VMEM((1, wide), F32)],
        compiler_params=pltpu.CompilerParams(dimension_semantics=("arbitrary",) * 3,
                                             vmem_limit_bytes=_vmem_limit(resident)),
        name="attn",
    )(qt, k, vt)


def _scan_tile(s, nl, reverse):
    return jnp.where(s == 0, nl, nl - s) if reverse else jnp.where(s == 0, nl, s - 1)


def _tri(reverse, n=CHUNK):
    r = lax.broadcasted_iota(jnp.int32, (n, n), 0)
    c = lax.broadcasted_iota(jnp.int32, (n, n), 1)
    return (c >= r) if reverse else (c <= r)


def _lane_bcast(x, c):
    return jnp.broadcast_to(x[:, c:c + 1], (x.shape[0], V7X_LANES))


def _ssd_tile(t, x_ref, xp_ref, xn_ref, dt_ref, cw_ref, cb_ref, alog_ref, dskip_ref, y_ref, st_sc,
              *, nl, reverse):
    x = x_ref[0]
    row = lax.broadcasted_iota(jnp.int32, (8, SSD_CONV_DIM), 0)
    has_prev = jnp.logical_and(t != nl, t != 0)
    has_next = t < nl - 1
    prev_row = jnp.where(has_prev, xp_ref[0, 7:8, :], 0.0)
    next_row = jnp.where(has_next, xn_ref[0, 0:1, :], 0.0)
    x_m1 = pltpu.roll(x, 1, 0)
    x_m1 = jnp.concatenate([jnp.where(row == 0, prev_row, x_m1[:8]), x_m1[8:]], axis=0)
    x_p1 = pltpu.roll(x, TILE - 1, 0)
    x_p1 = jnp.concatenate([x_p1[:TILE - 8], jnp.where(row == 7, next_row, x_p1[TILE - 8:])], axis=0)
    xs = _silu(cw_ref[0:1, :] * x_m1 + cw_ref[1:2, :] * x + cw_ref[2:3, :] * x_p1 + cb_ref[...])

    tri = _tri(reverse)
    tri_b = tri.astype(BF16)
    lane = lax.broadcasted_iota(jnp.int32, (CHUNK, V7X_LANES), 1)
    left = lane < SSD_HEAD_DIM
    col0 = SSD_HEADS if reverse else 0
    a_neg = -jnp.exp(alog_ref[...])
    last = 0 if reverse else CHUNK - 1

    chunks = range(TILE // CHUNK)
    for ci in (reversed(chunks) if reverse else chunks):
        r0 = ci * CHUNK
        xc = xs[r0:r0 + CHUNK]
        dtc = dt_ref[0, r0:r0 + CHUNK, :]
        a = dtc * a_neg
        hi, mid, lo = _split3(a)
        a_cum = _dot(tri_b, hi) + _dot(tri_b, mid) + _dot(tri_b, lo)
        a_cum_t = jnp.transpose(a_cum)
        for g in range(SSD_GROUPS):
            bm = xc[:, SSD_D_INNER + g * SSD_STATE:SSD_D_INNER + (g + 1) * SSD_STATE]
            cm = xc[:, SSD_D_INNER + (SSD_GROUPS + g) * SSD_STATE:SSD_D_INNER + (SSD_GROUPS + g + 1) * SSD_STATE]
            bmb = bm.astype(BF16)
            cmb = cm.astype(BF16)
            cb = _dot_nt(cmb, bmb)
            st_prev = st_sc[g]
            y_off = _dot(cmb, st_prev.astype(BF16))
            xdd_pairs = []
            tot = []
            for pr in range(SSD_HEADS_PER_GROUP // 2):
                h0 = g * SSD_HEADS_PER_GROUP + 2 * pr
                lanes0 = h0 * SSD_HEAD_DIM
                x2 = xc[:, lanes0:lanes0 + V7X_LANES]
                acol = [_lane_bcast(a_cum, col0 + h0 + j) for j in range(2)]
                dcol = [_lane_bcast(dtc, col0 + h0 + j) for j in range(2)]
                a2 = jnp.where(left, acol[0], acol[1])
                xd2 = x2 * jnp.where(left, dcol[0], dcol[1])
                xd2b = xd2.astype(BF16)
                y_pair = []
                for j in range(2):
                    arow = a_cum_t[col0 + h0 + j:col0 + h0 + j + 1, :]
                    lmat = jnp.where(tri, jnp.exp(jnp.where(tri, acol[j] - arow, 0.0)), 0.0)
                    y_pair.append(_dot((cb * lmat).astype(BF16), xd2b))
                y2 = jnp.where(left, y_pair[0], y_pair[1])
                y2 = y2 + jnp.exp(a2) * y_off[:, pr * V7X_LANES:(pr + 1) * V7X_LANES]
                if not reverse:
                    y2 = y2 + dskip_ref[:, lanes0:lanes0 + V7X_LANES] * x2
                y_ref[0, r0:r0 + CHUNK, lanes0:lanes0 + V7X_LANES] = y2
                a_tot = a2[last:last + 1, :]
                xdd_pairs.append((xd2 * jnp.exp(a_tot - a2)).astype(BF16))
                tot.append(a_tot)
            xdd = jnp.concatenate(xdd_pairs, axis=1)
            st_sc[g] = jnp.exp(jnp.concatenate(tot, axis=1)) * st_prev + _dot_tn(bmb, xdd)


def _ssd_kernel(xf_ref, xpf_ref, xnf_ref, dtf_ref, xb_ref, xpb_ref, xnb_ref, dtb_ref, cw_ref, cb_ref, alog_ref,
                dskip_ref, yf_ref, yb_ref, stf_sc, stb_sc, *, nl):
    s = pl.program_id(1)

    @pl.when(s == 0)
    def _():
        stf_sc[...] = jnp.zeros(stf_sc.shape, F32)
        stb_sc[...] = jnp.zeros(stb_sc.shape, F32)

    _ssd_tile(_scan_tile(s, nl, False), xf_ref, xpf_ref, xnf_ref, dtf_ref, cw_ref, cb_ref, alog_ref, dskip_ref,
              yf_ref, stf_sc, nl=nl, reverse=False)
    _ssd_tile(_scan_tile(s, nl, True), xb_ref, xpb_ref, xnb_ref, dtb_ref, cw_ref, cb_ref, alog_ref, dskip_ref,
              yb_ref, stb_sc, nl=nl, reverse=True)


def _ssd_call(xbc, dt, cw, cb, alog, dskip, nl):
    b, l, _ = xbc.shape
    nt = l // TILE
    rows8 = TILE // 8
    const = lambda shape: pl.BlockSpec(shape, lambda bi, s: (0,) * len(shape))

    def stream(reverse):
        tile_of = lambda s: _scan_tile(s, nl, reverse)
        return [pl.BlockSpec((1, TILE, SSD_CONV_DIM), lambda bi, s: (bi, tile_of(s), 0)),
                pl.BlockSpec((1, 8, SSD_CONV_DIM), lambda bi, s: (bi, jnp.maximum(tile_of(s) * rows8 - 1, 0), 0)),
                pl.BlockSpec((1, 8, SSD_CONV_DIM),
                             lambda bi, s: (bi, jnp.minimum((tile_of(s) + 1) * rows8, nt * rows8 - 1), 0)),
                pl.BlockSpec((1, TILE, DT_PAD), lambda bi, s: (bi, tile_of(s), 0))]

    out = lambda reverse: pl.BlockSpec((1, TILE, SSD_D_INNER), lambda bi, s: (bi, _scan_tile(s, nl, reverse), 0))
    state = pltpu.VMEM((SSD_GROUPS, SSD_STATE, SSD_HEADS_PER_GROUP * SSD_HEAD_DIM), F32)
    y = jax.ShapeDtypeStruct((b, l, SSD_D_INNER), F32)
    return pl.pallas_call(
        functools.partial(_ssd_kernel, nl=nl),
        out_shape=(y, y),
        grid=(b, nt),
        in_specs=stream(False) + stream(True) + [const(cw.shape), const(cb.shape), const(alog.shape),
                                                 const(dskip.shape)],
        out_specs=(out(False), out(True)),
        scratch_shapes=[state, state],
        compiler_params=pltpu.CompilerParams(dimension_semantics=("arbitrary", "arbitrary")),
        name="ssd",
    )(xbc, xbc, xbc, dt, xbc, xbc, xbc, dt, cw, cb, alog, dskip)


def _ret_tile(q_ref, k_ref, v_ref, ld_ref, y_ref, st_sc, *, reverse):
    n = RET_SCAN_CHUNK
    tri = _tri(reverse, n)
    r = lax.broadcasted_iota(jnp.int32, (n, n), 0)
    c = lax.broadcasted_iota(jnp.int32, (n, n), 1)
    dist = jnp.maximum((c - r) if reverse else (r - c), 0).astype(F32)
    rk = lax.broadcasted_iota(jnp.int32, (n, RET_DK), 0)
    pos = ((n - 1 - rk) if reverse else rk).astype(F32)
    col0 = RET_HEADS if reverse else 0

    for hd in range(RET_HEADS):
        c0 = hd * RET_DK
        log_g = -jnp.exp(ld_ref[0:1, col0 + hd:col0 + hd + 1])
        q = q_ref[0, :, c0:c0 + RET_DK]
        k = k_ref[0, :, c0:c0 + RET_DK]
        v = v_ref[0, :, c0:c0 + RET_DV]
        st_prev = st_sc[hd]
        sc = _dot_nt(q, k) * jnp.where(tri, jnp.exp(dist * jnp.broadcast_to(log_g, (n, n))), 0.0)
        lg = jnp.broadcast_to(log_g, (n, RET_DK))
        q_dec = (q.astype(F32) * jnp.exp((pos + 1.0) * lg)).astype(BF16)
        y_ref[0, :, c0:c0 + RET_DV] = _dot(jnp.concatenate([sc.astype(BF16), q_dec], axis=1),
                                           jnp.concatenate([v, st_prev.astype(BF16)], axis=0))
        k_dec = (k.astype(F32) * jnp.exp((n - 1.0 - pos) * lg)).astype(BF16)
        st_sc[hd] = jnp.exp(n * jnp.broadcast_to(log_g, (RET_DK, RET_DV))) * st_prev + _dot_tn(k_dec, v)


def _ret_kernel(qf_ref, kf_ref, vf_ref, qb_ref, kb_ref, vb_ref, ld_ref, yf_ref, yb_ref, stf_sc, stb_sc):
    @pl.when(pl.program_id(1) == 0)
    def _():
        stf_sc[...] = jnp.zeros(stf_sc.shape, F32)
        stb_sc[...] = jnp.zeros(stb_sc.shape, F32)

    _ret_tile(qf_ref, kf_ref, vf_ref, ld_ref, yf_ref, stf_sc, reverse=False)
    _ret_tile(qb_ref, kb_ref, vb_ref, ld_ref, yb_ref, stb_sc, reverse=True)


def _ret_call(rq, rk, rv, ld, nl):
    b, l, _ = rq.shape
    nt = l // TILE
    tok = lambda reverse: pl.BlockSpec((1, TILE, RET_DIM), lambda bi, s: (bi, _scan_tile(s, nl, reverse), 0))
    state = pltpu.VMEM((RET_HEADS, RET_DK, RET_DV), F32)
    y = jax.ShapeDtypeStruct((b, l, RET_DIM), F32)
    return pl.pallas_call(
        _ret_kernel,
        out_shape=(y, y),
        grid=(b, nt),
        in_specs=[tok(False)] * 3 + [tok(True)] * 3 + [pl.BlockSpec(ld.shape, lambda bi, s: (0, 0))],
        out_specs=(tok(False), tok(True)),
        scratch_shapes=[state, state],
        compiler_params=pltpu.CompilerParams(dimension_semantics=("arbitrary", "arbitrary")),
        name="ret",
    )(rq, rk, rv, rq, rk, rv, ld)


def _merge_kernel(h_ref, mod_ref, n1w_ref, ot_ref, sf_ref, sb_ref, rf_ref, rb_ref, wzg_ref, wgate_ref,
                  snw_ref, gnw_ref, wb_ref, wout_ref, o_ref):
    h = h_ref[0]
    u = _rms_rows(h, n1w_ref[...]) * (1.0 + mod_ref[0, 1:2, :]) + mod_ref[0, 0:1, :]
    ub = u.astype(BF16)
    zg = _dot(ub, wzg_ref[...])
    gates = _sigmoid(_dot(ub, wgate_ref[...]))

    br_attn = _dot_tn(ot_ref[0], wb_ref[0])

    y = (sf_ref[0] + sb_ref[0]) * _silu(zg[:, :SSD_D_INNER])
    br_ssd = _dot(_rms_rows(y, snw_ref[...]).astype(BF16), wb_ref[1])

    yr = rf_ref[0] + rb_ref[0]
    heads = []
    for hd in range(RET_HEADS):
        yh = yr[:, hd * RET_DV:(hd + 1) * RET_DV]
        yc = yh - jnp.mean(yh, axis=-1, keepdims=True)
        heads.append(yc * lax.rsqrt(jnp.mean(yc * yc, axis=-1, keepdims=True) + NORM_EPS))
    yn = jnp.concatenate(heads, axis=1) * gnw_ref[...] * _silu(zg[:, SSD_D_INNER:])
    br_ret = _dot(yn.astype(BF16), wb_ref[2])

    merged = (gates[:, :D_MODEL] * br_attn + gates[:, D_MODEL:2 * D_MODEL] * br_ssd
              + gates[:, 2 * D_MODEL:] * br_ret)
    o_ref[0] = h + mod_ref[0, 2:3, :] * _dot(merged.astype(BF16), wout_ref[...])


def _merge_call(h, mods, n1w, ot, sf, sb, rf, rb, wts, nl):
    b, l, _ = h.shape
    nt = l // TILE
    wzg, wgate, snw, gnw, wb, wout = wts
    const = lambda shape: pl.BlockSpec(shape, lambda bi, i: (0,) * len(shape))
    tok = lambda c: pl.BlockSpec((1, TILE, c), lambda bi, i: (bi, i, 0))
    weights_bytes = 2 * (wzg.size + wgate.size + wb.size + wout.size)
    return pl.pallas_call(
        _merge_kernel,
        out_shape=jax.ShapeDtypeStruct((b, l, D_MODEL), F32),
        grid=(b, nt),
        in_specs=[tok(D_MODEL),
                  pl.BlockSpec((1, 6, D_MODEL), lambda bi, i: (jnp.where(i == nl, 4, bi), 0, 0)),
                  const((1, D_MODEL)),
                  pl.BlockSpec((1, ATTN_Q_DIM, TILE), lambda bi, i: (bi, 0, i)),
                  tok(SSD_D_INNER), tok(SSD_D_INNER), tok(RET_DIM), tok(RET_DIM),
                  const(wzg.shape), const(wgate.shape), const(snw.shape), const(gnw.shape),
                  const(wb.shape), const(wout.shape)],
        out_specs=tok(D_MODEL),
        compiler_params=pltpu.CompilerParams(dimension_semantics=("arbitrary", "arbitrary"),
                                             vmem_limit_bytes=_vmem_limit(weights_bytes)),
        name="merge",
    )(h, mods, n1w, ot, sf, sb, rf, rb, wzg, wgate, snw, gnw, wb, wout)


def _mlp_kernel(h_ref, mod_ref, n2w_ref, w1_ref, w2_ref, fw_ref, o_ref, *, final):
    h = h_ref[0]
    v = _rms_rows(h, n2w_ref[...]) * (1.0 + mod_ref[0, 4:5, :]) + mod_ref[0, 3:4, :]
    a = jnp.maximum(_dot(v.astype(BF16), w1_ref[...]), 0.0)
    out = h + mod_ref[0, 5:6, :] * _dot((a * a).astype(BF16), w2_ref[...])
    o_ref[0] = _rms_rows(out, fw_ref[...]) if final else out


def _mlp_call(h, mods, n2w, w1, w2, fw, nl, final):
    b, l, _ = h.shape
    nt = nl if final else l // TILE
    const = lambda shape: pl.BlockSpec(shape, lambda bi, i: (0,) * len(shape))
    tok = pl.BlockSpec((1, TILE, D_MODEL), lambda bi, i: (bi, i, 0))
    return pl.pallas_call(
        functools.partial(_mlp_kernel, final=final),
        out_shape=jax.ShapeDtypeStruct((b, nt * TILE, D_MODEL), F32),
        grid=(b, nt),
        in_specs=[tok,
                  pl.BlockSpec((1, 6, D_MODEL), lambda bi, i: (jnp.where(i == nl, 4, bi), 0, 0)),
                  const((1, D_MODEL)), const(w1.shape), const(w2.shape), const((1, D_MODEL))],
        out_specs=tok,
        compiler_params=pltpu.CompilerParams(dimension_semantics=("arbitrary", "arbitrary"),
                                             vmem_limit_bytes=_vmem_limit(2 * (w1.size + w2.size))),
        name="mlp_final" if final else "mlp",
    )(h, mods, n2w, w1, w2, fw)


def _rope_tables(n, m):
    rows = n // GRID_W
    row = jnp.repeat(jnp.arange(rows, dtype=F32), GRID_W)
    col = jnp.tile(jnp.arange(GRID_W, dtype=F32), rows)
    inv = ROPE_THETA ** (-jnp.arange(ATTN_AXIS_FREQS, dtype=F32) / ATTN_AXIS_FREQS)
    ang = jnp.concatenate([row[:, None] * inv, col[:, None] * inv], axis=-1)
    cos_a = jnp.concatenate([jnp.cos(ang), jnp.ones((m, ATTN_HALF), F32)], axis=0).T
    sin_a = jnp.concatenate([jnp.sin(ang), jnp.zeros((m, ATTN_HALF), F32)], axis=0).T
    pos = jnp.concatenate([jnp.arange(n, dtype=F32) + m, jnp.arange(m, dtype=F32)])
    inv_r = ROPE_THETA ** (-jnp.linspace(0.0, 1.0, RET_DK // 2, dtype=F32))
    ang_r = pos[:, None] * inv_r
    cos_r = jnp.concatenate([jnp.cos(ang_r), jnp.cos(ang_r)], axis=-1)
    sin_r = jnp.concatenate([-jnp.sin(ang_r), jnp.sin(ang_r)], axis=-1)
    return cos_a, sin_a, cos_r, sin_r


def _layer_weights(w_in, q_norm, k_norm, dt_bias):
    offs = [0]
    for sz in IN_SPLITS:
        offs.append(offs[-1] + sz)
    col = lambda j: w_in[:, offs[j]:offs[j + 1]]
    wqkv = jnp.concatenate([col(0), col(1), col(2)], axis=1).T.astype(BF16)
    qkw = jnp.concatenate([jnp.tile(q_norm, ATTN_HEADS), jnp.tile(k_norm, ATTN_KV_HEADS)])
    qkw = jnp.broadcast_to(qkw[:, None], (ATTN_Q_DIM + ATTN_KV_DIM, TILE)).astype(F32)
    wxbc = col(4).astype(BF16)
    wdt = jnp.pad(col(5), ((0, 0), (0, DT_PAD - 2 * SSD_HEADS))).astype(BF16)
    dtb = jnp.pad(dt_bias.reshape(1, -1), ((0, 0), (0, DT_PAD - 2 * SSD_HEADS))).astype(F32)
    wr = jnp.concatenate([col(6), col(7), col(8)], axis=1).astype(BF16)
    wzg = jnp.concatenate([col(3), col(9)], axis=1).astype(BF16)
    wgate = col(10).astype(BF16)
    return (wqkv, qkw, wxbc, wdt, dtb, wr), (wzg, wgate)


def _pad_lanes(v):
    v = v.reshape(1, -1).astype(F32)
    return jnp.pad(v, ((0, 0), (0, V7X_LANES - v.shape[1])))


def kernel(x, c, ctx, c_ctx, w_mod, b_mod, norm1_w, norm2_w, w_in, attn_q_norm, attn_k_norm, ssd_conv_w,
           ssd_conv_b, ssd_dt_bias, ssd_a_log, ssd_d, ssd_norm_w, ret_log_decay, ret_gn_w, w_branch, w_out,
           w_mlp1, w_mlp2, final_norm_w):
    b, n, d = x.shape
    m = ctx.shape[1]
    depth = w_in.shape[0]
    assert d == D_MODEL and m == TILE and n % TILE == 0 and n % GRID_W == 0 and b <= 4
    nl = n // TILE

    tabs = _rope_tables(n, m)
    cc = jnp.zeros((8, D_MODEL), F32).at[:b].set(c).at[4].set(c_ctx)
    h = jnp.concatenate([x, ctx], axis=1)

    for layer in range(depth):
        final = layer == depth - 1
        in_w, (wzg, wgate) = _layer_weights(w_in[layer], attn_q_norm[layer], attn_k_norm[layer],
                                            ssd_dt_bias[layer])
        n1w = norm1_w[layer].reshape(1, -1)
        mods = _mod_call(cc, w_mod[layer], b_mod[layer])

        qt, k, vt, xbc, dt, rq, rk, rv = _inproj_call(h, mods, n1w, in_w, tabs, nl)
        ot = _attn_call(qt, k, vt, nl)

        cw = jnp.pad(ssd_conv_w[layer], ((0, 8 - SSD_CONV_K), (0, 0)))
        cb = ssd_conv_b[layer].reshape(1, -1)
        alog = _pad_lanes(ssd_a_log[layer])
        dskip = jnp.repeat(ssd_d[layer], SSD_HEAD_DIM).reshape(1, -1)
        sf, sb = _ssd_call(xbc, dt, cw, cb, alog, dskip, nl)

        ld = _pad_lanes(ret_log_decay[layer])
        rf, rb = _ret_call(rq, rk, rv, ld, nl)

        merge_w = (wzg, wgate, ssd_norm_w[layer].reshape(1, -1), ret_gn_w[layer].reshape(1, -1),
                   w_branch[layer].astype(BF16), w_out[layer].astype(BF16))
        h = _merge_call(h, mods, n1w, ot, sf, sb, rf, rb, merge_w, nl)
        h = _mlp_call(h, mods, norm2_w[layer].reshape(1, -1), w_mlp1[layer].astype(BF16),
                      w_mlp2[layer].astype(BF16), final_norm_w.reshape(1, -1), nl, final)
    return h
```

```python
import functools
import math

import jax
import jax.numpy as jnp
from jax import lax
from jax.experimental import pallas as pl
from jax.experimental.pallas import tpu as pltpu

F32 = jnp.float32
BF16 = jnp.bfloat16

D_MODEL = 1024
GRID_W = 64
NORM_EPS = 1e-6
ROPE_THETA = 10000.0

ATTN_HEADS = 8
ATTN_KV_HEADS = 2
ATTN_GROUP = ATTN_HEADS // ATTN_KV_HEADS
ATTN_HEAD_DIM = 64
ATTN_HALF = ATTN_HEAD_DIM // 2
ATTN_AXIS_FREQS = ATTN_HEAD_DIM // 4
ATTN_Q_DIM = ATTN_HEADS * ATTN_HEAD_DIM
ATTN_KV_DIM = ATTN_KV_HEADS * ATTN_HEAD_DIM

SSD_HEADS = 8
SSD_HEAD_DIM = 64
SSD_D_INNER = SSD_HEADS * SSD_HEAD_DIM
SSD_GROUPS = 2
SSD_STATE = 128
SSD_CONV_K = 3
SSD_CONV_DIM = SSD_D_INNER + 2 * SSD_GROUPS * SSD_STATE
SSD_HEADS_PER_GROUP = SSD_HEADS // SSD_GROUPS

RET_HEADS = 4
RET_DK = 128
RET_DV = 128
RET_DIM = RET_HEADS * RET_DK

N_BRANCH = 3
BRANCH_W = 512
MLP_HIDDEN = 4 * D_MODEL

IN_SPLITS = (ATTN_Q_DIM, ATTN_KV_DIM, ATTN_KV_DIM, SSD_D_INNER, SSD_CONV_DIM, 2 * SSD_HEADS,
             RET_DIM, RET_DIM, RET_HEADS * RET_DV, RET_HEADS * RET_DV, N_BRANCH * D_MODEL)

V7X_LANES = 128
V7X_VMEM_BYTES = 64 * 1024 * 1024

TILE = 256
CHUNK = 128
RET_SCAN_CHUNK = TILE
TOK_TILE = 512
DT_PAD = V7X_LANES
NEG_BIG = -1e30
ATTN_KEY_CHUNK = 768
ATTN_QK_TILE = 256
ATTN_PV_TILE = 256
ATTN_ONES_ROWS = 16


def _resident(shape):
    return pl.BlockSpec(shape, lambda bi, i: (0,) * len(shape), pipeline_mode=pl.Buffered(1))


def _tok_tiles(l, nl):
    assert (nl * TILE) % TOK_TILE == 0
    return pl.cdiv(l, TOK_TILE), nl * TILE // TOK_TILE


def _vmem_limit(resident_bytes):
    return int(min(V7X_VMEM_BYTES - 8 * 1024 * 1024, 2 * resident_bytes + 16 * 1024 * 1024))


def _sigmoid(x):
    return 0.5 * jnp.tanh(0.5 * x) + 0.5


def _silu(x):
    return x * _sigmoid(x)


def _rms_rows(x, w):
    return x * lax.rsqrt(jnp.mean(x * x, axis=-1, keepdims=True) + NORM_EPS) * w


def _dot(a, b):
    return jnp.dot(a, b, preferred_element_type=F32)


def _dot_nt(a, b):
    return lax.dot_general(a, b, (((1,), (1,)), ((), ())), preferred_element_type=F32)


def _dot_tn(a, b):
    return lax.dot_general(a, b, (((0,), (0,)), ((), ())), preferred_element_type=F32)


def _split3(x):
    hi = x.astype(BF16)
    r1 = x - hi.astype(F32)
    mid = r1.astype(BF16)
    lo = (r1 - mid.astype(F32)).astype(BF16)
    return hi, mid, lo


def _mod_kernel(c_ref, w_ref, b_ref, o_ref):
    s = _silu(c_ref[...])
    o_ref[...] = jnp.dot(s, w_ref[...], preferred_element_type=F32,
                         precision=lax.Precision.HIGHEST) + b_ref[...]


def _mod_call(cc, w_mod, b_mod):
    nblk = w_mod.shape[1] // D_MODEL
    out = pl.pallas_call(
        _mod_kernel,
        out_shape=jax.ShapeDtypeStruct((8, nblk * D_MODEL), F32),
        grid=(nblk,),
        in_specs=[pl.BlockSpec((8, D_MODEL), lambda j: (0, 0)),
                  pl.BlockSpec((D_MODEL, D_MODEL), lambda j: (0, j)),
                  pl.BlockSpec((1, D_MODEL), lambda j: (0, j))],
        out_specs=pl.BlockSpec((8, D_MODEL), lambda j: (0, j)),
        compiler_params=pltpu.CompilerParams(dimension_semantics=("arbitrary",)),
        name="mod",
    )(cc, w_mod, b_mod.reshape(1, -1))
    return out.reshape(8, nblk, D_MODEL)


def _inproj_kernel(h_ref, mod_ref, n1w_ref, wqkv_ref, qkw_ref, cos_ref, sin_ref, wxbc_ref, wdt_ref,
                   dtb_ref, wr_ref, cosr_ref, sinr_ref,
                   qt_ref, k_ref, vt_ref, xbc_ref, dt_ref, rq_ref, rk_ref, rv_ref):
    h = h_ref[0]
    u = _rms_rows(h, n1w_ref[...]) * (1.0 + mod_ref[0, 1:2, :]) + mod_ref[0, 0:1, :]
    ub = u.astype(BF16)

    qkv_t = _dot_nt(wqkv_ref[...], ub)
    cos = cos_ref[...]
    sin = sin_ref[...]
    k_rows = []
    for hd in range(ATTN_HEADS + ATTN_KV_HEADS):
        r0 = hd * ATTN_HEAD_DIM
        xh = qkv_t[r0:r0 + ATTN_HEAD_DIM]
        yh = xh * lax.rsqrt(jnp.mean(xh * xh, axis=0, keepdims=True) + NORM_EPS) * qkw_ref[r0:r0 + ATTN_HEAD_DIM, :]
        y1 = yh[:ATTN_HALF]
        y2 = yh[ATTN_HALF:]
        o1 = y1 * cos - y2 * sin
        o2 = y1 * sin + y2 * cos
        if hd < ATTN_HEADS:
            scale = ATTN_HEAD_DIM ** -0.5 * math.log2(math.e)
            qt_ref[0, r0:r0 + ATTN_HALF, :] = (o1 * scale).astype(BF16)
            qt_ref[0, r0 + ATTN_HALF:r0 + ATTN_HEAD_DIM, :] = (o2 * scale).astype(BF16)
        else:
            k_rows += [o1, o2]
    k_t = jnp.concatenate(k_rows, axis=0)
    k_ref[0] = jnp.transpose(k_t).astype(BF16)
    vt_ref[0] = qkv_t[ATTN_Q_DIM + ATTN_KV_DIM:].astype(BF16)

    xbc_ref[0] = _dot(ub, wxbc_ref[...])
    dt_raw = _dot(ub, wdt_ref[...]) + dtb_ref[...]
    dt_ref[0] = jnp.maximum(dt_raw, 0.0) + jnp.log1p(jnp.exp(-jnp.abs(dt_raw)))

    r = _dot(ub, wr_ref[...])
    cosr = cosr_ref[...]
    sinr = sinr_ref[...]
    for hd in range(RET_HEADS):
        c0 = hd * RET_DK
        qh = r[:, c0:c0 + RET_DK]
        kh = r[:, RET_DIM + c0:RET_DIM + c0 + RET_DK]
        rq_ref[0, :, c0:c0 + RET_DK] = (qh * cosr + pltpu.roll(qh, RET_DK // 2, 1) * sinr).astype(BF16)
        rk_ref[0, :, c0:c0 + RET_DK] = ((kh * cosr + pltpu.roll(kh, RET_DK // 2, 1) * sinr)
                                        * (RET_DK ** -0.5)).astype(BF16)
    rv_ref[0] = r[:, 2 * RET_DIM:].astype(BF16)


def _inproj_call(h, mods, n1w, wts, tabs, nl):
    b, l, _ = h.shape
    nt, ctx_i = _tok_tiles(l, nl)
    wqkv, qkw, wxbc, wdt, dtb, wr = wts
    cos_a, sin_a, cos_r, sin_r = tabs
    const = _resident
    tok = lambda c: pl.BlockSpec((1, TOK_TILE, c), lambda bi, i: (bi, i, 0))
    tok_t = lambda r: pl.BlockSpec((1, r, TOK_TILE), lambda bi, i: (bi, 0, i))
    weights_bytes = 2 * (wqkv.size + wxbc.size + wdt.size + wr.size) + 4 * qkw.size
    return pl.pallas_call(
        _inproj_kernel,
        out_shape=(jax.ShapeDtypeStruct((b, ATTN_Q_DIM, l), BF16),
                   jax.ShapeDtypeStruct((b, l, ATTN_KV_DIM), BF16),
                   jax.ShapeDtypeStruct((b, ATTN_KV_DIM, l), BF16),
                   jax.ShapeDtypeStruct((b, l, SSD_CONV_DIM), F32),
                   jax.ShapeDtypeStruct((b, l, DT_PAD), F32),
                   jax.ShapeDtypeStruct((b, l, RET_DIM), BF16),
                   jax.ShapeDtypeStruct((b, l, RET_DIM), BF16),
                   jax.ShapeDtypeStruct((b, l, RET_DIM), BF16)),
        grid=(b, nt),
        in_specs=[tok(D_MODEL),
                  pl.BlockSpec((1, 6, D_MODEL), lambda bi, i: (jnp.where(i == ctx_i, 4, bi), 0, 0)),
                  const((1, D_MODEL)),
                  const(wqkv.shape), const(qkw.shape),
                  pl.BlockSpec((ATTN_HALF, TOK_TILE), lambda bi, i: (0, i)),
                  pl.BlockSpec((ATTN_HALF, TOK_TILE), lambda bi, i: (0, i)),
                  const(wxbc.shape), const(wdt.shape), const(dtb.shape), const(wr.shape),
                  pl.BlockSpec((TOK_TILE, RET_DK), lambda bi, i: (i, 0)),
                  pl.BlockSpec((TOK_TILE, RET_DK), lambda bi, i: (i, 0))],
        out_specs=(tok_t(ATTN_Q_DIM), tok(ATTN_KV_DIM), tok_t(ATTN_KV_DIM), tok(SSD_CONV_DIM), tok(DT_PAD),
                   tok(RET_DIM), tok(RET_DIM), tok(RET_DIM)),
        compiler_params=pltpu.CompilerParams(dimension_semantics=("arbitrary", "arbitrary"),
                                             vmem_limit_bytes=_vmem_limit(weights_bytes)),
        name="inproj",
    )(h, mods, n1w, wqkv, qkw, cos_a, sin_a, wxbc, wdt, dtb, wr, cos_r, sin_r)


def _attn_kernel(qt_ref, k_ref, vt_ref, o_ref, qpad_sc, m_sc, acc_sc, s0_sc, s1_sc, mx0_sc, mx1_sc, *, nl, kc):
    kv = pl.program_id(1)
    i = pl.program_id(2)

    row = lax.broadcasted_iota(jnp.int32, (2 * ATTN_HEAD_DIM, TILE), 0)
    keep = (row >= ATTN_HEAD_DIM).astype(jnp.int32) == kv
    for g in range(ATTN_GROUP):
        qg = qt_ref[0, g * ATTN_HEAD_DIM:(g + 1) * ATTN_HEAD_DIM, :]
        qq = jnp.concatenate([qg, qg], axis=0)
        qpad_sc[:, g * TILE:(g + 1) * TILE] = jnp.where(keep, qq, jnp.zeros_like(qq))
    m_sc[...] = jnp.full(m_sc.shape, NEG_BIG, F32)
    acc_sc[...] = jnp.zeros(acc_sc.shape, F32)

    wide = ATTN_GROUP * TILE

    def scores(off, size, c0, width):
        return _dot(k_ref[0, pl.ds(off, size), :], qpad_sc[:, c0:c0 + width])

    def accumulate(off, size, c0, s, mx):
        cols = slice(c0, c0 + ATTN_PV_TILE)
        m_old = m_sc[:, cols]
        m_new = jnp.maximum(m_old, mx)
        m_sc[:, cols] = m_new
        p = jnp.exp2(s - m_new).astype(BF16)
        v_aug = jnp.concatenate([vt_ref[0, :, pl.ds(off, size)], jnp.ones((ATTN_ONES_ROWS, size), BF16)], axis=0)
        acc_sc[:, cols] = jnp.exp2(m_old - m_new) * acc_sc[:, cols] + _dot(v_aug, p)

    def qk(off, size, c0, s_sc, mx_sc):
        s = scores(off, size, c0, ATTN_QK_TILE)
        s_sc[:size, c0:c0 + ATTN_QK_TILE] = s
        mx_sc[:, c0:c0 + ATTN_QK_TILE] = jnp.max(s, axis=0, keepdims=True)

    def pv(off, size, c0, s_sc, mx_sc):
        accumulate(off, size, c0, s_sc[:size, c0:c0 + ATTN_PV_TILE], mx_sc[:, c0:c0 + ATTN_PV_TILE])

    bufs = ((s0_sc, mx0_sc), (s1_sc, mx1_sc))

    def step(c, off_qk, off_pv):
        for q0 in range(0, wide, ATTN_QK_TILE):
            if c < len(sizes):
                qk(off_qk, sizes[c], q0, *bufs[c % 2])
            if c >= 1:
                for c0 in range(q0, q0 + ATTN_QK_TILE, ATTN_PV_TILE):
                    pv(off_pv, sizes[c - 1], c0, *bufs[(c - 1) % 2])

    n_mid, rest = divmod(nl * TILE, kc)
    sizes = [TILE] + [kc] * n_mid + ([rest] if rest else [])
    offs = [sum(sizes[:c]) for c in range(len(sizes) + 1)]
    steady = [c for c in range(1, len(sizes)) if sizes[c] == kc and sizes[c - 1] == kc]
    pairs = len(steady) // 2
    looped = steady[:2 * pairs]

    @pl.when(i != nl)
    def _():
        for c in range(len(sizes) + 1):
            if c in looped:
                if c == looped[0]:
                    def body(j, carry):
                        for d in range(2):
                            off = pl.multiple_of(offs[looped[0] + d] + 2 * j * kc, TILE)
                            step(looped[0] + d, off, off - kc)
                        return carry
                    lax.fori_loop(0, pairs, body, 0)
            else:
                step(c, offs[c], offs[c - 1] if c >= 1 else None)

    @pl.when(i == nl)
    def _():
        for c0 in range(0, wide, ATTN_PV_TILE):
            s = scores(nl * TILE, TILE, c0, ATTN_PV_TILE)
            accumulate(nl * TILE, TILE, c0, s, jnp.max(s, axis=0, keepdims=True))

    acc = acc_sc[...]
    out = acc[:ATTN_HEAD_DIM] / acc[ATTN_HEAD_DIM:ATTN_HEAD_DIM + 1]
    for g in range(ATTN_GROUP):
        o_ref[0, g * ATTN_HEAD_DIM:(g + 1) * ATTN_HEAD_DIM, :] = out[:, g * TILE:(g + 1) * TILE].astype(BF16)


def _attn_call(qt, k, vt, nl):
    b, _, l = qt.shape
    nt = l // TILE
    kc = ATTN_KEY_CHUNK
    gq = ATTN_GROUP * ATTN_HEAD_DIM
    wide = ATTN_GROUP * TILE
    resident = 2 * (l * ATTN_KV_DIM + ATTN_HEAD_DIM * l) + 4 * kc * wide
    return pl.pallas_call(
        functools.partial(_attn_kernel, nl=nl, kc=kc),
        out_shape=jax.ShapeDtypeStruct((b, ATTN_Q_DIM, l), BF16),
        grid=(b, ATTN_KV_HEADS, nt),
        in_specs=[pl.BlockSpec((1, gq, TILE), lambda bi, kv, i: (bi, kv, i)),
                  pl.BlockSpec((1, l, ATTN_KV_DIM), lambda bi, kv, i: (bi, 0, 0)),
                  pl.BlockSpec((1, ATTN_HEAD_DIM, l), lambda bi, kv, i: (bi, kv, 0))],
        out_specs=pl.BlockSpec((1, gq, TILE), lambda bi, kv, i: (bi, kv, i)),
        scratch_shapes=[pltpu.VMEM((2 * ATTN_HEAD_DIM, wide), BF16),
                        pltpu.VMEM((1, wide), F32),
                        pltpu.VMEM((ATTN_HEAD_DIM + ATTN_ONES_ROWS, wide), F32),
                        pltpu.VMEM((kc, wide), F32), pltpu.VMEM((kc, wide), F32),
                        pltpu.VMEM((1, wide), F32), pltpu.VMEM((1, wide), F32)],
        compiler_params=pltpu.CompilerParams(dimension_semantics=("arbitrary",) * 3,
                                             vmem_limit_bytes=_vmem_limit(resident)),
        name="attn",
    )(qt, k, vt)


def _scan_tile(s, nl, reverse):
    return jnp.where(s == 0, nl, nl - s) if reverse else jnp.where(s == 0, nl, s - 1)


def _tri(reverse, n=CHUNK):
    r = lax.broadcasted_iota(jnp.int32, (n, n), 0)
    c = lax.broadcasted_iota(jnp.int32, (n, n), 1)
    return (c >= r) if reverse else (c <= r)


def _lane_bcast(x, c):
    return jnp.broadcast_to(x[:, c:c + 1], (x.shape[0], V7X_LANES))


def _ssd_tile(t, x_ref, xp_ref, xn_ref, dt_ref, cw_ref, cb_ref, alog_ref, dskip_ref, y_ref, st_sc,
              *, nl, reverse):
    x = x_ref[0]
    row = lax.broadcasted_iota(jnp.int32, (8, SSD_CONV_DIM), 0)
    has_prev = jnp.logical_and(t != nl, t != 0)
    has_next = t < nl - 1
    prev_row = jnp.where(has_prev, xp_ref[0, 7:8, :], 0.0)
    next_row = jnp.where(has_next, xn_ref[0, 0:1, :], 0.0)
    x_m1 = pltpu.roll(x, 1, 0)
    x_m1 = jnp.concatenate([jnp.where(row == 0, prev_row, x_m1[:8]), x_m1[8:]], axis=0)
    x_p1 = pltpu.roll(x, TILE - 1, 0)
    x_p1 = jnp.concatenate([x_p1[:TILE - 8], jnp.where(row == 7, next_row, x_p1[TILE - 8:])], axis=0)
    xs = _silu(cw_ref[0:1, :] * x_m1 + cw_ref[1:2, :] * x + cw_ref[2:3, :] * x_p1 + cb_ref[...])

    tri = _tri(reverse)
    tri_b = tri.astype(BF16)
    lane = lax.broadcasted_iota(jnp.int32, (CHUNK, V7X_LANES), 1)
    left = lane < SSD_HEAD_DIM
    col0 = SSD_HEADS if reverse else 0
    a_neg = -jnp.exp(alog_ref[...])
    last = 0 if reverse else CHUNK - 1

    chunks = range(TILE // CHUNK)
    for ci in (reversed(chunks) if reverse else chunks):
        r0 = ci * CHUNK
        xc = xs[r0:r0 + CHUNK]
        dtc = dt_ref[0, r0:r0 + CHUNK, :]
        a = dtc * a_neg
        hi, mid, lo = _split3(a)
        a_cum = _dot(tri_b, hi) + _dot(tri_b, mid) + _dot(tri_b, lo)
        a_cum_t = jnp.transpose(a_cum)
        for g in range(SSD_GROUPS):
            bm = xc[:, SSD_D_INNER + g * SSD_STATE:SSD_D_INNER + (g + 1) * SSD_STATE]
            cm = xc[:, SSD_D_INNER + (SSD_GROUPS + g) * SSD_STATE:SSD_D_INNER + (SSD_GROUPS + g + 1) * SSD_STATE]
            bmb = bm.astype(BF16)
            cmb = cm.astype(BF16)
            cb = _dot_nt(cmb, bmb)
            st_prev = st_sc[g]
            y_off = _dot(cmb, st_prev.astype(BF16))
            xdd_pairs = []
            tot = []
            for pr in range(SSD_HEADS_PER_GROUP // 2):
                h0 = g * SSD_HEADS_PER_GROUP + 2 * pr
                lanes0 = h0 * SSD_HEAD_DIM
                x2 = xc[:, lanes0:lanes0 + V7X_LANES]
                acol = [_lane_bcast(a_cum, col0 + h0 + j) for j in range(2)]
                dcol = [_lane_bcast(dtc, col0 + h0 + j) for j in range(2)]
                a2 = jnp.where(left, acol[0], acol[1])
                xd2 = x2 * jnp.where(left, dcol[0], dcol[1])
                xd2b = xd2.astype(BF16)
                y_pair = []
                for j in range(2):
                    arow = a_cum_t[col0 + h0 + j:col0 + h0 + j + 1, :]
                    lmat = jnp.where(tri, jnp.exp(jnp.where(tri, acol[j] - arow, 0.0)), 0.0)
                    y_pair.append(_dot((cb * lmat).astype(BF16), xd2b))
                y2 = jnp.where(left, y_pair[0], y_pair[1])
                y2 = y2 + jnp.exp(a2) * y_off[:, pr * V7X_LANES:(pr + 1) * V7X_LANES]
                if not reverse:
                    y2 = y2 + dskip_ref[:, lanes0:lanes0 + V7X_LANES] * x2
                y_ref[0, r0:r0 + CHUNK, lanes0:lanes0 + V7X_LANES] = y2
                a_tot = a2[last:last + 1, :]
                xdd_pairs.append((xd2 * jnp.exp(a_tot - a2)).astype(BF16))
                tot.append(a_tot)
            xdd = jnp.concatenate(xdd_pairs, axis=1)
            st_sc[g] = jnp.exp(jnp.concatenate(tot, axis=1)) * st_prev + _dot_tn(bmb, xdd)


def _ssd_kernel(xf_ref, xpf_ref, xnf_ref, dtf_ref, xb_ref, xpb_ref, xnb_ref, dtb_ref, cw_ref, cb_ref, alog_ref,
                dskip_ref, yf_ref, yb_ref, stf_sc, stb_sc, *, nl):
    s = pl.program_id(1)

    @pl.when(s == 0)
    def _():
        stf_sc[...] = jnp.zeros(stf_sc.shape, F32)
        stb_sc[...] = jnp.zeros(stb_sc.shape, F32)

    _ssd_tile(_scan_tile(s, nl, False), xf_ref, xpf_ref, xnf_ref, dtf_ref, cw_ref, cb_ref, alog_ref, dskip_ref,
              yf_ref, stf_sc, nl=nl, reverse=False)
    _ssd_tile(_scan_tile(s, nl, True), xb_ref, xpb_ref, xnb_ref, dtb_ref, cw_ref, cb_ref, alog_ref, dskip_ref,
              yb_ref, stb_sc, nl=nl, reverse=True)


def _ssd_call(xbc, dt, cw, cb, alog, dskip, nl):
    b, l, _ = xbc.shape
    nt = l // TILE
    rows8 = TILE // 8
    const = lambda shape: pl.BlockSpec(shape, lambda bi, s: (0,) * len(shape))

    def stream(reverse):
        tile_of = lambda s: _scan_tile(s, nl, reverse)
        return [pl.BlockSpec((1, TILE, SSD_CONV_DIM), lambda bi, s: (bi, tile_of(s), 0)),
                pl.BlockSpec((1, 8, SSD_CONV_DIM), lambda bi, s: (bi, jnp.maximum(tile_of(s) * rows8 - 1, 0), 0)),
                pl.BlockSpec((1, 8, SSD_CONV_DIM),
                             lambda bi, s: (bi, jnp.minimum((tile_of(s) + 1) * rows8, nt * rows8 - 1), 0)),
                pl.BlockSpec((1, TILE, DT_PAD), lambda bi, s: (bi, tile_of(s), 0))]

    out = lambda reverse: pl.BlockSpec((1, TILE, SSD_D_INNER), lambda bi, s: (bi, _scan_tile(s, nl, reverse), 0))
    state = pltpu.VMEM((SSD_GROUPS, SSD_STATE, SSD_HEADS_PER_GROUP * SSD_HEAD_DIM), F32)
    y = jax.ShapeDtypeStruct((b, l, SSD_D_INNER), F32)
    return pl.pallas_call(
        functools.partial(_ssd_kernel, nl=nl),
        out_shape=(y, y),
        grid=(b, nt),
        in_specs=stream(False) + stream(True) + [const(cw.shape), const(cb.shape), const(alog.shape),
                                                 const(dskip.shape)],
        out_specs=(out(False), out(True)),
        scratch_shapes=[state, state],
        compiler_params=pltpu.CompilerParams(dimension_semantics=("arbitrary", "arbitrary")),
        name="ssd",
    )(xbc, xbc, xbc, dt, xbc, xbc, xbc, dt, cw, cb, alog, dskip)


def _ret_tile(q_ref, k_ref, v_ref, ld_ref, y_ref, st_sc, *, reverse):
    n = RET_SCAN_CHUNK
    tri = _tri(reverse, n)
    r = lax.broadcasted_iota(jnp.int32, (n, n), 0)
    c = lax.broadcasted_iota(jnp.int32, (n, n), 1)
    dist = jnp.maximum((c - r) if reverse else (r - c), 0).astype(F32)
    rk = lax.broadcasted_iota(jnp.int32, (n, RET_DK), 0)
    pos = ((n - 1 - rk) if reverse else rk).astype(F32)
    col0 = RET_HEADS if reverse else 0

    for hd in range(RET_HEADS):
        c0 = hd * RET_DK
        log_g = -jnp.exp(ld_ref[0:1, col0 + hd:col0 + hd + 1])
        q = q_ref[0, :, c0:c0 + RET_DK]
        k = k_ref[0, :, c0:c0 + RET_DK]
        v = v_ref[0, :, c0:c0 + RET_DV]
        st_prev = st_sc[hd]
        sc = _dot_nt(q, k) * jnp.where(tri, jnp.exp(dist * jnp.broadcast_to(log_g, (n, n))), 0.0)
        lg = jnp.broadcast_to(log_g, (n, RET_DK))
        q_dec = (q.astype(F32) * jnp.exp((pos + 1.0) * lg)).astype(BF16)
        y_ref[0, :, c0:c0 + RET_DV] = _dot(jnp.concatenate([sc.astype(BF16), q_dec], axis=1),
                                           jnp.concatenate([v, st_prev.astype(BF16)], axis=0))
        k_dec = (k.astype(F32) * jnp.exp((n - 1.0 - pos) * lg)).astype(BF16)
        st_sc[hd] = jnp.exp(n * jnp.broadcast_to(log_g, (RET_DK, RET_DV))) * st_prev + _dot_tn(k_dec, v)


def _ret_kernel(qf_ref, kf_ref, vf_ref, qb_ref, kb_ref, vb_ref, ld_ref, yf_ref, yb_ref, stf_sc, stb_sc):
    @pl.when(pl.program_id(1) == 0)
    def _():
        stf_sc[...] = jnp.zeros(stf_sc.shape, F32)
        stb_sc[...] = jnp.zeros(stb_sc.shape, F32)

    _ret_tile(qf_ref, kf_ref, vf_ref, ld_ref, yf_ref, stf_sc, reverse=False)
    _ret_tile(qb_ref, kb_ref, vb_ref, ld_ref, yb_ref, stb_sc, reverse=True)


def _ret_call(rq, rk, rv, ld, nl):
    b, l, _ = rq.shape
    nt = l // TILE
    tok = lambda reverse: pl.BlockSpec((1, TILE, RET_DIM), lambda bi, s: (bi, _scan_tile(s, nl, reverse), 0))
    state = pltpu.VMEM((RET_HEADS, RET_DK, RET_DV), F32)
    y = jax.ShapeDtypeStruct((b, l, RET_DIM), F32)
    return pl.pallas_call(
        _ret_kernel,
        out_shape=(y, y),
        grid=(b, nt),
        in_specs=[tok(False)] * 3 + [tok(True)] * 3 + [pl.BlockSpec(ld.shape, lambda bi, s: (0, 0))],
        out_specs=(tok(False), tok(True)),
        scratch_shapes=[state, state],
        compiler_params=pltpu.CompilerParams(dimension_semantics=("arbitrary", "arbitrary")),
        name="ret",
    )(rq, rk, rv, rq, rk, rv, ld)


def _merge_kernel(h_ref, mod_ref, n1w_ref, ot_ref, sf_ref, sb_ref, rf_ref, rb_ref, wzg_ref, wgate_ref,
                  snw_ref, gnw_ref, wb_ref, wout_ref, o_ref):
    h = h_ref[0]
    u = _rms_rows(h, n1w_ref[...]) * (1.0 + mod_ref[0, 1:2, :]) + mod_ref[0, 0:1, :]
    ub = u.astype(BF16)
    zg = _dot(ub, wzg_ref[...])
    gates = _sigmoid(_dot(ub, wgate_ref[...]))

    br_attn = _dot_tn(ot_ref[0], wb_ref[0])

    y = (sf_ref[0] + sb_ref[0]) * _silu(zg[:, :SSD_D_INNER])
    br_ssd = _dot(_rms_rows(y, snw_ref[...]).astype(BF16), wb_ref[1])

    yr = rf_ref[0] + rb_ref[0]
    heads = []
    for hd in range(RET_HEADS):
        yh = yr[:, hd * RET_DV:(hd + 1) * RET_DV]
        yc = yh - jnp.mean(yh, axis=-1, keepdims=True)
        heads.append(yc * lax.rsqrt(jnp.mean(yc * yc, axis=-1, keepdims=True) + NORM_EPS))
    yn = jnp.concatenate(heads, axis=1) * gnw_ref[...] * _silu(zg[:, SSD_D_INNER:])
    br_ret = _dot(yn.astype(BF16), wb_ref[2])

    merged = (gates[:, :D_MODEL] * br_attn + gates[:, D_MODEL:2 * D_MODEL] * br_ssd
              + gates[:, 2 * D_MODEL:] * br_ret)
    o_ref[0] = h + mod_ref[0, 2:3, :] * _dot(merged.astype(BF16), wout_ref[...])


def _merge_call(h, mods, n1w, ot, sf, sb, rf, rb, wts, nl):
    b, l, _ = h.shape
    nt, ctx_i = _tok_tiles(l, nl)
    wzg, wgate, snw, gnw, wb, wout = wts
    const = _resident
    tok = lambda c: pl.BlockSpec((1, TOK_TILE, c), lambda bi, i: (bi, i, 0))
    weights_bytes = 2 * (wzg.size + wgate.size + wb.size + wout.size)
    return pl.pallas_call(
        _merge_kernel,
        out_shape=jax.ShapeDtypeStruct((b, l, D_MODEL), F32),
        grid=(b, nt),
        in_specs=[tok(D_MODEL),
                  pl.BlockSpec((1, 6, D_MODEL), lambda bi, i: (jnp.where(i == ctx_i, 4, bi), 0, 0)),
                  const((1, D_MODEL)),
                  pl.BlockSpec((1, ATTN_Q_DIM, TOK_TILE), lambda bi, i: (bi, 0, i)),
                  tok(SSD_D_INNER), tok(SSD_D_INNER), tok(RET_DIM), tok(RET_DIM),
                  const(wzg.shape), const(wgate.shape), const(snw.shape), const(gnw.shape),
                  const(wb.shape), const(wout.shape)],
        out_specs=tok(D_MODEL),
        compiler_params=pltpu.CompilerParams(dimension_semantics=("arbitrary", "arbitrary"),
                                             vmem_limit_bytes=_vmem_limit(weights_bytes)),
        name="merge",
    )(h, mods, n1w, ot, sf, sb, rf, rb, wzg, wgate, snw, gnw, wb, wout)


def _mlp_kernel(h_ref, mod_ref, n2w_ref, w1_ref, w2_ref, fw_ref, o_ref, *, final):
    h = h_ref[0]
    v = _rms_rows(h, n2w_ref[...]) * (1.0 + mod_ref[0, 4:5, :]) + mod_ref[0, 3:4, :]
    a = jnp.maximum(_dot(v.astype(BF16), w1_ref[...]), 0.0)
    out = h + mod_ref[0, 5:6, :] * _dot((a * a).astype(BF16), w2_ref[...])
    o_ref[0] = _rms_rows(out, fw_ref[...]) if final else out


def _mlp_call(h, mods, n2w, w1, w2, fw, nl, final):
    b, l, _ = h.shape
    nt, ctx_i = _tok_tiles(l, nl)
    if final:
        nt, l = ctx_i, nl * TILE
    const = _resident
    tok = pl.BlockSpec((1, TOK_TILE, D_MODEL), lambda bi, i: (bi, i, 0))
    return pl.pallas_call(
        functools.partial(_mlp_kernel, final=final),
        out_shape=jax.ShapeDtypeStruct((b, l, D_MODEL), F32),
        grid=(b, nt),
        in_specs=[tok,
                  pl.BlockSpec((1, 6, D_MODEL), lambda bi, i: (jnp.where(i == ctx_i, 4, bi), 0, 0)),
                  const((1, D_MODEL)), const(w1.shape), const(w2.shape), const((1, D_MODEL))],
        out_specs=tok,
        compiler_params=pltpu.CompilerParams(dimension_semantics=("arbitrary", "arbitrary"),
                                             vmem_limit_bytes=_vmem_limit(2 * (w1.size + w2.size))),
        name="mlp_final" if final else "mlp",
    )(h, mods, n2w, w1, w2, fw)


def _rope_tables(n, m):
    rows = n // GRID_W
    row = jnp.repeat(jnp.arange(rows, dtype=F32), GRID_W)
    col = jnp.tile(jnp.arange(GRID_W, dtype=F32), rows)
    inv = ROPE_THETA ** (-jnp.arange(ATTN_AXIS_FREQS, dtype=F32) / ATTN_AXIS_FREQS)
    ang = jnp.concatenate([row[:, None] * inv, col[:, None] * inv], axis=-1)
    cos_a = jnp.concatenate([jnp.cos(ang), jnp.ones((m, ATTN_HALF), F32)], axis=0).T
    sin_a = jnp.concatenate([jnp.sin(ang), jnp.zeros((m, ATTN_HALF), F32)], axis=0).T
    pos = jnp.concatenate([jnp.arange(n, dtype=F32) + m, jnp.arange(m, dtype=F32)])
    inv_r = ROPE_THETA ** (-jnp.linspace(0.0, 1.0, RET_DK // 2, dtype=F32))
    ang_r = pos[:, None] * inv_r
    cos_r = jnp.concatenate([jnp.cos(ang_r), jnp.cos(ang_r)], axis=-1)
    sin_r = jnp.concatenate([-jnp.sin(ang_r), jnp.sin(ang_r)], axis=-1)
    return cos_a, sin_a, cos_r, sin_r


def _layer_weights(w_in, q_norm, k_norm, dt_bias):
    offs = [0]
    for sz in IN_SPLITS:
        offs.append(offs[-1] + sz)
    col = lambda j: w_in[:, offs[j]:offs[j + 1]]
    wqkv = jnp.concatenate([col(0), col(1), col(2)], axis=1).T.astype(BF16)
    qkw = jnp.concatenate([jnp.tile(q_norm, ATTN_HEADS), jnp.tile(k_norm, ATTN_KV_HEADS)])
    qkw = jnp.broadcast_to(qkw[:, None], (ATTN_Q_DIM + ATTN_KV_DIM, TOK_TILE)).astype(F32)
    wxbc = col(4).astype(BF16)
    wdt = jnp.pad(col(5), ((0, 0), (0, DT_PAD - 2 * SSD_HEADS))).astype(BF16)
    dtb = jnp.pad(dt_bias.reshape(1, -1), ((0, 0), (0, DT_PAD - 2 * SSD_HEADS))).astype(F32)
    wr = jnp.concatenate([col(6), col(7), col(8)], axis=1).astype(BF16)
    wzg = jnp.concatenate([col(3), col(9)], axis=1).astype(BF16)
    wgate = col(10).astype(BF16)
    return (wqkv, qkw, wxbc, wdt, dtb, wr), (wzg, wgate)


def _pad_lanes(v):
    v = v.reshape(1, -1).astype(F32)
    return jnp.pad(v, ((0, 0), (0, V7X_LANES - v.shape[1])))


def kernel(x, c, ctx, c_ctx, w_mod, b_mod, norm1_w, norm2_w, w_in, attn_q_norm, attn_k_norm, ssd_conv_w,
           ssd_conv_b, ssd_dt_bias, ssd_a_log, ssd_d, ssd_norm_w, ret_log_decay, ret_gn_w, w_branch, w_out,
           w_mlp1, w_mlp2, final_norm_w):
    b, n, d = x.shape
    m = ctx.shape[1]
    depth = w_in.shape[0]
    assert d == D_MODEL and m == TILE and n % TILE == 0 and n % GRID_W == 0 and b <= 4
    nl = n // TILE

    tabs = _rope_tables(n, m)
    cc = jnp.zeros((8, D_MODEL), F32).at[:b].set(c).at[4].set(c_ctx)
    h = jnp.concatenate([x, ctx], axis=1)

    for layer in range(depth):
        final = layer == depth - 1
        in_w, (wzg, wgate) = _layer_weights(w_in[layer], attn_q_norm[layer], attn_k_norm[layer],
                                            ssd_dt_bias[layer])
        n1w = norm1_w[layer].reshape(1, -1)
        mods = _mod_call(cc, w_mod[layer], b_mod[layer])

        qt, k, vt, xbc, dt, rq, rk, rv = _inproj_call(h, mods, n1w, in_w, tabs, nl)
        ot = _attn_call(qt, k, vt, nl)

        cw = jnp.pad(ssd_conv_w[layer], ((0, 8 - SSD_CONV_K), (0, 0)))
        cb = ssd_conv_b[layer].reshape(1, -1)
        alog = _pad_lanes(ssd_a_log[layer])
        dskip = jnp.repeat(ssd_d[layer], SSD_HEAD_DIM).reshape(1, -1)
        sf, sb = _ssd_call(xbc, dt, cw, cb, alog, dskip, nl)

        ld = _pad_lanes(ret_log_decay[layer])
        rf, rb = _ret_call(rq, rk, rv, ld, nl)

        merge_w = (wzg, wgate, ssd_norm_w[layer].reshape(1, -1), ret_gn_w[layer].reshape(1, -1),
                   w_branch[layer].astype(BF16), w_out[layer].astype(BF16))
        h = _merge_call(h, mods, n1w, ot, sf, sb, rf, rb, merge_w, nl)
        h = _mlp_call(h, mods, norm2_w[layer].reshape(1, -1), w_mlp1[layer].astype(BF16),
                      w_mlp2[layer].astype(BF16), final_norm_w.reshape(1, -1), nl, final)
    return h
```

```python
import functools
import math

import jax
import jax.numpy as jnp
from jax import lax
from jax.experimental import pallas as pl
from jax.experimental.pallas import tpu as pltpu

F32 = jnp.float32
BF16 = jnp.bfloat16

D_MODEL = 1024
GRID_W = 64
NORM_EPS = 1e-6
ROPE_THETA = 10000.0

ATTN_HEADS = 8
ATTN_KV_HEADS = 2
ATTN_GROUP = ATTN_HEADS // ATTN_KV_HEADS
ATTN_HEAD_DIM = 64
ATTN_HALF = ATTN_HEAD_DIM // 2
ATTN_AXIS_FREQS = ATTN_HEAD_DIM // 4
ATTN_Q_DIM = ATTN_HEADS * ATTN_HEAD_DIM
ATTN_KV_DIM = ATTN_KV_HEADS * ATTN_HEAD_DIM

SSD_HEADS = 8
SSD_HEAD_DIM = 64
SSD_D_INNER = SSD_HEADS * SSD_HEAD_DIM
SSD_GROUPS = 2
SSD_STATE = 128
SSD_CONV_K = 3
SSD_CONV_DIM = SSD_D_INNER + 2 * SSD_GROUPS * SSD_STATE
SSD_HEADS_PER_GROUP = SSD_HEADS // SSD_GROUPS

RET_HEADS = 4
RET_DK = 128
RET_DV = 128
RET_DIM = RET_HEADS * RET_DK

N_BRANCH = 3
BRANCH_W = 512
MLP_HIDDEN = 4 * D_MODEL

IN_SPLITS = (ATTN_Q_DIM, ATTN_KV_DIM, ATTN_KV_DIM, SSD_D_INNER, SSD_CONV_DIM, 2 * SSD_HEADS,
             RET_DIM, RET_DIM, RET_HEADS * RET_DV, RET_HEADS * RET_DV, N_BRANCH * D_MODEL)

V7X_LANES = 128
V7X_VMEM_BYTES = 64 * 1024 * 1024

TILE = 256
CHUNK = 128
RET_SCAN_CHUNK = TILE
TOK_TILE = 512
DT_PAD = V7X_LANES
NEG_BIG = -1e30
ATTN_KEY_CHUNK = 768
ATTN_Q_TILE = 512
ATTN_QK_TILE = 256
ATTN_PV_TILE = 256
ATTN_ONES_ROWS = 16


def _resident(shape):
    return pl.BlockSpec(shape, lambda bi, i: (0,) * len(shape), pipeline_mode=pl.Buffered(1))


def _tok_tiles(l, nl):
    assert (nl * TILE) % TOK_TILE == 0
    return pl.cdiv(l, TOK_TILE), nl * TILE // TOK_TILE


def _vmem_limit(resident_bytes):
    return int(min(V7X_VMEM_BYTES - 8 * 1024 * 1024, 2 * resident_bytes + 16 * 1024 * 1024))


def _sigmoid(x):
    return 0.5 * jnp.tanh(0.5 * x) + 0.5


def _silu(x):
    return x * _sigmoid(x)


def _rms_rows(x, w):
    return x * lax.rsqrt(jnp.mean(x * x, axis=-1, keepdims=True) + NORM_EPS) * w


def _dot(a, b):
    return jnp.dot(a, b, preferred_element_type=F32)


def _dot_nt(a, b):
    return lax.dot_general(a, b, (((1,), (1,)), ((), ())), preferred_element_type=F32)


def _dot_tn(a, b):
    return lax.dot_general(a, b, (((0,), (0,)), ((), ())), preferred_element_type=F32)


def _split3(x):
    hi = x.astype(BF16)
    r1 = x - hi.astype(F32)
    mid = r1.astype(BF16)
    lo = (r1 - mid.astype(F32)).astype(BF16)
    return hi, mid, lo


def _mod_kernel(c_ref, w_ref, b_ref, o_ref):
    s = _silu(c_ref[...])
    o_ref[...] = jnp.dot(s, w_ref[...], preferred_element_type=F32,
                         precision=lax.Precision.HIGHEST) + b_ref[...]


def _mod_call(cc, w_mod, b_mod):
    nblk = w_mod.shape[1] // D_MODEL
    out = pl.pallas_call(
        _mod_kernel,
        out_shape=jax.ShapeDtypeStruct((8, nblk * D_MODEL), F32),
        grid=(nblk,),
        in_specs=[pl.BlockSpec((8, D_MODEL), lambda j: (0, 0)),
                  pl.BlockSpec((D_MODEL, D_MODEL), lambda j: (0, j)),
                  pl.BlockSpec((1, D_MODEL), lambda j: (0, j))],
        out_specs=pl.BlockSpec((8, D_MODEL), lambda j: (0, j)),
        compiler_params=pltpu.CompilerParams(dimension_semantics=("arbitrary",)),
        name="mod",
    )(cc, w_mod, b_mod.reshape(1, -1))
    return out.reshape(8, nblk, D_MODEL)


def _inproj_kernel(h_ref, mod_ref, n1w_ref, wqkv_ref, qkw_ref, cos_ref, sin_ref, wxbc_ref, wdt_ref,
                   dtb_ref, wr_ref, cosr_ref, sinr_ref,
                   qt_ref, k_ref, vt_ref, xbc_ref, dt_ref, rq_ref, rk_ref, rv_ref):
    h = h_ref[0]
    u = _rms_rows(h, n1w_ref[...]) * (1.0 + mod_ref[0, 1:2, :]) + mod_ref[0, 0:1, :]
    ub = u.astype(BF16)

    qkv_t = _dot_nt(wqkv_ref[...], ub)
    cos = cos_ref[...]
    sin = sin_ref[...]
    k_rows = []
    for hd in range(ATTN_HEADS + ATTN_KV_HEADS):
        r0 = hd * ATTN_HEAD_DIM
        xh = qkv_t[r0:r0 + ATTN_HEAD_DIM]
        yh = xh * lax.rsqrt(jnp.mean(xh * xh, axis=0, keepdims=True) + NORM_EPS) * qkw_ref[r0:r0 + ATTN_HEAD_DIM, :]
        y1 = yh[:ATTN_HALF]
        y2 = yh[ATTN_HALF:]
        o1 = y1 * cos - y2 * sin
        o2 = y1 * sin + y2 * cos
        if hd < ATTN_HEADS:
            scale = ATTN_HEAD_DIM ** -0.5 * math.log2(math.e)
            qt_ref[0, r0:r0 + ATTN_HALF, :] = (o1 * scale).astype(BF16)
            qt_ref[0, r0 + ATTN_HALF:r0 + ATTN_HEAD_DIM, :] = (o2 * scale).astype(BF16)
        else:
            k_rows += [o1, o2]
    k_t = jnp.concatenate(k_rows, axis=0)
    k_ref[0] = jnp.transpose(k_t).astype(BF16)
    vt_ref[0] = qkv_t[ATTN_Q_DIM + ATTN_KV_DIM:].astype(BF16)

    xbc_ref[0] = _dot(ub, wxbc_ref[...])
    dt_raw = _dot(ub, wdt_ref[...]) + dtb_ref[...]
    dt_ref[0] = jnp.maximum(dt_raw, 0.0) + jnp.log1p(jnp.exp(-jnp.abs(dt_raw)))

    r = _dot(ub, wr_ref[...])
    cosr = cosr_ref[...]
    sinr = sinr_ref[...]
    for hd in range(RET_HEADS):
        c0 = hd * RET_DK
        qh = r[:, c0:c0 + RET_DK]
        kh = r[:, RET_DIM + c0:RET_DIM + c0 + RET_DK]
        rq_ref[0, :, c0:c0 + RET_DK] = (qh * cosr + pltpu.roll(qh, RET_DK // 2, 1) * sinr).astype(BF16)
        rk_ref[0, :, c0:c0 + RET_DK] = ((kh * cosr + pltpu.roll(kh, RET_DK // 2, 1) * sinr)
                                        * (RET_DK ** -0.5)).astype(BF16)
    rv_ref[0] = r[:, 2 * RET_DIM:].astype(BF16)


def _inproj_call(h, mods, n1w, wts, tabs, nl):
    b, l, _ = h.shape
    nt, ctx_i = _tok_tiles(l, nl)
    wqkv, qkw, wxbc, wdt, dtb, wr = wts
    cos_a, sin_a, cos_r, sin_r = tabs
    const = _resident
    tok = lambda c: pl.BlockSpec((1, TOK_TILE, c), lambda bi, i: (bi, i, 0))
    tok_t = lambda r: pl.BlockSpec((1, r, TOK_TILE), lambda bi, i: (bi, 0, i))
    weights_bytes = 2 * (wqkv.size + wxbc.size + wdt.size + wr.size) + 4 * qkw.size
    return pl.pallas_call(
        _inproj_kernel,
        out_shape=(jax.ShapeDtypeStruct((b, ATTN_Q_DIM, l), BF16),
                   jax.ShapeDtypeStruct((b, l, ATTN_KV_DIM), BF16),
                   jax.ShapeDtypeStruct((b, ATTN_KV_DIM, l), BF16),
                   jax.ShapeDtypeStruct((b, l, SSD_CONV_DIM), F32),
                   jax.ShapeDtypeStruct((b, l, DT_PAD), F32),
                   jax.ShapeDtypeStruct((b, l, RET_DIM), BF16),
                   jax.ShapeDtypeStruct((b, l, RET_DIM), BF16),
                   jax.ShapeDtypeStruct((b, l, RET_DIM), BF16)),
        grid=(b, nt),
        in_specs=[tok(D_MODEL),
                  pl.BlockSpec((1, 6, D_MODEL), lambda bi, i: (jnp.where(i == ctx_i, 4, bi), 0, 0)),
                  const((1, D_MODEL)),
                  const(wqkv.shape), const(qkw.shape),
                  pl.BlockSpec((ATTN_HALF, TOK_TILE), lambda bi, i: (0, i)),
                  pl.BlockSpec((ATTN_HALF, TOK_TILE), lambda bi, i: (0, i)),
                  const(wxbc.shape), const(wdt.shape), const(dtb.shape), const(wr.shape),
                  pl.BlockSpec((TOK_TILE, RET_DK), lambda bi, i: (i, 0)),
                  pl.BlockSpec((TOK_TILE, RET_DK), lambda bi, i: (i, 0))],
        out_specs=(tok_t(ATTN_Q_DIM), tok(ATTN_KV_DIM), tok_t(ATTN_KV_DIM), tok(SSD_CONV_DIM), tok(DT_PAD),
                   tok(RET_DIM), tok(RET_DIM), tok(RET_DIM)),
        compiler_params=pltpu.CompilerParams(dimension_semantics=("arbitrary", "arbitrary"),
                                             vmem_limit_bytes=_vmem_limit(weights_bytes)),
        name="inproj",
    )(h, mods, n1w, wqkv, qkw, cos_a, sin_a, wxbc, wdt, dtb, wr, cos_r, sin_r)


def _attn_kernel(qt_ref, k_ref, vt_ref, o_ref, qpad_sc, m_sc, acc_sc, s0_sc, s1_sc, mx0_sc, mx1_sc, *, nl, kc):
    kv = pl.program_id(1)
    i = pl.program_id(2)
    ctx_i = nl * TILE // ATTN_Q_TILE

    row = lax.broadcasted_iota(jnp.int32, (2 * ATTN_HEAD_DIM, ATTN_Q_TILE), 0)
    keep = (row >= ATTN_HEAD_DIM).astype(jnp.int32) == kv
    for g in range(ATTN_GROUP):
        qg = qt_ref[0, g * ATTN_HEAD_DIM:(g + 1) * ATTN_HEAD_DIM, :]
        qq = jnp.concatenate([qg, qg], axis=0)
        qpad_sc[:, g * ATTN_Q_TILE:(g + 1) * ATTN_Q_TILE] = jnp.where(keep, qq, jnp.zeros_like(qq))
    m_sc[...] = jnp.full(m_sc.shape, NEG_BIG, F32)
    acc_sc[...] = jnp.zeros(acc_sc.shape, F32)

    wide = ATTN_GROUP * ATTN_Q_TILE

    def scores(off, size, c0, width):
        return _dot(k_ref[0, pl.ds(off, size), :], qpad_sc[:, c0:c0 + width])

    def accumulate(off, size, c0, s, mx):
        cols = slice(c0, c0 + ATTN_PV_TILE)
        m_old = m_sc[:, cols]
        m_new = jnp.maximum(m_old, mx)
        m_sc[:, cols] = m_new
        p = jnp.exp2(s - m_new).astype(BF16)
        v_aug = jnp.concatenate([vt_ref[0, :, pl.ds(off, size)], jnp.ones((ATTN_ONES_ROWS, size), BF16)], axis=0)
        acc_sc[:, cols] = jnp.exp2(m_old - m_new) * acc_sc[:, cols] + _dot(v_aug, p)

    def qk(off, size, c0, s_sc, mx_sc):
        s = scores(off, size, c0, ATTN_QK_TILE)
        s_sc[:size, c0:c0 + ATTN_QK_TILE] = s
        mx_sc[:, c0:c0 + ATTN_QK_TILE] = jnp.max(s, axis=0, keepdims=True)

    def pv(off, size, c0, s_sc, mx_sc):
        accumulate(off, size, c0, s_sc[:size, c0:c0 + ATTN_PV_TILE], mx_sc[:, c0:c0 + ATTN_PV_TILE])

    bufs = ((s0_sc, mx0_sc), (s1_sc, mx1_sc))

    def step(c, off_qk, off_pv):
        for q0 in range(0, wide, ATTN_QK_TILE):
            if c < len(sizes):
                qk(off_qk, sizes[c], q0, *bufs[c % 2])
            if c >= 1:
                for c0 in range(q0, q0 + ATTN_QK_TILE, ATTN_PV_TILE):
                    pv(off_pv, sizes[c - 1], c0, *bufs[(c - 1) % 2])

    n_mid, rest = divmod(nl * TILE, kc)
    sizes = [TILE] + [kc] * n_mid + ([rest] if rest else [])
    offs = [sum(sizes[:c]) for c in range(len(sizes) + 1)]
    steady = [c for c in range(1, len(sizes)) if sizes[c] == kc and sizes[c - 1] == kc]
    pairs = len(steady) // 2
    looped = steady[:2 * pairs]

    @pl.when(i != ctx_i)
    def _():
        for c in range(len(sizes) + 1):
            if c in looped:
                if c == looped[0]:
                    def body(j, carry):
                        for d in range(2):
                            off = pl.multiple_of(offs[looped[0] + d] + 2 * j * kc, TILE)
                            step(looped[0] + d, off, off - kc)
                        return carry
                    lax.fori_loop(0, pairs, body, 0)
            else:
                step(c, offs[c], offs[c - 1] if c >= 1 else None)

    @pl.when(i == ctx_i)
    def _():
        for c0 in range(0, wide, ATTN_PV_TILE):
            s = scores(nl * TILE, TILE, c0, ATTN_PV_TILE)
            accumulate(nl * TILE, TILE, c0, s, jnp.max(s, axis=0, keepdims=True))

    acc = acc_sc[...]
    out = acc[:ATTN_HEAD_DIM] / acc[ATTN_HEAD_DIM:ATTN_HEAD_DIM + 1]
    for g in range(ATTN_GROUP):
        o_ref[0, g * ATTN_HEAD_DIM:(g + 1) * ATTN_HEAD_DIM, :] = (
            out[:, g * ATTN_Q_TILE:(g + 1) * ATTN_Q_TILE].astype(BF16))


def _attn_call(qt, k, vt, nl):
    b, _, l = qt.shape
    assert (nl * TILE) % ATTN_Q_TILE == 0
    nt = pl.cdiv(l, ATTN_Q_TILE)
    kc = ATTN_KEY_CHUNK
    gq = ATTN_GROUP * ATTN_HEAD_DIM
    wide = ATTN_GROUP * ATTN_Q_TILE
    resident = 2 * (l * ATTN_KV_DIM + ATTN_HEAD_DIM * l) + 4 * kc * wide
    return pl.pallas_call(
        functools.partial(_attn_kernel, nl=nl, kc=kc),
        out_shape=jax.ShapeDtypeStruct((b, ATTN_Q_DIM, l), BF16),
        grid=(b, ATTN_KV_HEADS, nt),
        in_specs=[pl.BlockSpec((1, gq, ATTN_Q_TILE), lambda bi, kv, i: (bi, kv, i)),
                  pl.BlockSpec((1, l, ATTN_KV_DIM), lambda bi, kv, i: (bi, 0, 0)),
                  pl.BlockSpec((1, ATTN_HEAD_DIM, l), lambda bi, kv, i: (bi, kv, 0))],
        out_specs=pl.BlockSpec((1, gq, ATTN_Q_TILE), lambda bi, kv, i: (bi, kv, i)),
        scratch_shapes=[pltpu.VMEM((2 * ATTN_HEAD_DIM, wide), BF16),
                        pltpu.VMEM((1, wide), F32),
                        pltpu.VMEM((ATTN_HEAD_DIM + ATTN_ONES_ROWS, wide), F32),
                        pltpu.VMEM((kc, wide), F32), pltpu.VMEM((kc, wide), F32),
                        pltpu.VMEM((1, wide), F32), pltpu.VMEM((1, wide), F32)],
        compiler_params=pltpu.CompilerParams(dimension_semantics=("arbitrary",) * 3,
                                             vmem_limit_bytes=_vmem_limit(resident)),
        name="attn",
    )(qt, k, vt)


def _scan_tile(s, nl, reverse):
    return jnp.where(s == 0, nl, nl - s) if reverse else jnp.where(s == 0, nl, s - 1)


def _tri(reverse, n=CHUNK):
    r = lax.broadcasted_iota(jnp.int32, (n, n), 0)
    c = lax.broadcasted_iota(jnp.int32, (n, n), 1)
    return (c >= r) if reverse else (c <= r)


def _lane_bcast(x, c):
    return jnp.broadcast_to(x[:, c:c + 1], (x.shape[0], V7X_LANES))


def _ssd_tile(t, x_ref, xp_ref, xn_ref, dt_ref, cw_ref, cb_ref, alog_ref, dskip_ref, y_ref, st_sc,
              *, nl, reverse):
    x = x_ref[0]
    row = lax.broadcasted_iota(jnp.int32, (8, SSD_CONV_DIM), 0)
    has_prev = jnp.logical_and(t != nl, t != 0)
    has_next = t < nl - 1
    prev_row = jnp.where(has_prev, xp_ref[0, 7:8, :], 0.0)
    next_row = jnp.where(has_next, xn_ref[0, 0:1, :], 0.0)
    x_m1 = pltpu.roll(x, 1, 0)
    x_m1 = jnp.concatenate([jnp.where(row == 0, prev_row, x_m1[:8]), x_m1[8:]], axis=0)
    x_p1 = pltpu.roll(x, TILE - 1, 0)
    x_p1 = jnp.concatenate([x_p1[:TILE - 8], jnp.where(row == 7, next_row, x_p1[TILE - 8:])], axis=0)
    xs = _silu(cw_ref[0:1, :] * x_m1 + cw_ref[1:2, :] * x + cw_ref[2:3, :] * x_p1 + cb_ref[...])

    tri = _tri(reverse)
    tri_b = tri.astype(BF16)
    lane = lax.broadcasted_iota(jnp.int32, (CHUNK, V7X_LANES), 1)
    left = lane < SSD_HEAD_DIM
    col0 = SSD_HEADS if reverse else 0
    a_neg = -jnp.exp(alog_ref[...])
    last = 0 if reverse else CHUNK - 1

    chunks = range(TILE // CHUNK)
    for ci in (reversed(chunks) if reverse else chunks):
        r0 = ci * CHUNK
        xc = xs[r0:r0 + CHUNK]
        dtc = dt_ref[0, r0:r0 + CHUNK, :]
        a = dtc * a_neg
        hi, mid, lo = _split3(a)
        a_cum = _dot(tri_b, hi) + _dot(tri_b, mid) + _dot(tri_b, lo)
        a_cum_t = jnp.transpose(a_cum)
        for g in range(SSD_GROUPS):
            bm = xc[:, SSD_D_INNER + g * SSD_STATE:SSD_D_INNER + (g + 1) * SSD_STATE]
            cm = xc[:, SSD_D_INNER + (SSD_GROUPS + g) * SSD_STATE:SSD_D_INNER + (SSD_GROUPS + g + 1) * SSD_STATE]
            bmb = bm.astype(BF16)
            cmb = cm.astype(BF16)
            cb = _dot_nt(cmb, bmb)
            st_prev = st_sc[g]
            y_off = _dot(cmb, st_prev.astype(BF16))
            xdd_pairs = []
            tot = []
            for pr in range(SSD_HEADS_PER_GROUP // 2):
                h0 = g * SSD_HEADS_PER_GROUP + 2 * pr
                lanes0 = h0 * SSD_HEAD_DIM
                x2 = xc[:, lanes0:lanes0 + V7X_LANES]
                acol = [_lane_bcast(a_cum, col0 + h0 + j) for j in range(2)]
                dcol = [_lane_bcast(dtc, col0 + h0 + j) for j in range(2)]
                a2 = jnp.where(left, acol[0], acol[1])
                xd2 = x2 * jnp.where(left, dcol[0], dcol[1])
                xd2b = xd2.astype(BF16)
                y_pair = []
                for j in range(2):
                    arow = a_cum_t[col0 + h0 + j:col0 + h0 + j + 1, :]
                    lmat = jnp.where(tri, jnp.exp(jnp.where(tri, acol[j] - arow, 0.0)), 0.0)
                    y_pair.append(_dot((cb * lmat).astype(BF16), xd2b))
                y2 = jnp.where(left, y_pair[0], y_pair[1])
                y2 = y2 + jnp.exp(a2) * y_off[:, pr * V7X_LANES:(pr + 1) * V7X_LANES]
                if not reverse:
                    y2 = y2 + dskip_ref[:, lanes0:lanes0 + V7X_LANES] * x2
                y_ref[0, r0:r0 + CHUNK, lanes0:lanes0 + V7X_LANES] = y2
                a_tot = a2[last:last + 1, :]
                xdd_pairs.append((xd2 * jnp.exp(a_tot - a2)).astype(BF16))
                tot.append(a_tot)
            xdd = jnp.concatenate(xdd_pairs, axis=1)
            st_sc[g] = jnp.exp(jnp.concatenate(tot, axis=1)) * st_prev + _dot_tn(bmb, xdd)


def _ssd_kernel(xf_ref, xpf_ref, xnf_ref, dtf_ref, xb_ref, xpb_ref, xnb_ref, dtb_ref, cw_ref, cb_ref, alog_ref,
                dskip_ref, yf_ref, yb_ref, stf_sc, stb_sc, *, nl):
    s = pl.program_id(1)

    @pl.when(s == 0)
    def _():
        stf_sc[...] = jnp.zeros(stf_sc.shape, F32)
        stb_sc[...] = jnp.zeros(stb_sc.shape, F32)

    _ssd_tile(_scan_tile(s, nl, False), xf_ref, xpf_ref, xnf_ref, dtf_ref, cw_ref, cb_ref, alog_ref, dskip_ref,
              yf_ref, stf_sc, nl=nl, reverse=False)
    _ssd_tile(_scan_tile(s, nl, True), xb_ref, xpb_ref, xnb_ref, dtb_ref, cw_ref, cb_ref, alog_ref, dskip_ref,
              yb_ref, stb_sc, nl=nl, reverse=True)


def _ssd_call(xbc, dt, cw, cb, alog, dskip, nl):
    b, l, _ = xbc.shape
    nt = l // TILE
    rows8 = TILE // 8
    const = lambda shape: pl.BlockSpec(shape, lambda bi, s: (0,) * len(shape))

    def stream(reverse):
        tile_of = lambda s: _scan_tile(s, nl, reverse)
        return [pl.BlockSpec((1, TILE, SSD_CONV_DIM), lambda bi, s: (bi, tile_of(s), 0)),
                pl.BlockSpec((1, 8, SSD_CONV_DIM), lambda bi, s: (bi, jnp.maximum(tile_of(s) * rows8 - 1, 0), 0)),
                pl.BlockSpec((1, 8, SSD_CONV_DIM),
                             lambda bi, s: (bi, jnp.minimum((tile_of(s) + 1) * rows8, nt * rows8 - 1), 0)),
                pl.BlockSpec((1, TILE, DT_PAD), lambda bi, s: (bi, tile_of(s), 0))]

    out = lambda reverse: pl.BlockSpec((1, TILE, SSD_D_INNER), lambda bi, s: (bi, _scan_tile(s, nl, reverse), 0))
    state = pltpu.VMEM((SSD_GROUPS, SSD_STATE, SSD_HEADS_PER_GROUP * SSD_HEAD_DIM), F32)
    y = jax.ShapeDtypeStruct((b, l, SSD_D_INNER), F32)
    return pl.pallas_call(
        functools.partial(_ssd_kernel, nl=nl),
        out_shape=(y, y),
        grid=(b, nt),
        in_specs=stream(False) + stream(True) + [const(cw.shape), const(cb.shape), const(alog.shape),
                                                 const(dskip.shape)],
        out_specs=(out(False), out(True)),
        scratch_shapes=[state, state],
        compiler_params=pltpu.CompilerParams(dimension_semantics=("arbitrary", "arbitrary")),
        name="ssd",
    )(xbc, xbc, xbc, dt, xbc, xbc, xbc, dt, cw, cb, alog, dskip)


def _ret_tile(q_ref, k_ref, v_ref, ld_ref, y_ref, st_sc, *, reverse):
    n = RET_SCAN_CHUNK
    tri = _tri(reverse, n)
    r = lax.broadcasted_iota(jnp.int32, (n, n), 0)
    c = lax.broadcasted_iota(jnp.int32, (n, n), 1)
    dist = jnp.maximum((c - r) if reverse else (r - c), 0).astype(F32)
    rk = lax.broadcasted_iota(jnp.int32, (n, RET_DK), 0)
    pos = ((n - 1 - rk) if reverse else rk).astype(F32)
    col0 = RET_HEADS if reverse else 0

    for hd in range(RET_HEADS):
        c0 = hd * RET_DK
        log_g = -jnp.exp(ld_ref[0:1, col0 + hd:col0 + hd + 1])
        q = q_ref[0, :, c0:c0 + RET_DK]
        k = k_ref[0, :, c0:c0 + RET_DK]
        v = v_ref[0, :, c0:c0 + RET_DV]
        st_prev = st_sc[hd]
        sc = _dot_nt(q, k) * jnp.where(tri, jnp.exp(dist * jnp.broadcast_to(log_g, (n, n))), 0.0)
        lg = jnp.broadcast_to(log_g, (n, RET_DK))
        q_dec = (q.astype(F32) * jnp.exp((pos + 1.0) * lg)).astype(BF16)
        y_ref[0, :, c0:c0 + RET_DV] = _dot(jnp.concatenate([sc.astype(BF16), q_dec], axis=1),
                                           jnp.concatenate([v, st_prev.astype(BF16)], axis=0))
        k_dec = (k.astype(F32) * jnp.exp((n - 1.0 - pos) * lg)).astype(BF16)
        st_sc[hd] = jnp.exp(n * jnp.broadcast_to(log_g, (RET_DK, RET_DV))) * st_prev + _dot_tn(k_dec, v)


def _ret_kernel(qf_ref, kf_ref, vf_ref, qb_ref, kb_ref, vb_ref, ld_ref, yf_ref, yb_ref, stf_sc, stb_sc):
    @pl.when(pl.program_id(1) == 0)
    def _():
        stf_sc[...] = jnp.zeros(stf_sc.shape, F32)
        stb_sc[...] = jnp.zeros(stb_sc.shape, F32)

    _ret_tile(qf_ref, kf_ref, vf_ref, ld_ref, yf_ref, stf_sc, reverse=False)
    _ret_tile(qb_ref, kb_ref, vb_ref, ld_ref, yb_ref, stb_sc, reverse=True)


def _ret_call(rq, rk, rv, ld, nl):
    b, l, _ = rq.shape
    nt = l // TILE
    tok = lambda reverse: pl.BlockSpec((1, TILE, RET_DIM), lambda bi, s: (bi, _scan_tile(s, nl, reverse), 0))
    state = pltpu.VMEM((RET_HEADS, RET_DK, RET_DV), F32)
    y = jax.ShapeDtypeStruct((b, l, RET_DIM), F32)
    return pl.pallas_call(
        _ret_kernel,
        out_shape=(y, y),
        grid=(b, nt),
        in_specs=[tok(False)] * 3 + [tok(True)] * 3 + [pl.BlockSpec(ld.shape, lambda bi, s: (0, 0))],
        out_specs=(tok(False), tok(True)),
        scratch_shapes=[state, state],
        compiler_params=pltpu.CompilerParams(dimension_semantics=("arbitrary", "arbitrary")),
        name="ret",
    )(rq, rk, rv, rq, rk, rv, ld)


def _merge_kernel(h_ref, mod_ref, n1w_ref, ot_ref, sf_ref, sb_ref, rf_ref, rb_ref, wzg_ref, wgate_ref,
                  snw_ref, gnw_ref, wb_ref, wout_ref, o_ref):
    h = h_ref[0]
    u = _rms_rows(h, n1w_ref[...]) * (1.0 + mod_ref[0, 1:2, :]) + mod_ref[0, 0:1, :]
    ub = u.astype(BF16)
    zg = _dot(ub, wzg_ref[...])
    gates = _sigmoid(_dot(ub, wgate_ref[...]))

    br_attn = _dot_tn(ot_ref[0], wb_ref[0])

    y = (sf_ref[0] + sb_ref[0]) * _silu(zg[:, :SSD_D_INNER])
    br_ssd = _dot(_rms_rows(y, snw_ref[...]).astype(BF16), wb_ref[1])

    yr = rf_ref[0] + rb_ref[0]
    heads = []
    for hd in range(RET_HEADS):
        yh = yr[:, hd * RET_DV:(hd + 1) * RET_DV]
        yc = yh - jnp.mean(yh, axis=-1, keepdims=True)
        heads.append(yc * lax.rsqrt(jnp.mean(yc * yc, axis=-1, keepdims=True) + NORM_EPS))
    yn = jnp.concatenate(heads, axis=1) * gnw_ref[...] * _silu(zg[:, SSD_D_INNER:])
    br_ret = _dot(yn.astype(BF16), wb_ref[2])

    merged = (gates[:, :D_MODEL] * br_attn + gates[:, D_MODEL:2 * D_MODEL] * br_ssd
              + gates[:, 2 * D_MODEL:] * br_ret)
    o_ref[0] = h + mod_ref[0, 2:3, :] * _dot(merged.astype(BF16), wout_ref[...])


def _merge_call(h, mods, n1w, ot, sf, sb, rf, rb, wts, nl):
    b, l, _ = h.shape
    nt, ctx_i = _tok_tiles(l, nl)
    wzg, wgate, snw, gnw, wb, wout = wts
    const = _resident
    tok = lambda c: pl.BlockSpec((1, TOK_TILE, c), lambda bi, i: (bi, i, 0))
    weights_bytes = 2 * (wzg.size + wgate.size + wb.size + wout.size)
    return pl.pallas_call(
        _merge_kernel,
        out_shape=jax.ShapeDtypeStruct((b, l, D_MODEL), F32),
        grid=(b, nt),
        in_specs=[tok(D_MODEL),
                  pl.BlockSpec((1, 6, D_MODEL), lambda bi, i: (jnp.where(i == ctx_i, 4, bi), 0, 0)),
                  const((1, D_MODEL)),
                  pl.BlockSpec((1, ATTN_Q_DIM, TOK_TILE), lambda bi, i: (bi, 0, i)),
                  tok(SSD_D_INNER), tok(SSD_D_INNER), tok(RET_DIM), tok(RET_DIM),
                  const(wzg.shape), const(wgate.shape), const(snw.shape), const(gnw.shape),
                  const(wb.shape), const(wout.shape)],
        out_specs=tok(D_MODEL),
        compiler_params=pltpu.CompilerParams(dimension_semantics=("arbitrary", "arbitrary"),
                                             vmem_limit_bytes=_vmem_limit(weights_bytes)),
        name="merge",
    )(h, mods, n1w, ot, sf, sb, rf, rb, wzg, wgate, snw, gnw, wb, wout)


def _mlp_kernel(h_ref, mod_ref, n2w_ref, w1_ref, w2_ref, fw_ref, o_ref, *, final):
    h = h_ref[0]
    v = _rms_rows(h, n2w_ref[...]) * (1.0 + mod_ref[0, 4:5, :]) + mod_ref[0, 3:4, :]
    a = jnp.maximum(_dot(v.astype(BF16), w1_ref[...]), 0.0)
    out = h + mod_ref[0, 5:6, :] * _dot((a * a).astype(BF16), w2_ref[...])
    o_ref[0] = _rms_rows(out, fw_ref[...]) if final else out


def _mlp_call(h, mods, n2w, w1, w2, fw, nl, final):
    b, l, _ = h.shape
    nt, ctx_i = _tok_tiles(l, nl)
    if final:
        nt, l = ctx_i, nl * TILE
    const = _resident
    tok = pl.BlockSpec((1, TOK_TILE, D_MODEL), lambda bi, i: (bi, i, 0))
    return pl.pallas_call(
        functools.partial(_mlp_kernel, final=final),
        out_shape=jax.ShapeDtypeStruct((b, l, D_MODEL), F32),
        grid=(b, nt),
        in_specs=[tok,
                  pl.BlockSpec((1, 6, D_MODEL), lambda bi, i: (jnp.where(i == ctx_i, 4, bi), 0, 0)),
                  const((1, D_MODEL)), const(w1.shape), const(w2.shape), const((1, D_MODEL))],
        out_specs=tok,
        compiler_params=pltpu.CompilerParams(dimension_semantics=("arbitrary", "arbitrary"),
                                             vmem_limit_bytes=_vmem_limit(2 * (w1.size + w2.size))),
        name="mlp_final" if final else "mlp",
    )(h, mods, n2w, w1, w2, fw)


def _rope_tables(n, m):
    rows = n // GRID_W
    row = jnp.repeat(jnp.arange(rows, dtype=F32), GRID_W)
    col = jnp.tile(jnp.arange(GRID_W, dtype=F32), rows)
    inv = ROPE_THETA ** (-jnp.arange(ATTN_AXIS_FREQS, dtype=F32) / ATTN_AXIS_FREQS)
    ang = jnp.concatenate([row[:, None] * inv, col[:, None] * inv], axis=-1)
    cos_a = jnp.concatenate([jnp.cos(ang), jnp.ones((m, ATTN_HALF), F32)], axis=0).T
    sin_a = jnp.concatenate([jnp.sin(ang), jnp.zeros((m, ATTN_HALF), F32)], axis=0).T
    pos = jnp.concatenate([jnp.arange(n, dtype=F32) + m, jnp.arange(m, dtype=F32)])
    inv_r = ROPE_THETA ** (-jnp.linspace(0.0, 1.0, RET_DK // 2, dtype=F32))
    ang_r = pos[:, None] * inv_r
    cos_r = jnp.concatenate([jnp.cos(ang_r), jnp.cos(ang_r)], axis=-1)
    sin_r = jnp.concatenate([-jnp.sin(ang_r), jnp.sin(ang_r)], axis=-1)
    return cos_a, sin_a, cos_r, sin_r


def _layer_weights(w_in, q_norm, k_norm, dt_bias):
    offs = [0]
    for sz in IN_SPLITS:
        offs.append(offs[-1] + sz)
    col = lambda j: w_in[:, offs[j]:offs[j + 1]]
    wqkv = jnp.concatenate([col(0), col(1), col(2)], axis=1).T.astype(BF16)
    qkw = jnp.concatenate([jnp.tile(q_norm, ATTN_HEADS), jnp.tile(k_norm, ATTN_KV_HEADS)])
    qkw = jnp.broadcast_to(qkw[:, None], (ATTN_Q_DIM + ATTN_KV_DIM, TOK_TILE)).astype(F32)
    wxbc = col(4).astype(BF16)
    wdt = jnp.pad(col(5), ((0, 0), (0, DT_PAD - 2 * SSD_HEADS))).astype(BF16)
    dtb = jnp.pad(dt_bias.reshape(1, -1), ((0, 0), (0, DT_PAD - 2 * SSD_HEADS))).astype(F32)
    wr = jnp.concatenate([col(6), col(7), col(8)], axis=1).astype(BF16)
    wzg = jnp.concatenate([col(3), col(9)], axis=1).astype(BF16)
    wgate = col(10).astype(BF16)
    return (wqkv, qkw, wxbc, wdt, dtb, wr), (wzg, wgate)


def _pad_lanes(v):
    v = v.reshape(1, -1).astype(F32)
    return jnp.pad(v, ((0, 0), (0, V7X_LANES - v.shape[1])))


def kernel(x, c, ctx, c_ctx, w_mod, b_mod, norm1_w, norm2_w, w_in, attn_q_norm, attn_k_norm, ssd_conv_w,
           ssd_conv_b, ssd_dt_bias, ssd_a_log, ssd_d, ssd_norm_w, ret_log_decay, ret_gn_w, w_branch, w_out,
           w_mlp1, w_mlp2, final_norm_w):
    b, n, d = x.shape
    m = ctx.shape[1]
    depth = w_in.shape[0]
    assert d == D_MODEL and m == TILE and n % TILE == 0 and n % GRID_W == 0 and b <= 4
    nl = n // TILE

    tabs = _rope_tables(n, m)
    cc = jnp.zeros((8, D_MODEL), F32).at[:b].set(c).at[4].set(c_ctx)
    h = jnp.concatenate([x, ctx], axis=1)

    for layer in range(depth):
        final = layer == depth - 1
        in_w, (wzg, wgate) = _layer_weights(w_in[layer], attn_q_norm[layer], attn_k_norm[layer],
                                            ssd_dt_bias[layer])
        n1w = norm1_w[layer].reshape(1, -1)
        mods = _mod_call(cc, w_mod[layer], b_mod[layer])

        qt, k, vt, xbc, dt, rq, rk, rv = _inproj_call(h, mods, n1w, in_w, tabs, nl)
        ot = _attn_call(qt, k, vt, nl)

        cw = jnp.pad(ssd_conv_w[layer], ((0, 8 - SSD_CONV_K), (0, 0)))
        cb = ssd_conv_b[layer].reshape(1, -1)
        alog = _pad_lanes(ssd_a_log[layer])
        dskip = jnp.repeat(ssd_d[layer], SSD_HEAD_DIM).reshape(1, -1)
        sf, sb = _ssd_call(xbc, dt, cw, cb, alog, dskip, nl)

        ld = _pad_lanes(ret_log_decay[layer])
        rf, rb = _ret_call(rq, rk, rv, ld, nl)

        merge_w = (wzg, wgate, ssd_norm_w[layer].reshape(1, -1), ret_gn_w[layer].reshape(1, -1),
                   w_branch[layer].astype(BF16), w_out[layer].astype(BF16))
        h = _merge_call(h, mods, n1w, ot, sf, sb, rf, rb, merge_w, nl)
        h = _mlp_call(h, mods, norm2_w[layer].reshape(1, -1), w_mlp1[layer].astype(BF16),
                      w_mlp2[layer].astype(BF16), final_norm_w.reshape(1, -1), nl, final)
    return h
```

```python
import functools
import math

import jax
import jax.numpy as jnp
from jax import lax
from jax.experimental import pallas as pl
from jax.experimental.pallas import tpu as pltpu

F32 = jnp.float32
BF16 = jnp.bfloat16

D_MODEL = 1024
GRID_W = 64
NORM_EPS = 1e-6
ROPE_THETA = 10000.0

ATTN_HEADS = 8
ATTN_KV_HEADS = 2
ATTN_GROUP = ATTN_HEADS // ATTN_KV_HEADS
ATTN_HEAD_DIM = 64
ATTN_HALF = ATTN_HEAD_DIM // 2
ATTN_AXIS_FREQS = ATTN_HEAD_DIM // 4
ATTN_Q_DIM = ATTN_HEADS * ATTN_HEAD_DIM
ATTN_KV_DIM = ATTN_KV_HEADS * ATTN_HEAD_DIM

SSD_HEADS = 8
SSD_HEAD_DIM = 64
SSD_D_INNER = SSD_HEADS * SSD_HEAD_DIM
SSD_GROUPS = 2
SSD_STATE = 128
SSD_CONV_K = 3
SSD_CONV_DIM = SSD_D_INNER + 2 * SSD_GROUPS * SSD_STATE
SSD_HEADS_PER_GROUP = SSD_HEADS // SSD_GROUPS

RET_HEADS = 4
RET_DK = 128
RET_DV = 128
RET_DIM = RET_HEADS * RET_DK

N_BRANCH = 3
BRANCH_W = 512
MLP_HIDDEN = 4 * D_MODEL

IN_SPLITS = (ATTN_Q_DIM, ATTN_KV_DIM, ATTN_KV_DIM, SSD_D_INNER, SSD_CONV_DIM, 2 * SSD_HEADS,
             RET_DIM, RET_DIM, RET_HEADS * RET_DV, RET_HEADS * RET_DV, N_BRANCH * D_MODEL)

V7X_LANES = 128
V7X_VMEM_BYTES = 64 * 1024 * 1024

TILE = 256
CHUNK = 128
RET_SCAN_CHUNK = TILE
TOK_TILE = 512
DT_PAD = V7X_LANES
NEG_BIG = -1e30
ATTN_KEY_CHUNK = 768
ATTN_Q_TILE = 1024
ATTN_QK_TILE = 256
ATTN_PV_TILE = 256
ATTN_ONES_ROWS = 16


def _resident(shape):
    return pl.BlockSpec(shape, lambda bi, i: (0,) * len(shape), pipeline_mode=pl.Buffered(1))


def _tok_tiles(l, nl):
    assert (nl * TILE) % TOK_TILE == 0
    return pl.cdiv(l, TOK_TILE), nl * TILE // TOK_TILE


def _vmem_limit(resident_bytes):
    return int(min(V7X_VMEM_BYTES - 8 * 1024 * 1024, 2 * resident_bytes + 16 * 1024 * 1024))


def _sigmoid(x):
    return 0.5 * jnp.tanh(0.5 * x) + 0.5


def _silu(x):
    return x * _sigmoid(x)


def _rms_rows(x, w):
    return x * lax.rsqrt(jnp.mean(x * x, axis=-1, keepdims=True) + NORM_EPS) * w


def _dot(a, b):
    return jnp.dot(a, b, preferred_element_type=F32)


def _dot_nt(a, b):
    return lax.dot_general(a, b, (((1,), (1,)), ((), ())), preferred_element_type=F32)


def _dot_tn(a, b):
    return lax.dot_general(a, b, (((0,), (0,)), ((), ())), preferred_element_type=F32)


def _split3(x):
    hi = x.astype(BF16)
    r1 = x - hi.astype(F32)
    mid = r1.astype(BF16)
    lo = (r1 - mid.astype(F32)).astype(BF16)
    return hi, mid, lo


def _mod_kernel(c_ref, w_ref, b_ref, o_ref):
    s = _silu(c_ref[...])
    o_ref[...] = jnp.dot(s, w_ref[...], preferred_element_type=F32,
                         precision=lax.Precision.HIGHEST) + b_ref[...]


def _mod_call(cc, w_mod, b_mod):
    nblk = w_mod.shape[1] // D_MODEL
    out = pl.pallas_call(
        _mod_kernel,
        out_shape=jax.ShapeDtypeStruct((8, nblk * D_MODEL), F32),
        grid=(nblk,),
        in_specs=[pl.BlockSpec((8, D_MODEL), lambda j: (0, 0)),
                  pl.BlockSpec((D_MODEL, D_MODEL), lambda j: (0, j)),
                  pl.BlockSpec((1, D_MODEL), lambda j: (0, j))],
        out_specs=pl.BlockSpec((8, D_MODEL), lambda j: (0, j)),
        compiler_params=pltpu.CompilerParams(dimension_semantics=("arbitrary",)),
        name="mod",
    )(cc, w_mod, b_mod.reshape(1, -1))
    return out.reshape(8, nblk, D_MODEL)


def _inproj_kernel(h_ref, mod_ref, n1w_ref, wqkv_ref, qkw_ref, cos_ref, sin_ref, wxbc_ref, wdt_ref,
                   dtb_ref, wr_ref, cosr_ref, sinr_ref,
                   qt_ref, k_ref, vt_ref, xbc_ref, dt_ref, rq_ref, rk_ref, rv_ref):
    h = h_ref[0]
    u = _rms_rows(h, n1w_ref[...]) * (1.0 + mod_ref[0, 1:2, :]) + mod_ref[0, 0:1, :]
    ub = u.astype(BF16)

    qkv_t = _dot_nt(wqkv_ref[...], ub)
    cos = cos_ref[...]
    sin = sin_ref[...]
    k_rows = []
    for hd in range(ATTN_HEADS + ATTN_KV_HEADS):
        r0 = hd * ATTN_HEAD_DIM
        xh = qkv_t[r0:r0 + ATTN_HEAD_DIM]
        yh = xh * lax.rsqrt(jnp.mean(xh * xh, axis=0, keepdims=True) + NORM_EPS) * qkw_ref[r0:r0 + ATTN_HEAD_DIM, :]
        y1 = yh[:ATTN_HALF]
        y2 = yh[ATTN_HALF:]
        o1 = y1 * cos - y2 * sin
        o2 = y1 * sin + y2 * cos
        if hd < ATTN_HEADS:
            scale = ATTN_HEAD_DIM ** -0.5 * math.log2(math.e)
            qt_ref[0, r0:r0 + ATTN_HALF, :] = (o1 * scale).astype(BF16)
            qt_ref[0, r0 + ATTN_HALF:r0 + ATTN_HEAD_DIM, :] = (o2 * scale).astype(BF16)
        else:
            k_rows += [o1, o2]
    k_t = jnp.concatenate(k_rows, axis=0)
    k_ref[0] = jnp.transpose(k_t).astype(BF16)
    vt_ref[0] = qkv_t[ATTN_Q_DIM + ATTN_KV_DIM:].astype(BF16)

    xbc_ref[0] = _dot(ub, wxbc_ref[...])
    dt_raw = _dot(ub, wdt_ref[...]) + dtb_ref[...]
    dt_ref[0] = jnp.maximum(dt_raw, 0.0) + jnp.log1p(jnp.exp(-jnp.abs(dt_raw)))

    r = _dot(ub, wr_ref[...])
    cosr = cosr_ref[...]
    sinr = sinr_ref[...]
    for hd in range(RET_HEADS):
        c0 = hd * RET_DK
        qh = r[:, c0:c0 + RET_DK]
        kh = r[:, RET_DIM + c0:RET_DIM + c0 + RET_DK]
        rq_ref[0, :, c0:c0 + RET_DK] = (qh * cosr + pltpu.roll(qh, RET_DK // 2, 1) * sinr).astype(BF16)
        rk_ref[0, :, c0:c0 + RET_DK] = ((kh * cosr + pltpu.roll(kh, RET_DK // 2, 1) * sinr)
                                        * (RET_DK ** -0.5)).astype(BF16)
    rv_ref[0] = r[:, 2 * RET_DIM:].astype(BF16)


def _inproj_call(h, mods, n1w, wts, tabs, nl):
    b, l, _ = h.shape
    nt, ctx_i = _tok_tiles(l, nl)
    wqkv, qkw, wxbc, wdt, dtb, wr = wts
    cos_a, sin_a, cos_r, sin_r = tabs
    const = _resident
    tok = lambda c: pl.BlockSpec((1, TOK_TILE, c), lambda bi, i: (bi, i, 0))
    tok_t = lambda r: pl.BlockSpec((1, r, TOK_TILE), lambda bi, i: (bi, 0, i))
    weights_bytes = 2 * (wqkv.size + wxbc.size + wdt.size + wr.size) + 4 * qkw.size
    return pl.pallas_call(
        _inproj_kernel,
        out_shape=(jax.ShapeDtypeStruct((b, ATTN_Q_DIM, l), BF16),
                   jax.ShapeDtypeStruct((b, l, ATTN_KV_DIM), BF16),
                   jax.ShapeDtypeStruct((b, ATTN_KV_DIM, l), BF16),
                   jax.ShapeDtypeStruct((b, l, SSD_CONV_DIM), F32),
                   jax.ShapeDtypeStruct((b, l, DT_PAD), F32),
                   jax.ShapeDtypeStruct((b, l, RET_DIM), BF16),
                   jax.ShapeDtypeStruct((b, l, RET_DIM), BF16),
                   jax.ShapeDtypeStruct((b, l, RET_DIM), BF16)),
        grid=(b, nt),
        in_specs=[tok(D_MODEL),
                  pl.BlockSpec((1, 6, D_MODEL), lambda bi, i: (jnp.where(i == ctx_i, 4, bi), 0, 0)),
                  const((1, D_MODEL)),
                  const(wqkv.shape), const(qkw.shape),
                  pl.BlockSpec((ATTN_HALF, TOK_TILE), lambda bi, i: (0, i)),
                  pl.BlockSpec((ATTN_HALF, TOK_TILE), lambda bi, i: (0, i)),
                  const(wxbc.shape), const(wdt.shape), const(dtb.shape), const(wr.shape),
                  pl.BlockSpec((TOK_TILE, RET_DK), lambda bi, i: (i, 0)),
                  pl.BlockSpec((TOK_TILE, RET_DK), lambda bi, i: (i, 0))],
        out_specs=(tok_t(ATTN_Q_DIM), tok(ATTN_KV_DIM), tok_t(ATTN_KV_DIM), tok(SSD_CONV_DIM), tok(DT_PAD),
                   tok(RET_DIM), tok(RET_DIM), tok(RET_DIM)),
        compiler_params=pltpu.CompilerParams(dimension_semantics=("arbitrary", "arbitrary"),
                                             vmem_limit_bytes=_vmem_limit(weights_bytes)),
        name="inproj",
    )(h, mods, n1w, wqkv, qkw, cos_a, sin_a, wxbc, wdt, dtb, wr, cos_r, sin_r)


def _attn_kernel(qt_ref, k_ref, vt_ref, o_ref, qpad_sc, m_sc, acc_sc, s0_sc, s1_sc, mx0_sc, mx1_sc, *, nl, kc):
    kv = pl.program_id(1)
    i = pl.program_id(2)
    ctx_i = nl * TILE // ATTN_Q_TILE

    row = lax.broadcasted_iota(jnp.int32, (2 * ATTN_HEAD_DIM, ATTN_Q_TILE), 0)
    keep = (row >= ATTN_HEAD_DIM).astype(jnp.int32) == kv
    for g in range(ATTN_GROUP):
        qg = qt_ref[0, g * ATTN_HEAD_DIM:(g + 1) * ATTN_HEAD_DIM, :]
        qq = jnp.concatenate([qg, qg], axis=0)
        qpad_sc[:, g * ATTN_Q_TILE:(g + 1) * ATTN_Q_TILE] = jnp.where(keep, qq, jnp.zeros_like(qq))
    m_sc[...] = jnp.full(m_sc.shape, NEG_BIG, F32)
    acc_sc[...] = jnp.zeros(acc_sc.shape, F32)

    wide = ATTN_GROUP * ATTN_Q_TILE

    def scores(off, size, c0, width):
        return _dot(k_ref[0, pl.ds(off, size), :], qpad_sc[:, c0:c0 + width])

    def accumulate(off, size, c0, s, mx):
        cols = slice(c0, c0 + ATTN_PV_TILE)
        m_old = m_sc[:, cols]
        m_new = jnp.maximum(m_old, mx)
        m_sc[:, cols] = m_new
        p = jnp.exp2(s - m_new).astype(BF16)
        v_aug = jnp.concatenate([vt_ref[0, :, pl.ds(off, size)], jnp.ones((ATTN_ONES_ROWS, size), BF16)], axis=0)
        acc_sc[:, cols] = jnp.exp2(m_old - m_new) * acc_sc[:, cols] + _dot(v_aug, p)

    def qk(off, size, c0, s_sc, mx_sc):
        s = scores(off, size, c0, ATTN_QK_TILE)
        s_sc[:size, c0:c0 + ATTN_QK_TILE] = s
        mx_sc[:, c0:c0 + ATTN_QK_TILE] = jnp.max(s, axis=0, keepdims=True)

    def pv(off, size, c0, s_sc, mx_sc):
        accumulate(off, size, c0, s_sc[:size, c0:c0 + ATTN_PV_TILE], mx_sc[:, c0:c0 + ATTN_PV_TILE])

    bufs = ((s0_sc, mx0_sc), (s1_sc, mx1_sc))

    def step(c, off_qk, off_pv):
        for q0 in range(0, wide, ATTN_QK_TILE):
            if c < len(sizes):
                qk(off_qk, sizes[c], q0, *bufs[c % 2])
            if c >= 1:
                for c0 in range(q0, q0 + ATTN_QK_TILE, ATTN_PV_TILE):
                    pv(off_pv, sizes[c - 1], c0, *bufs[(c - 1) % 2])

    n_mid, rest = divmod(nl * TILE, kc)
    sizes = [TILE] + [kc] * n_mid + ([rest] if rest else [])
    offs = [sum(sizes[:c]) for c in range(len(sizes) + 1)]
    steady = [c for c in range(1, len(sizes)) if sizes[c] == kc and sizes[c - 1] == kc]
    pairs = len(steady) // 2
    looped = steady[:2 * pairs]

    @pl.when(i != ctx_i)
    def _():
        for c in range(len(sizes) + 1):
            if c in looped:
                if c == looped[0]:
                    def body(j, carry):
                        for d in range(2):
                            off = pl.multiple_of(offs[looped[0] + d] + 2 * j * kc, TILE)
                            step(looped[0] + d, off, off - kc)
                        return carry
                    lax.fori_loop(0, pairs, body, 0)
            else:
                step(c, offs[c], offs[c - 1] if c >= 1 else None)

    @pl.when(i == ctx_i)
    def _():
        for c0 in range(0, wide, ATTN_PV_TILE):
            s = scores(nl * TILE, TILE, c0, ATTN_PV_TILE)
            accumulate(nl * TILE, TILE, c0, s, jnp.max(s, axis=0, keepdims=True))

    acc = acc_sc[...]
    out = acc[:ATTN_HEAD_DIM] / acc[ATTN_HEAD_DIM:ATTN_HEAD_DIM + 1]
    for g in range(ATTN_GROUP):
        o_ref[0, g * ATTN_HEAD_DIM:(g + 1) * ATTN_HEAD_DIM, :] = (
            out[:, g * ATTN_Q_TILE:(g + 1) * ATTN_Q_TILE].astype(BF16))


def _attn_call(qt, k, vt, nl):
    b, _, l = qt.shape
    assert (nl * TILE) % ATTN_Q_TILE == 0
    nt = pl.cdiv(l, ATTN_Q_TILE)
    kc = ATTN_KEY_CHUNK
    gq = ATTN_GROUP * ATTN_HEAD_DIM
    wide = ATTN_GROUP * ATTN_Q_TILE
    resident = 2 * (l * ATTN_KV_DIM + ATTN_HEAD_DIM * l) + 4 * kc * wide
    return pl.pallas_call(
        functools.partial(_attn_kernel, nl=nl, kc=kc),
        out_shape=jax.ShapeDtypeStruct((b, ATTN_Q_DIM, l), BF16),
        grid=(b, ATTN_KV_HEADS, nt),
        in_specs=[pl.BlockSpec((1, gq, ATTN_Q_TILE), lambda bi, kv, i: (bi, kv, i)),
                  pl.BlockSpec((1, l, ATTN_KV_DIM), lambda bi, kv, i: (bi, 0, 0)),
                  pl.BlockSpec((1, ATTN_HEAD_DIM, l), lambda bi, kv, i: (bi, kv, 0))],
        out_specs=pl.BlockSpec((1, gq, ATTN_Q_TILE), lambda bi, kv, i: (bi, kv, i)),
        scratch_shapes=[pltpu.VMEM((2 * ATTN_HEAD_DIM, wide), BF16),
                        pltpu.VMEM((1, wide), F32),
                        pltpu.VMEM((ATTN_HEAD_DIM + ATTN_ONES_ROWS, wide), F32),
                        pltpu.VMEM((kc, wide), F32), pltpu.VMEM((kc, wide), F32),
                        pltpu.VMEM((1, wide), F32), pltpu.VMEM((1, wide), F32)],
        compiler_params=pltpu.CompilerParams(dimension_semantics=("arbitrary",) * 3,
                                             vmem_limit_bytes=_vmem_limit(resident)),
        name="attn",
    )(qt, k, vt)


def _scan_tile(s, nl, reverse):
    return jnp.where(s == 0, nl, nl - s) if reverse else jnp.where(s == 0, nl, s - 1)


def _tri(reverse, n=CHUNK):
    r = lax.broadcasted_iota(jnp.int32, (n, n), 0)
    c = lax.broadcasted_iota(jnp.int32, (n, n), 1)
    return (c >= r) if reverse else (c <= r)


def _lane_bcast(x, c):
    return jnp.broadcast_to(x[:, c:c + 1], (x.shape[0], V7X_LANES))


def _ssd_tile(t, x_ref, xp_ref, xn_ref, dt_ref, cw_ref, cb_ref, alog_ref, dskip_ref, y_ref, st_sc,
              *, nl, reverse):
    x = x_ref[0]
    row = lax.broadcasted_iota(jnp.int32, (8, SSD_CONV_DIM), 0)
    has_prev = jnp.logical_and(t != nl, t != 0)
    has_next = t < nl - 1
    prev_row = jnp.where(has_prev, xp_ref[0, 7:8, :], 0.0)
    next_row = jnp.where(has_next, xn_ref[0, 0:1, :], 0.0)
    x_m1 = pltpu.roll(x, 1, 0)
    x_m1 = jnp.concatenate([jnp.where(row == 0, prev_row, x_m1[:8]), x_m1[8:]], axis=0)
    x_p1 = pltpu.roll(x, TILE - 1, 0)
    x_p1 = jnp.concatenate([x_p1[:TILE - 8], jnp.where(row == 7, next_row, x_p1[TILE - 8:])], axis=0)
    xs = _silu(cw_ref[0:1, :] * x_m1 + cw_ref[1:2, :] * x + cw_ref[2:3, :] * x_p1 + cb_ref[...])

    tri = _tri(reverse)
    tri_b = tri.astype(BF16)
    lane = lax.broadcasted_iota(jnp.int32, (CHUNK, V7X_LANES), 1)
    left = lane < SSD_HEAD_DIM
    col0 = SSD_HEADS if reverse else 0
    a_neg = -jnp.exp(alog_ref[...])
    last = 0 if reverse else CHUNK - 1

    chunks = range(TILE // CHUNK)
    for ci in (reversed(chunks) if reverse else chunks):
        r0 = ci * CHUNK
        xc = xs[r0:r0 + CHUNK]
        dtc = dt_ref[0, r0:r0 + CHUNK, :]
        a = dtc * a_neg
        hi, mid, lo = _split3(a)
        a_cum = _dot(tri_b, hi) + _dot(tri_b, mid) + _dot(tri_b, lo)
        a_cum_t = jnp.transpose(a_cum)
        for g in range(SSD_GROUPS):
            bm = xc[:, SSD_D_INNER + g * SSD_STATE:SSD_D_INNER + (g + 1) * SSD_STATE]
            cm = xc[:, SSD_D_INNER + (SSD_GROUPS + g) * SSD_STATE:SSD_D_INNER + (SSD_GROUPS + g + 1) * SSD_STATE]
            bmb = bm.astype(BF16)
            cmb = cm.astype(BF16)
            cb = _dot_nt(cmb, bmb)
            st_prev = st_sc[g]
            y_off = _dot(cmb, st_prev.astype(BF16))
            xdd_pairs = []
            tot = []
            for pr in range(SSD_HEADS_PER_GROUP // 2):
                h0 = g * SSD_HEADS_PER_GROUP + 2 * pr
                lanes0 = h0 * SSD_HEAD_DIM
                x2 = xc[:, lanes0:lanes0 + V7X_LANES]
                acol = [_lane_bcast(a_cum, col0 + h0 + j) for j in range(2)]
                dcol = [_lane_bcast(dtc, col0 + h0 + j) for j in range(2)]
                a2 = jnp.where(left, acol[0], acol[1])
                xd2 = x2 * jnp.where(left, dcol[0], dcol[1])
                xd2b = xd2.astype(BF16)
                y_pair = []
                for j in range(2):
                    arow = a_cum_t[col0 + h0 + j:col0 + h0 + j + 1, :]
                    lmat = jnp.where(tri, jnp.exp(jnp.where(tri, acol[j] - arow, 0.0)), 0.0)
                    y_pair.append(_dot((cb * lmat).astype(BF16), xd2b))
                y2 = jnp.where(left, y_pair[0], y_pair[1])
                y2 = y2 + jnp.exp(a2) * y_off[:, pr * V7X_LANES:(pr + 1) * V7X_LANES]
                if not reverse:
                    y2 = y2 + dskip_ref[:, lanes0:lanes0 + V7X_LANES] * x2
                y_ref[0, r0:r0 + CHUNK, lanes0:lanes0 + V7X_LANES] = y2
                a_tot = a2[last:last + 1, :]
                xdd_pairs.append((xd2 * jnp.exp(a_tot - a2)).astype(BF16))
                tot.append(a_tot)
            xdd = jnp.concatenate(xdd_pairs, axis=1)
            st_sc[g] = jnp.exp(jnp.concatenate(tot, axis=1)) * st_prev + _dot_tn(bmb, xdd)


def _ssd_kernel(xf_ref, xpf_ref, xnf_ref, dtf_ref, xb_ref, xpb_ref, xnb_ref, dtb_ref, cw_ref, cb_ref, alog_ref,
                dskip_ref, yf_ref, yb_ref, stf_sc, stb_sc, *, nl):
    s = pl.program_id(1)

    @pl.when(s == 0)
    def _():
        stf_sc[...] = jnp.zeros(stf_sc.shape, F32)
        stb_sc[...] = jnp.zeros(stb_sc.shape, F32)

    _ssd_tile(_scan_tile(s, nl, False), xf_ref, xpf_ref, xnf_ref, dtf_ref, cw_ref, cb_ref, alog_ref, dskip_ref,
              yf_ref, stf_sc, nl=nl, reverse=False)
    _ssd_tile(_scan_tile(s, nl, True), xb_ref, xpb_ref, xnb_ref, dtb_ref, cw_ref, cb_ref, alog_ref, dskip_ref,
              yb_ref, stb_sc, nl=nl, reverse=True)


def _ssd_call(xbc, dt, cw, cb, alog, dskip, nl):
    b, l, _ = xbc.shape
    nt = l // TILE
    rows8 = TILE // 8
    const = lambda shape: pl.BlockSpec(shape, lambda bi, s: (0,) * len(shape))

    def stream(reverse):
        tile_of = lambda s: _scan_tile(s, nl, reverse)
        return [pl.BlockSpec((1, TILE, SSD_CONV_DIM), lambda bi, s: (bi, tile_of(s), 0)),
                pl.BlockSpec((1, 8, SSD_CONV_DIM), lambda bi, s: (bi, jnp.maximum(tile_of(s) * rows8 - 1, 0), 0)),
                pl.BlockSpec((1, 8, SSD_CONV_DIM),
                             lambda bi, s: (bi, jnp.minimum((tile_of(s) + 1) * rows8, nt * rows8 - 1), 0)),
                pl.BlockSpec((1, TILE, DT_PAD), lambda bi, s: (bi, tile_of(s), 0))]

    out = lambda reverse: pl.BlockSpec((1, TILE, SSD_D_INNER), lambda bi, s: (bi, _scan_tile(s, nl, reverse), 0))
    state = pltpu.VMEM((SSD_GROUPS, SSD_STATE, SSD_HEADS_PER_GROUP * SSD_HEAD_DIM), F32)
    y = jax.ShapeDtypeStruct((b, l, SSD_D_INNER), F32)
    return pl.pallas_call(
        functools.partial(_ssd_kernel, nl=nl),
        out_shape=(y, y),
        grid=(b, nt),
        in_specs=stream(False) + stream(True) + [const(cw.shape), const(cb.shape), const(alog.shape),
                                                 const(dskip.shape)],
        out_specs=(out(False), out(True)),
        scratch_shapes=[state, state],
        compiler_params=pltpu.CompilerParams(dimension_semantics=("arbitrary", "arbitrary")),
        name="ssd",
    )(xbc, xbc, xbc, dt, xbc, xbc, xbc, dt, cw, cb, alog, dskip)


def _ret_tile(q_ref, k_ref, v_ref, ld_ref, y_ref, st_sc, *, reverse):
    n = RET_SCAN_CHUNK
    tri = _tri(reverse, n)
    r = lax.broadcasted_iota(jnp.int32, (n, n), 0)
    c = lax.broadcasted_iota(jnp.int32, (n, n), 1)
    dist = jnp.maximum((c - r) if reverse else (r - c), 0).astype(F32)
    rk = lax.broadcasted_iota(jnp.int32, (n, RET_DK), 0)
    pos = ((n - 1 - rk) if reverse else rk).astype(F32)
    col0 = RET_HEADS if reverse else 0

    for hd in range(RET_HEADS):
        c0 = hd * RET_DK
        log_g = -jnp.exp(ld_ref[0:1, col0 + hd:col0 + hd + 1])
        q = q_ref[0, :, c0:c0 + RET_DK]
        k = k_ref[0, :, c0:c0 + RET_DK]
        v = v_ref[0, :, c0:c0 + RET_DV]
        st_prev = st_sc[hd]
        sc = _dot_nt(q, k) * jnp.where(tri, jnp.exp(dist * jnp.broadcast_to(log_g, (n, n))), 0.0)
        lg = jnp.broadcast_to(log_g, (n, RET_DK))
        q_dec = (q.astype(F32) * jnp.exp((pos + 1.0) * lg)).astype(BF16)
        y_ref[0, :, c0:c0 + RET_DV] = _dot(jnp.concatenate([sc.astype(BF16), q_dec], axis=1),
                                           jnp.concatenate([v, st_prev.astype(BF16)], axis=0))
        k_dec = (k.astype(F32) * jnp.exp((n - 1.0 - pos) * lg)).astype(BF16)
        st_sc[hd] = jnp.exp(n * jnp.broadcast_to(log_g, (RET_DK, RET_DV))) * st_prev + _dot_tn(k_dec, v)


def _ret_kernel(qf_ref, kf_ref, vf_ref, qb_ref, kb_ref, vb_ref, ld_ref, yf_ref, yb_ref, stf_sc, stb_sc):
    @pl.when(pl.program_id(1) == 0)
    def _():
        stf_sc[...] = jnp.zeros(stf_sc.shape, F32)
        stb_sc[...] = jnp.zeros(stb_sc.shape, F32)

    _ret_tile(qf_ref, kf_ref, vf_ref, ld_ref, yf_ref, stf_sc, reverse=False)
    _ret_tile(qb_ref, kb_ref, vb_ref, ld_ref, yb_ref, stb_sc, reverse=True)


def _ret_call(rq, rk, rv, ld, nl):
    b, l, _ = rq.shape
    nt = l // TILE
    tok = lambda reverse: pl.BlockSpec((1, TILE, RET_DIM), lambda bi, s: (bi, _scan_tile(s, nl, reverse), 0))
    state = pltpu.VMEM((RET_HEADS, RET_DK, RET_DV), F32)
    y = jax.ShapeDtypeStruct((b, l, RET_DIM), F32)
    return pl.pallas_call(
        _ret_kernel,
        out_shape=(y, y),
        grid=(b, nt),
        in_specs=[tok(False)] * 3 + [tok(True)] * 3 + [pl.BlockSpec(ld.shape, lambda bi, s: (0, 0))],
        out_specs=(tok(False), tok(True)),
        scratch_shapes=[state, state],
        compiler_params=pltpu.CompilerParams(dimension_semantics=("arbitrary", "arbitrary")),
        name="ret",
    )(rq, rk, rv, rq, rk, rv, ld)


def _merge_kernel(h_ref, mod_ref, n1w_ref, ot_ref, sf_ref, sb_ref, rf_ref, rb_ref, wzg_ref, wgate_ref,
                  snw_ref, gnw_ref, wb_ref, wout_ref, o_ref):
    h = h_ref[0]
    u = _rms_rows(h, n1w_ref[...]) * (1.0 + mod_ref[0, 1:2, :]) + mod_ref[0, 0:1, :]
    ub = u.astype(BF16)
    zg = _dot(ub, wzg_ref[...])
    gates = _sigmoid(_dot(ub, wgate_ref[...]))

    br_attn = _dot_tn(ot_ref[0], wb_ref[0])

    y = (sf_ref[0] + sb_ref[0]) * _silu(zg[:, :SSD_D_INNER])
    br_ssd = _dot(_rms_rows(y, snw_ref[...]).astype(BF16), wb_ref[1])

    yr = rf_ref[0] + rb_ref[0]
    heads = []
    for hd in range(RET_HEADS):
        yh = yr[:, hd * RET_DV:(hd + 1) * RET_DV]
        yc = yh - jnp.mean(yh, axis=-1, keepdims=True)
        heads.append(yc * lax.rsqrt(jnp.mean(yc * yc, axis=-1, keepdims=True) + NORM_EPS))
    yn = jnp.concatenate(heads, axis=1) * gnw_ref[...] * _silu(zg[:, SSD_D_INNER:])
    br_ret = _dot(yn.astype(BF16), wb_ref[2])

    merged = (gates[:, :D_MODEL] * br_attn + gates[:, D_MODEL:2 * D_MODEL] * br_ssd
              + gates[:, 2 * D_MODEL:] * br_ret)
    o_ref[0] = h + mod_ref[0, 2:3, :] * _dot(merged.astype(BF16), wout_ref[...])


def _merge_call(h, mods, n1w, ot, sf, sb, rf, rb, wts, nl):
    b, l, _ = h.shape
    nt, ctx_i = _tok_tiles(l, nl)
    wzg, wgate, snw, gnw, wb, wout = wts
    const = _resident
    tok = lambda c: pl.BlockSpec((1, TOK_TILE, c), lambda bi, i: (bi, i, 0))
    weights_bytes = 2 * (wzg.size + wgate.size + wb.size + wout.size)
    return pl.pallas_call(
        _merge_kernel,
        out_shape=jax.ShapeDtypeStruct((b, l, D_MODEL), F32),
        grid=(b, nt),
        in_specs=[tok(D_MODEL),
                  pl.BlockSpec((1, 6, D_MODEL), lambda bi, i: (jnp.where(i == ctx_i, 4, bi), 0, 0)),
                  const((1, D_MODEL)),
                  pl.BlockSpec((1, ATTN_Q_DIM, TOK_TILE), lambda bi, i: (bi, 0, i)),
                  tok(SSD_D_INNER), tok(SSD_D_INNER), tok(RET_DIM), tok(RET_DIM),
                  const(wzg.shape), const(wgate.shape), const(snw.shape), const(gnw.shape),
                  const(wb.shape), const(wout.shape)],
        out_specs=tok(D_MODEL),
        compiler_params=pltpu.CompilerParams(dimension_semantics=("arbitrary", "arbitrary"),
                                             vmem_limit_bytes=_vmem_limit(weights_bytes)),
        name="merge",
    )(h, mods, n1w, ot, sf, sb, rf, rb, wzg, wgate, snw, gnw, wb, wout)


def _mlp_kernel(h_ref, mod_ref, n2w_ref, w1_ref, w2_ref, fw_ref, o_ref, *, final):
    h = h_ref[0]
    v = _rms_rows(h, n2w_ref[...]) * (1.0 + mod_ref[0, 4:5, :]) + mod_ref[0, 3:4, :]
    a = jnp.maximum(_dot(v.astype(BF16), w1_ref[...]), 0.0)
    out = h + mod_ref[0, 5:6, :] * _dot((a * a).astype(BF16), w2_ref[...])
    o_ref[0] = _rms_rows(out, fw_ref[...]) if final else out


def _mlp_call(h, mods, n2w, w1, w2, fw, nl, final):
    b, l, _ = h.shape
    nt, ctx_i = _tok_tiles(l, nl)
    if final:
        nt, l = ctx_i, nl * TILE
    const = _resident
    tok = pl.BlockSpec((1, TOK_TILE, D_MODEL), lambda bi, i: (bi, i, 0))
    return pl.pallas_call(
        functools.partial(_mlp_kernel, final=final),
        out_shape=jax.ShapeDtypeStruct((b, l, D_MODEL), F32),
        grid=(b, nt),
        in_specs=[tok,
                  pl.BlockSpec((1, 6, D_MODEL), lambda bi, i: (jnp.where(i == ctx_i, 4, bi), 0, 0)),
                  const((1, D_MODEL)), const(w1.shape), const(w2.shape), const((1, D_MODEL))],
        out_specs=tok,
        compiler_params=pltpu.CompilerParams(dimension_semantics=("arbitrary", "arbitrary"),
                                             vmem_limit_bytes=_vmem_limit(2 * (w1.size + w2.size))),
        name="mlp_final" if final else "mlp",
    )(h, mods, n2w, w1, w2, fw)


def _rope_tables(n, m):
    rows = n // GRID_W
    row = jnp.repeat(jnp.arange(rows, dtype=F32), GRID_W)
    col = jnp.tile(jnp.arange(GRID_W, dtype=F32), rows)
    inv = ROPE_THETA ** (-jnp.arange(ATTN_AXIS_FREQS, dtype=F32) / ATTN_AXIS_FREQS)
    ang = jnp.concatenate([row[:, None] * inv, col[:, None] * inv], axis=-1)
    cos_a = jnp.concatenate([jnp.cos(ang), jnp.ones((m, ATTN_HALF), F32)], axis=0).T
    sin_a = jnp.concatenate([jnp.sin(ang), jnp.zeros((m, ATTN_HALF), F32)], axis=0).T
    pos = jnp.concatenate([jnp.arange(n, dtype=F32) + m, jnp.arange(m, dtype=F32)])
    inv_r = ROPE_THETA ** (-jnp.linspace(0.0, 1.0, RET_DK // 2, dtype=F32))
    ang_r = pos[:, None] * inv_r
    cos_r = jnp.concatenate([jnp.cos(ang_r), jnp.cos(ang_r)], axis=-1)
    sin_r = jnp.concatenate([-jnp.sin(ang_r), jnp.sin(ang_r)], axis=-1)
    return cos_a, sin_a, cos_r, sin_r


def _layer_weights(w_in, q_norm, k_norm, dt_bias):
    offs = [0]
    for sz in IN_SPLITS:
        offs.append(offs[-1] + sz)
    col = lambda j: w_in[:, offs[j]:offs[j + 1]]
    wqkv = jnp.concatenate([col(0), col(1), col(2)], axis=1).T.astype(BF16)
    qkw = jnp.concatenate([jnp.tile(q_norm, ATTN_HEADS), jnp.tile(k_norm, ATTN_KV_HEADS)])
    qkw = jnp.broadcast_to(qkw[:, None], (ATTN_Q_DIM + ATTN_KV_DIM, TOK_TILE)).astype(F32)
    wxbc = col(4).astype(BF16)
    wdt = jnp.pad(col(5), ((0, 0), (0, DT_PAD - 2 * SSD_HEADS))).astype(BF16)
    dtb = jnp.pad(dt_bias.reshape(1, -1), ((0, 0), (0, DT_PAD - 2 * SSD_HEADS))).astype(F32)
    wr = jnp.concatenate([col(6), col(7), col(8)], axis=1).astype(BF16)
    wzg = jnp.concatenate([col(3), col(9)], axis=1).astype(BF16)
    wgate = col(10).astype(BF16)
    return (wqkv, qkw, wxbc, wdt, dtb, wr), (wzg, wgate)


def _pad_lanes(v):
    v = v.reshape(1, -1).astype(F32)
    return jnp.pad(v, ((0, 0), (0, V7X_LANES - v.shape[1])))


def kernel(x, c, ctx, c_ctx, w_mod, b_mod, norm1_w, norm2_w, w_in, attn_q_norm, attn_k_norm, ssd_conv_w,
           ssd_conv_b, ssd_dt_bias, ssd_a_log, ssd_d, ssd_norm_w, ret_log_decay, ret_gn_w, w_branch, w_out,
           w_mlp1, w_mlp2, final_norm_w):
    b, n, d = x.shape
    m = ctx.shape[1]
    depth = w_in.shape[0]
    assert d == D_MODEL and m == TILE and n % TILE == 0 and n % GRID_W == 0 and b <= 4
    nl = n // TILE

    tabs = _rope_tables(n, m)
    cc = jnp.zeros((8, D_MODEL), F32).at[:b].set(c).at[4].set(c_ctx)
    h = jnp.concatenate([x, ctx], axis=1)

    for layer in range(depth):
        final = layer == depth - 1
        in_w, (wzg, wgate) = _layer_weights(w_in[layer], attn_q_norm[layer], attn_k_norm[layer],
                                            ssd_dt_bias[layer])
        n1w = norm1_w[layer].reshape(1, -1)
        mods = _mod_call(cc, w_mod[layer], b_mod[layer])

        qt, k, vt, xbc, dt, rq, rk, rv = _inproj_call(h, mods, n1w, in_w, tabs, nl)
        ot = _attn_call(qt, k, vt, nl)

        cw = jnp.pad(ssd_conv_w[layer], ((0, 8 - SSD_CONV_K), (0, 0)))
        cb = ssd_conv_b[layer].reshape(1, -1)
        alog = _pad_lanes(ssd_a_log[layer])
        dskip = jnp.repeat(ssd_d[layer], SSD_HEAD_DIM).reshape(1, -1)
        sf, sb = _ssd_call(xbc, dt, cw, cb, alog, dskip, nl)

        ld = _pad_lanes(ret_log_decay[layer])
        rf, rb = _ret_call(rq, rk, rv, ld, nl)

        merge_w = (wzg, wgate, ssd_norm_w[layer].reshape(1, -1), ret_gn_w[layer].reshape(1, -1),
                   w_branch[layer].astype(BF16), w_out[layer].astype(BF16))
        h = _merge_call(h, mods, n1w, ot, sf, sb, rf, rb, merge_w, nl)
        h = _mlp_call(h, mods, norm2_w[layer].reshape(1, -1), w_mlp1[layer].astype(BF16),
                      w_mlp2[layer].astype(BF16), final_norm_w.reshape(1, -1), nl, final)
    return h
```

```python
import functools
import math

import jax
import jax.numpy as jnp
from jax import lax
from jax.experimental import pallas as pl
from jax.experimental.pallas import tpu as pltpu

F32 = jnp.float32
BF16 = jnp.bfloat16

D_MODEL = 1024
GRID_W = 64
NORM_EPS = 1e-6
ROPE_THETA = 10000.0

ATTN_HEADS = 8
ATTN_KV_HEADS = 2
ATTN_GROUP = ATTN_HEADS // ATTN_KV_HEADS
ATTN_HEAD_DIM = 64
ATTN_HALF = ATTN_HEAD_DIM // 2
ATTN_AXIS_FREQS = ATTN_HEAD_DIM // 4
ATTN_Q_DIM = ATTN_HEADS * ATTN_HEAD_DIM
ATTN_KV_DIM = ATTN_KV_HEADS * ATTN_HEAD_DIM

SSD_HEADS = 8
SSD_HEAD_DIM = 64
SSD_D_INNER = SSD_HEADS * SSD_HEAD_DIM
SSD_GROUPS = 2
SSD_STATE = 128
SSD_CONV_K = 3
SSD_CONV_DIM = SSD_D_INNER + 2 * SSD_GROUPS * SSD_STATE
SSD_HEADS_PER_GROUP = SSD_HEADS // SSD_GROUPS

RET_HEADS = 4
RET_DK = 128
RET_DV = 128
RET_DIM = RET_HEADS * RET_DK

N_BRANCH = 3
BRANCH_W = 512
MLP_HIDDEN = 4 * D_MODEL

IN_SPLITS = (ATTN_Q_DIM, ATTN_KV_DIM, ATTN_KV_DIM, SSD_D_INNER, SSD_CONV_DIM, 2 * SSD_HEADS,
             RET_DIM, RET_DIM, RET_HEADS * RET_DV, RET_HEADS * RET_DV, N_BRANCH * D_MODEL)

V7X_LANES = 128
V7X_VMEM_BYTES = 64 * 1024 * 1024

TILE = 256
CHUNK = 128
RET_SCAN_CHUNK = TILE
TOK_TILE = 512
DT_PAD = V7X_LANES
NEG_BIG = -1e30
ATTN_KEY_CHUNK = 768
ATTN_Q_TILE = 1024
ATTN_QK_TILE = 256
ATTN_PV_TILE = 256
ATTN_ONES_ROWS = 16


def _resident(shape):
    return pl.BlockSpec(shape, lambda bi, i: (0,) * len(shape), pipeline_mode=pl.Buffered(1))


def _tok_tiles(l, nl):
    assert (nl * TILE) % TOK_TILE == 0
    return pl.cdiv(l, TOK_TILE), nl * TILE // TOK_TILE


def _vmem_limit(resident_bytes):
    return int(min(V7X_VMEM_BYTES - 8 * 1024 * 1024, 2 * resident_bytes + 16 * 1024 * 1024))


def _sigmoid(x):
    return 0.5 * jnp.tanh(0.5 * x) + 0.5


def _silu(x):
    h = 0.5 * x
    return h + h * jnp.tanh(h)


def _rms_rows(x, w):
    return x * lax.rsqrt(jnp.mean(x * x, axis=-1, keepdims=True) + NORM_EPS) * w


def _dot(a, b):
    return jnp.dot(a, b, preferred_element_type=F32)


def _dot_nt(a, b):
    return lax.dot_general(a, b, (((1,), (1,)), ((), ())), preferred_element_type=F32)


def _dot_tn(a, b):
    return lax.dot_general(a, b, (((0,), (0,)), ((), ())), preferred_element_type=F32)


def _split3(x):
    hi = x.astype(BF16)
    r1 = x - hi.astype(F32)
    mid = r1.astype(BF16)
    lo = (r1 - mid.astype(F32)).astype(BF16)
    return hi, mid, lo


def _mod_kernel(c_ref, w_ref, b_ref, o_ref):
    s = _silu(c_ref[...])
    o_ref[...] = jnp.dot(s, w_ref[...], preferred_element_type=F32,
                         precision=lax.Precision.HIGHEST) + b_ref[...]


def _mod_call(cc, w_mod, b_mod):
    nblk = w_mod.shape[1] // D_MODEL
    out = pl.pallas_call(
        _mod_kernel,
        out_shape=jax.ShapeDtypeStruct((8, nblk * D_MODEL), F32),
        grid=(nblk,),
        in_specs=[pl.BlockSpec((8, D_MODEL), lambda j: (0, 0)),
                  pl.BlockSpec((D_MODEL, D_MODEL), lambda j: (0, j)),
                  pl.BlockSpec((1, D_MODEL), lambda j: (0, j))],
        out_specs=pl.BlockSpec((8, D_MODEL), lambda j: (0, j)),
        compiler_params=pltpu.CompilerParams(dimension_semantics=("arbitrary",)),
        name="mod",
    )(cc, w_mod, b_mod.reshape(1, -1))
    return out.reshape(8, nblk, D_MODEL)


def _inproj_kernel(h_ref, mod_ref, n1w_ref, wqkv_ref, qkw_ref, cos_ref, sin_ref, wxbc_ref, wdt_ref,
                   dtb_ref, wr_ref, cosr_ref, sinr_ref,
                   qt_ref, k_ref, vt_ref, xbc_ref, dt_ref, rq_ref, rk_ref, rv_ref):
    h = h_ref[0]
    u = _rms_rows(h, n1w_ref[...]) * (1.0 + mod_ref[0, 1:2, :]) + mod_ref[0, 0:1, :]
    ub = u.astype(BF16)

    qkv_t = _dot_nt(wqkv_ref[...], ub)
    cos = cos_ref[...]
    sin = sin_ref[...]
    k_rows = []
    for hd in range(ATTN_HEADS + ATTN_KV_HEADS):
        r0 = hd * ATTN_HEAD_DIM
        xh = qkv_t[r0:r0 + ATTN_HEAD_DIM]
        yh = xh * lax.rsqrt(jnp.mean(xh * xh, axis=0, keepdims=True) + NORM_EPS) * qkw_ref[r0:r0 + ATTN_HEAD_DIM, :]
        y1 = yh[:ATTN_HALF]
        y2 = yh[ATTN_HALF:]
        o1 = y1 * cos - y2 * sin
        o2 = y1 * sin + y2 * cos
        if hd < ATTN_HEADS:
            scale = ATTN_HEAD_DIM ** -0.5 * math.log2(math.e)
            qt_ref[0, r0:r0 + ATTN_HALF, :] = (o1 * scale).astype(BF16)
            qt_ref[0, r0 + ATTN_HALF:r0 + ATTN_HEAD_DIM, :] = (o2 * scale).astype(BF16)
        else:
            k_rows += [o1, o2]
    k_t = jnp.concatenate(k_rows, axis=0)
    k_ref[0] = jnp.transpose(k_t).astype(BF16)
    vt_ref[0] = qkv_t[ATTN_Q_DIM + ATTN_KV_DIM:].astype(BF16)

    xbc_ref[0] = _dot(ub, wxbc_ref[...])
    dt_raw = _dot(ub, wdt_ref[...]) + dtb_ref[...]
    dt_ref[0] = jnp.maximum(dt_raw, 0.0) + jnp.log1p(jnp.exp(-jnp.abs(dt_raw)))

    r = _dot(ub, wr_ref[...])
    cosr = cosr_ref[...]
    sinr = sinr_ref[...]
    for hd in range(RET_HEADS):
        c0 = hd * RET_DK
        qh = r[:, c0:c0 + RET_DK]
        kh = r[:, RET_DIM + c0:RET_DIM + c0 + RET_DK]
        rq_ref[0, :, c0:c0 + RET_DK] = (qh * cosr + pltpu.roll(qh, RET_DK // 2, 1) * sinr).astype(BF16)
        rk_ref[0, :, c0:c0 + RET_DK] = ((kh * cosr + pltpu.roll(kh, RET_DK // 2, 1) * sinr)
                                        * (RET_DK ** -0.5)).astype(BF16)
    rv_ref[0] = r[:, 2 * RET_DIM:].astype(BF16)


def _inproj_call(h, mods, n1w, wts, tabs, nl):
    b, l, _ = h.shape
    nt, ctx_i = _tok_tiles(l, nl)
    wqkv, qkw, wxbc, wdt, dtb, wr = wts
    cos_a, sin_a, cos_r, sin_r = tabs
    const = _resident
    tok = lambda c: pl.BlockSpec((1, TOK_TILE, c), lambda bi, i: (bi, i, 0))
    tok_t = lambda r: pl.BlockSpec((1, r, TOK_TILE), lambda bi, i: (bi, 0, i))
    weights_bytes = 2 * (wqkv.size + wxbc.size + wdt.size + wr.size) + 4 * qkw.size
    return pl.pallas_call(
        _inproj_kernel,
        out_shape=(jax.ShapeDtypeStruct((b, ATTN_Q_DIM, l), BF16),
                   jax.ShapeDtypeStruct((b, l, ATTN_KV_DIM), BF16),
                   jax.ShapeDtypeStruct((b, ATTN_KV_DIM, l), BF16),
                   jax.ShapeDtypeStruct((b, l, SSD_CONV_DIM), F32),
                   jax.ShapeDtypeStruct((b, l, DT_PAD), F32),
                   jax.ShapeDtypeStruct((b, l, RET_DIM), BF16),
                   jax.ShapeDtypeStruct((b, l, RET_DIM), BF16),
                   jax.ShapeDtypeStruct((b, l, RET_DIM), BF16)),
        grid=(b, nt),
        in_specs=[tok(D_MODEL),
                  pl.BlockSpec((1, 6, D_MODEL), lambda bi, i: (jnp.where(i == ctx_i, 4, bi), 0, 0)),
                  const((1, D_MODEL)),
                  const(wqkv.shape), const(qkw.shape),
                  pl.BlockSpec((ATTN_HALF, TOK_TILE), lambda bi, i: (0, i)),
                  pl.BlockSpec((ATTN_HALF, TOK_TILE), lambda bi, i: (0, i)),
                  const(wxbc.shape), const(wdt.shape), const(dtb.shape), const(wr.shape),
                  pl.BlockSpec((TOK_TILE, RET_DK), lambda bi, i: (i, 0)),
                  pl.BlockSpec((TOK_TILE, RET_DK), lambda bi, i: (i, 0))],
        out_specs=(tok_t(ATTN_Q_DIM), tok(ATTN_KV_DIM), tok_t(ATTN_KV_DIM), tok(SSD_CONV_DIM), tok(DT_PAD),
                   tok(RET_DIM), tok(RET_DIM), tok(RET_DIM)),
        compiler_params=pltpu.CompilerParams(dimension_semantics=("arbitrary", "arbitrary"),
                                             vmem_limit_bytes=_vmem_limit(weights_bytes)),
        name="inproj",
    )(h, mods, n1w, wqkv, qkw, cos_a, sin_a, wxbc, wdt, dtb, wr, cos_r, sin_r)


def _attn_kernel(qt_ref, k_ref, vt_ref, o_ref, qpad_sc, m_sc, acc_sc, s0_sc, s1_sc, mx0_sc, mx1_sc, *, nl, kc):
    kv = pl.program_id(1)
    i = pl.program_id(2)
    ctx_i = nl * TILE // ATTN_Q_TILE

    row = lax.broadcasted_iota(jnp.int32, (2 * ATTN_HEAD_DIM, ATTN_Q_TILE), 0)
    keep = (row >= ATTN_HEAD_DIM).astype(jnp.int32) == kv
    for g in range(ATTN_GROUP):
        qg = qt_ref[0, g * ATTN_HEAD_DIM:(g + 1) * ATTN_HEAD_DIM, :]
        qq = jnp.concatenate([qg, qg], axis=0)
        qpad_sc[:, g * ATTN_Q_TILE:(g + 1) * ATTN_Q_TILE] = jnp.where(keep, qq, jnp.zeros_like(qq))
    m_sc[...] = jnp.full(m_sc.shape, NEG_BIG, F32)
    acc_sc[...] = jnp.zeros(acc_sc.shape, F32)

    wide = ATTN_GROUP * ATTN_Q_TILE

    def scores(off, size, c0, width):
        return _dot(k_ref[0, pl.ds(off, size), :], qpad_sc[:, c0:c0 + width])

    def accumulate(off, size, c0, s, mx):
        cols = slice(c0, c0 + ATTN_PV_TILE)
        m_old = m_sc[:, cols]
        m_new = jnp.maximum(m_old, mx)
        m_sc[:, cols] = m_new
        p = jnp.exp2(s - m_new).astype(BF16)
        v_aug = jnp.concatenate([vt_ref[0, :, pl.ds(off, size)], jnp.ones((ATTN_ONES_ROWS, size), BF16)], axis=0)
        acc_sc[:, cols] = jnp.exp2(m_old - m_new) * acc_sc[:, cols] + _dot(v_aug, p)

    def qk(off, size, c0, s_sc, mx_sc):
        s = scores(off, size, c0, ATTN_QK_TILE)
        s_sc[:size, c0:c0 + ATTN_QK_TILE] = s
        mx_sc[:, c0:c0 + ATTN_QK_TILE] = jnp.max(s, axis=0, keepdims=True)

    def pv(off, size, c0, s_sc, mx_sc):
        accumulate(off, size, c0, s_sc[:size, c0:c0 + ATTN_PV_TILE], mx_sc[:, c0:c0 + ATTN_PV_TILE])

    bufs = ((s0_sc, mx0_sc), (s1_sc, mx1_sc))

    def step(c, off_qk, off_pv):
        for q0 in range(0, wide, ATTN_QK_TILE):
            if c < len(sizes):
                qk(off_qk, sizes[c], q0, *bufs[c % 2])
            if c >= 1:
                for c0 in range(q0, q0 + ATTN_QK_TILE, ATTN_PV_TILE):
                    pv(off_pv, sizes[c - 1], c0, *bufs[(c - 1) % 2])

    n_mid, rest = divmod(nl * TILE, kc)
    sizes = [TILE] + [kc] * n_mid + ([rest] if rest else [])
    offs = [sum(sizes[:c]) for c in range(len(sizes) + 1)]
    steady = [c for c in range(1, len(sizes)) if sizes[c] == kc and sizes[c - 1] == kc]
    pairs = len(steady) // 2
    looped = steady[:2 * pairs]

    @pl.when(i != ctx_i)
    def _():
        for c in range(len(sizes) + 1):
            if c in looped:
                if c == looped[0]:
                    def body(j, carry):
                        for d in range(2):
                            off = pl.multiple_of(offs[looped[0] + d] + 2 * j * kc, TILE)
                            step(looped[0] + d, off, off - kc)
                        return carry
                    lax.fori_loop(0, pairs, body, 0)
            else:
                step(c, offs[c], offs[c - 1] if c >= 1 else None)

    @pl.when(i == ctx_i)
    def _():
        for c0 in range(0, wide, ATTN_PV_TILE):
            s = scores(nl * TILE, TILE, c0, ATTN_PV_TILE)
            accumulate(nl * TILE, TILE, c0, s, jnp.max(s, axis=0, keepdims=True))

    acc = acc_sc[...]
    out = acc[:ATTN_HEAD_DIM] / acc[ATTN_HEAD_DIM:ATTN_HEAD_DIM + 1]
    for g in range(ATTN_GROUP):
        o_ref[0, g * ATTN_HEAD_DIM:(g + 1) * ATTN_HEAD_DIM, :] = (
            out[:, g * ATTN_Q_TILE:(g + 1) * ATTN_Q_TILE].astype(BF16))


def _attn_call(qt, k, vt, nl):
    b, _, l = qt.shape
    assert (nl * TILE) % ATTN_Q_TILE == 0
    nt = pl.cdiv(l, ATTN_Q_TILE)
    kc = ATTN_KEY_CHUNK
    gq = ATTN_GROUP * ATTN_HEAD_DIM
    wide = ATTN_GROUP * ATTN_Q_TILE
    resident = 2 * (l * ATTN_KV_DIM + ATTN_HEAD_DIM * l) + 4 * kc * wide
    return pl.pallas_call(
        functools.partial(_attn_kernel, nl=nl, kc=kc),
        out_shape=jax.ShapeDtypeStruct((b, ATTN_Q_DIM, l), BF16),
        grid=(b, ATTN_KV_HEADS, nt),
        in_specs=[pl.BlockSpec((1, gq, ATTN_Q_TILE), lambda bi, kv, i: (bi, kv, i)),
                  pl.BlockSpec((1, l, ATTN_KV_DIM), lambda bi, kv, i: (bi, 0, 0)),
                  pl.BlockSpec((1, ATTN_HEAD_DIM, l), lambda bi, kv, i: (bi, kv, 0))],
        out_specs=pl.BlockSpec((1, gq, ATTN_Q_TILE), lambda bi, kv, i: (bi, kv, i)),
        scratch_shapes=[pltpu.VMEM((2 * ATTN_HEAD_DIM, wide), BF16),
                        pltpu.VMEM((1, wide), F32),
                        pltpu.VMEM((ATTN_HEAD_DIM + ATTN_ONES_ROWS, wide), F32),
                        pltpu.VMEM((kc, wide), F32), pltpu.VMEM((kc, wide), F32),
                        pltpu.VMEM((1, wide), F32), pltpu.VMEM((1, wide), F32)],
        compiler_params=pltpu.CompilerParams(dimension_semantics=("arbitrary",) * 3,
                                             vmem_limit_bytes=_vmem_limit(resident)),
        name="attn",
    )(qt, k, vt)


def _scan_tile(s, nl, reverse):
    return jnp.where(s == 0, nl, nl - s) if reverse else jnp.where(s == 0, nl, s - 1)


def _tri(reverse, n=CHUNK):
    r = lax.broadcasted_iota(jnp.int32, (n, n), 0)
    c = lax.broadcasted_iota(jnp.int32, (n, n), 1)
    return (c >= r) if reverse else (c <= r)


def _lane_bcast(x, c):
    return jnp.broadcast_to(x[:, c:c + 1], (x.shape[0], V7X_LANES))


def _ssd_tile(t, x_ref, xp_ref, xn_ref, dt_ref, cw_ref, cb_ref, alog_ref, dskip_ref, y_ref, st_sc,
              *, nl, reverse):
    x = x_ref[0]
    row = lax.broadcasted_iota(jnp.int32, (8, SSD_CONV_DIM), 0)
    has_prev = jnp.logical_and(t != nl, t != 0)
    has_next = t < nl - 1
    prev_row = jnp.where(has_prev, xp_ref[0, 7:8, :], 0.0)
    next_row = jnp.where(has_next, xn_ref[0, 0:1, :], 0.0)
    x_m1 = pltpu.roll(x, 1, 0)
    x_m1 = jnp.concatenate([jnp.where(row == 0, prev_row, x_m1[:8]), x_m1[8:]], axis=0)
    x_p1 = pltpu.roll(x, TILE - 1, 0)
    x_p1 = jnp.concatenate([x_p1[:TILE - 8], jnp.where(row == 7, next_row, x_p1[TILE - 8:])], axis=0)
    xs = _silu(cw_ref[0:1, :] * x_m1 + cw_ref[1:2, :] * x + cw_ref[2:3, :] * x_p1 + cb_ref[...])

    tri = _tri(reverse)
    tri_b = tri.astype(BF16)
    lane = lax.broadcasted_iota(jnp.int32, (CHUNK, V7X_LANES), 1)
    left = lane < SSD_HEAD_DIM
    col0 = SSD_HEADS if reverse else 0
    a_neg = -jnp.exp(alog_ref[...]) * math.log2(math.e)
    last = 0 if reverse else CHUNK - 1

    chunks = range(TILE // CHUNK)
    for ci in (reversed(chunks) if reverse else chunks):
        r0 = ci * CHUNK
        xc = xs[r0:r0 + CHUNK]
        dtc = dt_ref[0, r0:r0 + CHUNK, :]
        a = dtc * a_neg
        hi, mid, lo = _split3(a)
        a_cum = _dot(tri_b, hi) + _dot(tri_b, mid) + _dot(tri_b, lo)
        a_cum_t = jnp.transpose(a_cum)
        for g in range(SSD_GROUPS):
            bm = xc[:, SSD_D_INNER + g * SSD_STATE:SSD_D_INNER + (g + 1) * SSD_STATE]
            cm = xc[:, SSD_D_INNER + (SSD_GROUPS + g) * SSD_STATE:SSD_D_INNER + (SSD_GROUPS + g + 1) * SSD_STATE]
            bmb = bm.astype(BF16)
            cmb = cm.astype(BF16)
            cb = jnp.where(tri, _dot_nt(cmb, bmb), 0.0)
            st_prev = st_sc[g]
            y_off = _dot(cmb, st_prev.astype(BF16))
            xdd_pairs = []
            tot = []
            for pr in range(SSD_HEADS_PER_GROUP // 2):
                h0 = g * SSD_HEADS_PER_GROUP + 2 * pr
                lanes0 = h0 * SSD_HEAD_DIM
                x2 = xc[:, lanes0:lanes0 + V7X_LANES]
                acol = [_lane_bcast(a_cum, col0 + h0 + j) for j in range(2)]
                dcol = [_lane_bcast(dtc, col0 + h0 + j) for j in range(2)]
                a2 = jnp.where(left, acol[0], acol[1])
                xd2 = x2 * jnp.where(left, dcol[0], dcol[1])
                xd2b = xd2.astype(BF16)
                y_pair = []
                for j in range(2):
                    arow = a_cum_t[col0 + h0 + j:col0 + h0 + j + 1, :]
                    lmat = jnp.exp2(jnp.minimum(acol[j] - arow, 0.0))
                    y_pair.append(_dot((cb * lmat).astype(BF16), xd2b))
                y2 = jnp.where(left, y_pair[0], y_pair[1])
                y2 = y2 + jnp.exp2(a2) * y_off[:, pr * V7X_LANES:(pr + 1) * V7X_LANES]
                if not reverse:
                    y2 = y2 + dskip_ref[:, lanes0:lanes0 + V7X_LANES] * x2
                y_ref[0, r0:r0 + CHUNK, lanes0:lanes0 + V7X_LANES] = y2
                a_tot = a2[last:last + 1, :]
                xdd_pairs.append((xd2 * jnp.exp2(a_tot - a2)).astype(BF16))
                tot.append(a_tot)
            xdd = jnp.concatenate(xdd_pairs, axis=1)
            st_sc[g] = jnp.exp2(jnp.concatenate(tot, axis=1)) * st_prev + _dot_tn(bmb, xdd)


def _ssd_kernel(xf_ref, xpf_ref, xnf_ref, dtf_ref, xb_ref, xpb_ref, xnb_ref, dtb_ref, cw_ref, cb_ref, alog_ref,
                dskip_ref, yf_ref, yb_ref, stf_sc, stb_sc, *, nl):
    s = pl.program_id(1)

    @pl.when(s == 0)
    def _():
        stf_sc[...] = jnp.zeros(stf_sc.shape, F32)
        stb_sc[...] = jnp.zeros(stb_sc.shape, F32)

    _ssd_tile(_scan_tile(s, nl, False), xf_ref, xpf_ref, xnf_ref, dtf_ref, cw_ref, cb_ref, alog_ref, dskip_ref,
              yf_ref, stf_sc, nl=nl, reverse=False)
    _ssd_tile(_scan_tile(s, nl, True), xb_ref, xpb_ref, xnb_ref, dtb_ref, cw_ref, cb_ref, alog_ref, dskip_ref,
              yb_ref, stb_sc, nl=nl, reverse=True)


def _ssd_call(xbc, dt, cw, cb, alog, dskip, nl):
    b, l, _ = xbc.shape
    nt = l // TILE
    rows8 = TILE // 8
    const = lambda shape: pl.BlockSpec(shape, lambda bi, s: (0,) * len(shape))

    def stream(reverse):
        tile_of = lambda s: _scan_tile(s, nl, reverse)
        return [pl.BlockSpec((1, TILE, SSD_CONV_DIM), lambda bi, s: (bi, tile_of(s), 0)),
                pl.BlockSpec((1, 8, SSD_CONV_DIM), lambda bi, s: (bi, jnp.maximum(tile_of(s) * rows8 - 1, 0), 0)),
                pl.BlockSpec((1, 8, SSD_CONV_DIM),
                             lambda bi, s: (bi, jnp.minimum((tile_of(s) + 1) * rows8, nt * rows8 - 1), 0)),
                pl.BlockSpec((1, TILE, DT_PAD), lambda bi, s: (bi, tile_of(s), 0))]

    out = lambda reverse: pl.BlockSpec((1, TILE, SSD_D_INNER), lambda bi, s: (bi, _scan_tile(s, nl, reverse), 0))
    state = pltpu.VMEM((SSD_GROUPS, SSD_STATE, SSD_HEADS_PER_GROUP * SSD_HEAD_DIM), F32)
    y = jax.ShapeDtypeStruct((b, l, SSD_D_INNER), F32)
    return pl.pallas_call(
        functools.partial(_ssd_kernel, nl=nl),
        out_shape=(y, y),
        grid=(b, nt),
        in_specs=stream(False) + stream(True) + [const(cw.shape), const(cb.shape), const(alog.shape),
                                                 const(dskip.shape)],
        out_specs=(out(False), out(True)),
        scratch_shapes=[state, state],
        compiler_params=pltpu.CompilerParams(dimension_semantics=("arbitrary", "arbitrary")),
        name="ssd",
    )(xbc, xbc, xbc, dt, xbc, xbc, xbc, dt, cw, cb, alog, dskip)


def _ret_head(hd, q_ref, k_ref, v_ref, ld_ref, y_ref, st_sc, *, reverse):
    n = RET_SCAN_CHUNK
    tri = _tri(reverse, n)
    r = lax.broadcasted_iota(jnp.int32, (n, n), 0)
    c = lax.broadcasted_iota(jnp.int32, (n, n), 1)
    dist = jnp.maximum((c - r) if reverse else (r - c), 0).astype(F32)
    rk = lax.broadcasted_iota(jnp.int32, (n, RET_DK), 0)
    pos = ((n - 1 - rk) if reverse else rk).astype(F32)
    col0 = RET_HEADS if reverse else 0

    c0 = hd * RET_DK
    log_g = -jnp.exp(ld_ref[0:1, col0 + hd:col0 + hd + 1])
    q = q_ref[0, :, c0:c0 + RET_DK]
    k = k_ref[0, :, c0:c0 + RET_DK]
    v = v_ref[0, :, c0:c0 + RET_DV]
    st_prev = st_sc[hd]
    sc = _dot_nt(q, k) * jnp.where(tri, jnp.exp(dist * jnp.broadcast_to(log_g, (n, n))), 0.0)
    lg = jnp.broadcast_to(log_g, (n, RET_DK))
    q_dec = (q.astype(F32) * jnp.exp((pos + 1.0) * lg)).astype(BF16)
    y_ref[0, :, c0:c0 + RET_DV] = _dot(jnp.concatenate([sc.astype(BF16), q_dec], axis=1),
                                       jnp.concatenate([v, st_prev.astype(BF16)], axis=0))
    k_dec = (k.astype(F32) * jnp.exp((n - 1.0 - pos) * lg)).astype(BF16)
    st_sc[hd] = jnp.exp(n * jnp.broadcast_to(log_g, (RET_DK, RET_DV))) * st_prev + _dot_tn(k_dec, v)


def _ret_kernel(qf_ref, kf_ref, vf_ref, qb_ref, kb_ref, vb_ref, ld_ref, yf_ref, yb_ref, stf_sc, stb_sc):
    @pl.when(pl.program_id(1) == 0)
    def _():
        stf_sc[...] = jnp.zeros(stf_sc.shape, F32)
        stb_sc[...] = jnp.zeros(stb_sc.shape, F32)

    for hd in range(RET_HEADS):
        _ret_head(hd, qf_ref, kf_ref, vf_ref, ld_ref, yf_ref, stf_sc, reverse=False)
        _ret_head(hd, qb_ref, kb_ref, vb_ref, ld_ref, yb_ref, stb_sc, reverse=True)


def _ret_call(rq, rk, rv, ld, nl):
    b, l, _ = rq.shape
    nt = l // TILE
    tok = lambda reverse: pl.BlockSpec((1, TILE, RET_DIM), lambda bi, s: (bi, _scan_tile(s, nl, reverse), 0))
    state = pltpu.VMEM((RET_HEADS, RET_DK, RET_DV), F32)
    y = jax.ShapeDtypeStruct((b, l, RET_DIM), F32)
    return pl.pallas_call(
        _ret_kernel,
        out_shape=(y, y),
        grid=(b, nt),
        in_specs=[tok(False)] * 3 + [tok(True)] * 3 + [pl.BlockSpec(ld.shape, lambda bi, s: (0, 0))],
        out_specs=(tok(False), tok(True)),
        scratch_shapes=[state, state],
        compiler_params=pltpu.CompilerParams(dimension_semantics=("arbitrary", "arbitrary")),
        name="ret",
    )(rq, rk, rv, rq, rk, rv, ld)


def _merge_kernel(h_ref, mod_ref, n1w_ref, ot_ref, sf_ref, sb_ref, rf_ref, rb_ref, wzg_ref, wgate_ref,
                  snw_ref, gnw_ref, wb_ref, wout_ref, o_ref):
    h = h_ref[0]
    u = _rms_rows(h, n1w_ref[...]) * (1.0 + mod_ref[0, 1:2, :]) + mod_ref[0, 0:1, :]
    ub = u.astype(BF16)
    zg = _dot(ub, wzg_ref[...])
    gates = _sigmoid(_dot(ub, wgate_ref[...]))

    br_attn = _dot_tn(ot_ref[0], wb_ref[0])

    y = (sf_ref[0] + sb_ref[0]) * _silu(zg[:, :SSD_D_INNER])
    br_ssd = _dot(_rms_rows(y, snw_ref[...]).astype(BF16), wb_ref[1])

    yr = rf_ref[0] + rb_ref[0]
    heads = []
    for hd in range(RET_HEADS):
        yh = yr[:, hd * RET_DV:(hd + 1) * RET_DV]
        yc = yh - jnp.mean(yh, axis=-1, keepdims=True)
        heads.append(yc * lax.rsqrt(jnp.mean(yc * yc, axis=-1, keepdims=True) + NORM_EPS))
    yn = jnp.concatenate(heads, axis=1) * gnw_ref[...] * _silu(zg[:, SSD_D_INNER:])
    br_ret = _dot(yn.astype(BF16), wb_ref[2])

    merged = (gates[:, :D_MODEL] * br_attn + gates[:, D_MODEL:2 * D_MODEL] * br_ssd
              + gates[:, 2 * D_MODEL:] * br_ret)
    o_ref[0] = h + mod_ref[0, 2:3, :] * _dot(merged.astype(BF16), wout_ref[...])


def _merge_call(h, mods, n1w, ot, sf, sb, rf, rb, wts, nl):
    b, l, _ = h.shape
    nt, ctx_i = _tok_tiles(l, nl)
    wzg, wgate, snw, gnw, wb, wout = wts
    const = _resident
    tok = lambda c: pl.BlockSpec((1, TOK_TILE, c), lambda bi, i: (bi, i, 0))
    weights_bytes = 2 * (wzg.size + wgate.size + wb.size + wout.size)
    return pl.pallas_call(
        _merge_kernel,
        out_shape=jax.ShapeDtypeStruct((b, l, D_MODEL), F32),
        grid=(b, nt),
        in_specs=[tok(D_MODEL),
                  pl.BlockSpec((1, 6, D_MODEL), lambda bi, i: (jnp.where(i == ctx_i, 4, bi), 0, 0)),
                  const((1, D_MODEL)),
                  pl.BlockSpec((1, ATTN_Q_DIM, TOK_TILE), lambda bi, i: (bi, 0, i)),
                  tok(SSD_D_INNER), tok(SSD_D_INNER), tok(RET_DIM), tok(RET_DIM),
                  const(wzg.shape), const(wgate.shape), const(snw.shape), const(gnw.shape),
                  const(wb.shape), const(wout.shape)],
        out_specs=tok(D_MODEL),
        compiler_params=pltpu.CompilerParams(dimension_semantics=("arbitrary", "arbitrary"),
                                             vmem_limit_bytes=_vmem_limit(weights_bytes)),
        name="merge",
    )(h, mods, n1w, ot, sf, sb, rf, rb, wzg, wgate, snw, gnw, wb, wout)


def _mlp_kernel(h_ref, mod_ref, n2w_ref, w1_ref, w2_ref, fw_ref, o_ref, *, final):
    h = h_ref[0]
    v = _rms_rows(h, n2w_ref[...]) * (1.0 + mod_ref[0, 4:5, :]) + mod_ref[0, 3:4, :]
    a = jnp.maximum(_dot(v.astype(BF16), w1_ref[...]), 0.0)
    out = h + mod_ref[0, 5:6, :] * _dot((a * a).astype(BF16), w2_ref[...])
    o_ref[0] = _rms_rows(out, fw_ref[...]) if final else out


def _mlp_call(h, mods, n2w, w1, w2, fw, nl, final):
    b, l, _ = h.shape
    nt, ctx_i = _tok_tiles(l, nl)
    if final:
        nt, l = ctx_i, nl * TILE
    const = _resident
    tok = pl.BlockSpec((1, TOK_TILE, D_MODEL), lambda bi, i: (bi, i, 0))
    return pl.pallas_call(
        functools.partial(_mlp_kernel, final=final),
        out_shape=jax.ShapeDtypeStruct((b, l, D_MODEL), F32),
        grid=(b, nt),
        in_specs=[tok,
                  pl.BlockSpec((1, 6, D_MODEL), lambda bi, i: (jnp.where(i == ctx_i, 4, bi), 0, 0)),
                  const((1, D_MODEL)), const(w1.shape), const(w2.shape), const((1, D_MODEL))],
        out_specs=tok,
        compiler_params=pltpu.CompilerParams(dimension_semantics=("arbitrary", "arbitrary"),
                                             vmem_limit_bytes=_vmem_limit(2 * (w1.size + w2.size))),
        name="mlp_final" if final else "mlp",
    )(h, mods, n2w, w1, w2, fw)


def _rope_tables(n, m):
    rows = n // GRID_W
    row = jnp.repeat(jnp.arange(rows, dtype=F32), GRID_W)
    col = jnp.tile(jnp.arange(GRID_W, dtype=F32), rows)
    inv = ROPE_THETA ** (-jnp.arange(ATTN_AXIS_FREQS, dtype=F32) / ATTN_AXIS_FREQS)
    ang = jnp.concatenate([row[:, None] * inv, col[:, None] * inv], axis=-1)
    cos_a = jnp.concatenate([jnp.cos(ang), jnp.ones((m, ATTN_HALF), F32)], axis=0).T
    sin_a = jnp.concatenate([jnp.sin(ang), jnp.zeros((m, ATTN_HALF), F32)], axis=0).T
    pos = jnp.concatenate([jnp.arange(n, dtype=F32) + m, jnp.arange(m, dtype=F32)])
    inv_r = ROPE_THETA ** (-jnp.linspace(0.0, 1.0, RET_DK // 2, dtype=F32))
    ang_r = pos[:, None] * inv_r
    cos_r = jnp.concatenate([jnp.cos(ang_r), jnp.cos(ang_r)], axis=-1)
    sin_r = jnp.concatenate([-jnp.sin(ang_r), jnp.sin(ang_r)], axis=-1)
    return cos_a, sin_a, cos_r, sin_r


def _layer_weights(w_in, q_norm, k_norm, dt_bias):
    offs = [0]
    for sz in IN_SPLITS:
        offs.append(offs[-1] + sz)
    col = lambda j: w_in[:, offs[j]:offs[j + 1]]
    wqkv = jnp.concatenate([col(0), col(1), col(2)], axis=1).T.astype(BF16)
    qkw = jnp.concatenate([jnp.tile(q_norm, ATTN_HEADS), jnp.tile(k_norm, ATTN_KV_HEADS)])
    qkw = jnp.broadcast_to(qkw[:, None], (ATTN_Q_DIM + ATTN_KV_DIM, TOK_TILE)).astype(F32)
    wxbc = col(4).astype(BF16)
    wdt = jnp.pad(col(5), ((0, 0), (0, DT_PAD - 2 * SSD_HEADS))).astype(BF16)
    dtb = jnp.pad(dt_bias.reshape(1, -1), ((0, 0), (0, DT_PAD - 2 * SSD_HEADS))).astype(F32)
    wr = jnp.concatenate([col(6), col(7), col(8)], axis=1).astype(BF16)
    wzg = jnp.concatenate([col(3), col(9)], axis=1).astype(BF16)
    wgate = col(10).astype(BF16)
    return (wqkv, qkw, wxbc, wdt, dtb, wr), (wzg, wgate)


def _pad_lanes(v):
    v = v.reshape(1, -1).astype(F32)
    return jnp.pad(v, ((0, 0), (0, V7X_LANES - v.shape[1])))


def kernel(x, c, ctx, c_ctx, w_mod, b_mod, norm1_w, norm2_w, w_in, attn_q_norm, attn_k_norm, ssd_conv_w,
           ssd_conv_b, ssd_dt_bias, ssd_a_log, ssd_d, ssd_norm_w, ret_log_decay, ret_gn_w, w_branch, w_out,
           w_mlp1, w_mlp2, final_norm_w):
    b, n, d = x.shape
    m = ctx.shape[1]
    depth = w_in.shape[0]
    assert d == D_MODEL and m == TILE and n % TILE == 0 and n % GRID_W == 0 and b <= 4
    nl = n // TILE

    tabs = _rope_tables(n, m)
    cc = jnp.zeros((8, D_MODEL), F32).at[:b].set(c).at[4].set(c_ctx)
    h = jnp.concatenate([x, ctx], axis=1)

    for layer in range(depth):
        final = layer == depth - 1
        in_w, (wzg, wgate) = _layer_weights(w_in[layer], attn_q_norm[layer], attn_k_norm[layer],
                                            ssd_dt_bias[layer])
        n1w = norm1_w[layer].reshape(1, -1)
        mods = _mod_call(cc, w_mod[layer], b_mod[layer])

        qt, k, vt, xbc, dt, rq, rk, rv = _inproj_call(h, mods, n1w, in_w, tabs, nl)
        ot = _attn_call(qt, k, vt, nl)

        cw = jnp.pad(ssd_conv_w[layer], ((0, 8 - SSD_CONV_K), (0, 0)))
        cb = ssd_conv_b[layer].reshape(1, -1)
        alog = _pad_lanes(ssd_a_log[layer])
        dskip = jnp.repeat(ssd_d[layer], SSD_HEAD_DIM).reshape(1, -1)
        sf, sb = _ssd_call(xbc, dt, cw, cb, alog, dskip, nl)

        ld = _pad_lanes(ret_log_decay[layer])
        rf, rb = _ret_call(rq, rk, rv, ld, nl)

        merge_w = (wzg, wgate, ssd_norm_w[layer].reshape(1, -1), ret_gn_w[layer].reshape(1, -1),
                   w_branch[layer].astype(BF16), w_out[layer].astype(BF16))
        h = _merge_call(h, mods, n1w, ot, sf, sb, rf, rb, merge_w, nl)
        h = _mlp_call(h, mods, norm2_w[layer].reshape(1, -1), w_mlp1[layer].astype(BF16),
                      w_mlp2[layer].astype(BF16), final_norm_w.reshape(1, -1), nl, final)
    return h
```

```python
import functools
import math

import jax
import jax.numpy as jnp
from jax import lax
from jax.experimental import pallas as pl
from jax.experimental.pallas import tpu as pltpu

F32 = jnp.float32
BF16 = jnp.bfloat16

D_MODEL = 1024
GRID_W = 64
NORM_EPS = 1e-6
ROPE_THETA = 10000.0

ATTN_HEADS = 8
ATTN_KV_HEADS = 2
ATTN_GROUP = ATTN_HEADS // ATTN_KV_HEADS
ATTN_HEAD_DIM = 64
ATTN_HALF = ATTN_HEAD_DIM // 2
ATTN_AXIS_FREQS = ATTN_HEAD_DIM // 4
ATTN_Q_DIM = ATTN_HEADS * ATTN_HEAD_DIM
ATTN_KV_DIM = ATTN_KV_HEADS * ATTN_HEAD_DIM

SSD_HEADS = 8
SSD_HEAD_DIM = 64
SSD_D_INNER = SSD_HEADS * SSD_HEAD_DIM
SSD_GROUPS = 2
SSD_STATE = 128
SSD_CONV_K = 3
SSD_CONV_DIM = SSD_D_INNER + 2 * SSD_GROUPS * SSD_STATE
SSD_HEADS_PER_GROUP = SSD_HEADS // SSD_GROUPS

RET_HEADS = 4
RET_DK = 128
RET_DV = 128
RET_DIM = RET_HEADS * RET_DK

N_BRANCH = 3
BRANCH_W = 512
MLP_HIDDEN = 4 * D_MODEL

IN_SPLITS = (ATTN_Q_DIM, ATTN_KV_DIM, ATTN_KV_DIM, SSD_D_INNER, SSD_CONV_DIM, 2 * SSD_HEADS,
             RET_DIM, RET_DIM, RET_HEADS * RET_DV, RET_HEADS * RET_DV, N_BRANCH * D_MODEL)

V7X_LANES = 128
V7X_VMEM_BYTES = 64 * 1024 * 1024

TILE = 256
CHUNK = 128
RET_SCAN_CHUNK = TILE
TOK_TILE = 512
DT_PAD = V7X_LANES
NEG_BIG = -1e30
ATTN_KEY_CHUNK = 768
ATTN_Q_TILE = 1024
ATTN_QK_TILE = 256
ATTN_PV_TILE = 256
ATTN_ONES_ROWS = 16


def _resident(shape):
    return pl.BlockSpec(shape, lambda bi, i: (0,) * len(shape), pipeline_mode=pl.Buffered(1))


def _tok_tiles(l, nl):
    assert (nl * TILE) % TOK_TILE == 0
    return pl.cdiv(l, TOK_TILE), nl * TILE // TOK_TILE


def _vmem_limit(resident_bytes):
    return int(min(V7X_VMEM_BYTES - 8 * 1024 * 1024, 2 * resident_bytes + 16 * 1024 * 1024))


def _sigmoid(x):
    return 0.5 * jnp.tanh(0.5 * x) + 0.5


def _silu(x):
    h = 0.5 * x
    return h + h * jnp.tanh(h)


def _rms_rows(x, w):
    return x * lax.rsqrt(jnp.mean(x * x, axis=-1, keepdims=True) + NORM_EPS) * w


def _dot(a, b):
    return jnp.dot(a, b, preferred_element_type=F32)


def _dot_nt(a, b):
    return lax.dot_general(a, b, (((1,), (1,)), ((), ())), preferred_element_type=F32)


def _dot_tn(a, b):
    return lax.dot_general(a, b, (((0,), (0,)), ((), ())), preferred_element_type=F32)


def _split3(x):
    hi = x.astype(BF16)
    r1 = x - hi.astype(F32)
    mid = r1.astype(BF16)
    lo = (r1 - mid.astype(F32)).astype(BF16)
    return hi, mid, lo


def _mod_kernel(c_ref, w_ref, b_ref, o_ref):
    s = _silu(c_ref[...])
    o_ref[...] = jnp.dot(s, w_ref[...], preferred_element_type=F32,
                         precision=lax.Precision.HIGHEST) + b_ref[...]


def _mod_call(cc, w_mod, b_mod):
    nblk = w_mod.shape[1] // D_MODEL
    out = pl.pallas_call(
        _mod_kernel,
        out_shape=jax.ShapeDtypeStruct((8, nblk * D_MODEL), F32),
        grid=(nblk,),
        in_specs=[pl.BlockSpec((8, D_MODEL), lambda j: (0, 0)),
                  pl.BlockSpec((D_MODEL, D_MODEL), lambda j: (0, j)),
                  pl.BlockSpec((1, D_MODEL), lambda j: (0, j))],
        out_specs=pl.BlockSpec((8, D_MODEL), lambda j: (0, j)),
        compiler_params=pltpu.CompilerParams(dimension_semantics=("arbitrary",)),
        name="mod",
    )(cc, w_mod, b_mod.reshape(1, -1))
    return out.reshape(8, nblk, D_MODEL)


def _inproj_kernel(h_ref, mod_ref, n1w_ref, wqkv_ref, qkw_ref, cos_ref, sin_ref, wxbc_ref, wdt_ref,
                   dtb_ref, wr_ref, cosr_ref, sinr_ref,
                   qt_ref, k_ref, vt_ref, xbc_ref, dt_ref, rq_ref, rk_ref, rv_ref):
    h = h_ref[0]
    u = _rms_rows(h, n1w_ref[...]) * (1.0 + mod_ref[0, 1:2, :]) + mod_ref[0, 0:1, :]
    ub = u.astype(BF16)

    qkv_t = _dot_nt(wqkv_ref[...], ub)
    cos = cos_ref[...]
    sin = sin_ref[...]
    k_rows = []
    for hd in range(ATTN_HEADS + ATTN_KV_HEADS):
        r0 = hd * ATTN_HEAD_DIM
        xh = qkv_t[r0:r0 + ATTN_HEAD_DIM]
        yh = xh * lax.rsqrt(jnp.mean(xh * xh, axis=0, keepdims=True) + NORM_EPS) * qkw_ref[r0:r0 + ATTN_HEAD_DIM, :]
        y1 = yh[:ATTN_HALF]
        y2 = yh[ATTN_HALF:]
        o1 = y1 * cos - y2 * sin
        o2 = y1 * sin + y2 * cos
        if hd < ATTN_HEADS:
            scale = ATTN_HEAD_DIM ** -0.5 * math.log2(math.e)
            qt_ref[0, r0:r0 + ATTN_HALF, :] = (o1 * scale).astype(BF16)
            qt_ref[0, r0 + ATTN_HALF:r0 + ATTN_HEAD_DIM, :] = (o2 * scale).astype(BF16)
        else:
            k_rows += [o1, o2]
    k_t = jnp.concatenate(k_rows, axis=0)
    k_ref[0] = jnp.transpose(k_t).astype(BF16)
    vt_ref[0] = qkv_t[ATTN_Q_DIM + ATTN_KV_DIM:].astype(BF16)

    xbc_ref[0] = _dot(ub, wxbc_ref[...])
    dt_raw = _dot(ub, wdt_ref[...]) + dtb_ref[...]
    dt_ref[0] = jnp.maximum(dt_raw, 0.0) + jnp.log1p(jnp.exp(-jnp.abs(dt_raw)))

    r = _dot(ub, wr_ref[...])
    cosr = cosr_ref[...]
    sinr = sinr_ref[...]
    for hd in range(RET_HEADS):
        c0 = hd * RET_DK
        qh = r[:, c0:c0 + RET_DK]
        kh = r[:, RET_DIM + c0:RET_DIM + c0 + RET_DK]
        rq_ref[0, :, c0:c0 + RET_DK] = (qh * cosr + pltpu.roll(qh, RET_DK // 2, 1) * sinr).astype(BF16)
        rk_ref[0, :, c0:c0 + RET_DK] = ((kh * cosr + pltpu.roll(kh, RET_DK // 2, 1) * sinr)
                                        * (RET_DK ** -0.5)).astype(BF16)
    rv_ref[0] = r[:, 2 * RET_DIM:].astype(BF16)


def _inproj_call(h, mods, n1w, wts, tabs, nl):
    b, l, _ = h.shape
    nt, ctx_i = _tok_tiles(l, nl)
    wqkv, qkw, wxbc, wdt, dtb, wr = wts
    cos_a, sin_a, cos_r, sin_r = tabs
    const = _resident
    tok = lambda c: pl.BlockSpec((1, TOK_TILE, c), lambda bi, i: (bi, i, 0))
    tok_t = lambda r: pl.BlockSpec((1, r, TOK_TILE), lambda bi, i: (bi, 0, i))
    weights_bytes = 2 * (wqkv.size + wxbc.size + wdt.size + wr.size) + 4 * qkw.size
    return pl.pallas_call(
        _inproj_kernel,
        out_shape=(jax.ShapeDtypeStruct((b, ATTN_Q_DIM, l), BF16),
                   jax.ShapeDtypeStruct((b, l, ATTN_KV_DIM), BF16),
                   jax.ShapeDtypeStruct((b, ATTN_KV_DIM, l), BF16),
                   jax.ShapeDtypeStruct((b, l, SSD_CONV_DIM), F32),
                   jax.ShapeDtypeStruct((b, l, DT_PAD), F32),
                   jax.ShapeDtypeStruct((b, l, RET_DIM), BF16),
                   jax.ShapeDtypeStruct((b, l, RET_DIM), BF16),
                   jax.ShapeDtypeStruct((b, l, RET_DIM), BF16)),
        grid=(b, nt),
        in_specs=[tok(D_MODEL),
                  pl.BlockSpec((1, 6, D_MODEL), lambda bi, i: (jnp.where(i == ctx_i, 4, bi), 0, 0)),
                  const((1, D_MODEL)),
                  const(wqkv.shape), const(qkw.shape),
                  pl.BlockSpec((ATTN_HALF, TOK_TILE), lambda bi, i: (0, i)),
                  pl.BlockSpec((ATTN_HALF, TOK_TILE), lambda bi, i: (0, i)),
                  const(wxbc.shape), const(wdt.shape), const(dtb.shape), const(wr.shape),
                  pl.BlockSpec((TOK_TILE, RET_DK), lambda bi, i: (i, 0)),
                  pl.BlockSpec((TOK_TILE, RET_DK), lambda bi, i: (i, 0))],
        out_specs=(tok_t(ATTN_Q_DIM), tok(ATTN_KV_DIM), tok_t(ATTN_KV_DIM), tok(SSD_CONV_DIM), tok(DT_PAD),
                   tok(RET_DIM), tok(RET_DIM), tok(RET_DIM)),
        compiler_params=pltpu.CompilerParams(dimension_semantics=("arbitrary", "arbitrary"),
                                             vmem_limit_bytes=_vmem_limit(weights_bytes)),
        name="inproj",
    )(h, mods, n1w, wqkv, qkw, cos_a, sin_a, wxbc, wdt, dtb, wr, cos_r, sin_r)


def _attn_kernel(qt_ref, k_ref, vt_ref, o_ref, qpad_sc, m_sc, acc_sc, s0_sc, s1_sc, mx0_sc, mx1_sc, *, nl, kc):
    kv = pl.program_id(1)
    i = pl.program_id(2)
    ctx_i = nl * TILE // ATTN_Q_TILE

    row = lax.broadcasted_iota(jnp.int32, (2 * ATTN_HEAD_DIM, ATTN_Q_TILE), 0)
    keep = (row >= ATTN_HEAD_DIM).astype(jnp.int32) == kv
    for g in range(ATTN_GROUP):
        qg = qt_ref[0, g * ATTN_HEAD_DIM:(g + 1) * ATTN_HEAD_DIM, :]
        qq = jnp.concatenate([qg, qg], axis=0)
        qpad_sc[:, g * ATTN_Q_TILE:(g + 1) * ATTN_Q_TILE] = jnp.where(keep, qq, jnp.zeros_like(qq))
    m_sc[...] = jnp.full(m_sc.shape, NEG_BIG, F32)
    acc_sc[...] = jnp.zeros(acc_sc.shape, F32)

    wide = ATTN_GROUP * ATTN_Q_TILE

    def scores(off, size, c0, width):
        return _dot(k_ref[0, pl.ds(off, size), :], qpad_sc[:, c0:c0 + width])

    def accumulate(off, size, c0, s, mx):
        cols = slice(c0, c0 + ATTN_PV_TILE)
        m_old = m_sc[:, cols]
        m_new = jnp.maximum(m_old, mx)
        m_sc[:, cols] = m_new
        p = jnp.exp2(s - m_new).astype(BF16)
        v_aug = jnp.concatenate([vt_ref[0, :, pl.ds(off, size)], jnp.ones((ATTN_ONES_ROWS, size), BF16)], axis=0)
        acc_sc[:, cols] = jnp.exp2(m_old - m_new) * acc_sc[:, cols] + _dot(v_aug, p)

    def qk(off, size, c0, s_sc, mx_sc):
        s = scores(off, size, c0, ATTN_QK_TILE)
        s_sc[:size, c0:c0 + ATTN_QK_TILE] = s
        mx_sc[:, c0:c0 + ATTN_QK_TILE] = jnp.max(s, axis=0, keepdims=True)

    def pv(off, size, c0, s_sc, mx_sc):
        accumulate(off, size, c0, s_sc[:size, c0:c0 + ATTN_PV_TILE], mx_sc[:, c0:c0 + ATTN_PV_TILE])

    bufs = ((s0_sc, mx0_sc), (s1_sc, mx1_sc))

    def step(c, off_qk, off_pv):
        for q0 in range(0, wide, ATTN_QK_TILE):
            if c < len(sizes):
                qk(off_qk, sizes[c], q0, *bufs[c % 2])
            if c >= 1:
                for c0 in range(q0, q0 + ATTN_QK_TILE, ATTN_PV_TILE):
                    pv(off_pv, sizes[c - 1], c0, *bufs[(c - 1) % 2])

    n_mid, rest = divmod(nl * TILE, kc)
    sizes = [TILE] + [kc] * n_mid + ([rest] if rest else [])
    offs = [sum(sizes[:c]) for c in range(len(sizes) + 1)]
    steady = [c for c in range(1, len(sizes)) if sizes[c] == kc and sizes[c - 1] == kc]
    pairs = len(steady) // 2
    looped = steady[:2 * pairs]

    @pl.when(i != ctx_i)
    def _():
        for c in range(len(sizes) + 1):
            if c in looped:
                if c == looped[0]:
                    def body(j, carry):
                        for d in range(2):
                            off = pl.multiple_of(offs[looped[0] + d] + 2 * j * kc, TILE)
                            step(looped[0] + d, off, off - kc)
                        return carry
                    lax.fori_loop(0, pairs, body, 0)
            else:
                step(c, offs[c], offs[c - 1] if c >= 1 else None)

    @pl.when(i == ctx_i)
    def _():
        for c0 in range(0, wide, ATTN_PV_TILE):
            s = scores(nl * TILE, TILE, c0, ATTN_PV_TILE)
            accumulate(nl * TILE, TILE, c0, s, jnp.max(s, axis=0, keepdims=True))

    acc = acc_sc[...]
    out = acc[:ATTN_HEAD_DIM] / acc[ATTN_HEAD_DIM:ATTN_HEAD_DIM + 1]
    for g in range(ATTN_GROUP):
        o_ref[0, g * ATTN_HEAD_DIM:(g + 1) * ATTN_HEAD_DIM, :] = (
            out[:, g * ATTN_Q_TILE:(g + 1) * ATTN_Q_TILE].astype(BF16))


def _attn_call(qt, k, vt, nl):
    b, _, l = qt.shape
    assert (nl * TILE) % ATTN_Q_TILE == 0
    nt = pl.cdiv(l, ATTN_Q_TILE)
    kc = ATTN_KEY_CHUNK
    gq = ATTN_GROUP * ATTN_HEAD_DIM
    wide = ATTN_GROUP * ATTN_Q_TILE
    resident = 2 * (l * ATTN_KV_DIM + ATTN_HEAD_DIM * l) + 4 * kc * wide
    return pl.pallas_call(
        functools.partial(_attn_kernel, nl=nl, kc=kc),
        out_shape=jax.ShapeDtypeStruct((b, ATTN_Q_DIM, l), BF16),
        grid=(b, ATTN_KV_HEADS, nt),
        in_specs=[pl.BlockSpec((1, gq, ATTN_Q_TILE), lambda bi, kv, i: (bi, kv, i)),
                  pl.BlockSpec((1, l, ATTN_KV_DIM), lambda bi, kv, i: (bi, 0, 0)),
                  pl.BlockSpec((1, ATTN_HEAD_DIM, l), lambda bi, kv, i: (bi, kv, 0))],
        out_specs=pl.BlockSpec((1, gq, ATTN_Q_TILE), lambda bi, kv, i: (bi, kv, i)),
        scratch_shapes=[pltpu.VMEM((2 * ATTN_HEAD_DIM, wide), BF16),
                        pltpu.VMEM((1, wide), F32),
                        pltpu.VMEM((ATTN_HEAD_DIM + ATTN_ONES_ROWS, wide), F32),
                        pltpu.VMEM((kc, wide), F32), pltpu.VMEM((kc, wide), F32),
                        pltpu.VMEM((1, wide), F32), pltpu.VMEM((1, wide), F32)],
        compiler_params=pltpu.CompilerParams(dimension_semantics=("arbitrary",) * 3,
                                             vmem_limit_bytes=_vmem_limit(resident)),
        name="attn",
    )(qt, k, vt)


def _scan_tile(s, nl, reverse):
    return jnp.where(s == 0, nl, nl - s) if reverse else jnp.where(s == 0, nl, s - 1)


def _tri(reverse, n=CHUNK):
    r = lax.broadcasted_iota(jnp.int32, (n, n), 0)
    c = lax.broadcasted_iota(jnp.int32, (n, n), 1)
    return (c >= r) if reverse else (c <= r)


def _lane_bcast(x, c):
    return jnp.broadcast_to(x[:, c:c + 1], (x.shape[0], V7X_LANES))


def _ssd_tile(t, x_ref, xp_ref, xn_ref, dt_ref, cw_ref, cb_ref, alog_ref, dskip_ref, y_ref, st_sc,
              *, nl, reverse):
    x = x_ref[0]
    row = lax.broadcasted_iota(jnp.int32, (8, SSD_CONV_DIM), 0)
    has_prev = jnp.logical_and(t != nl, t != 0)
    has_next = t < nl - 1
    prev_row = jnp.where(has_prev, xp_ref[0, 7:8, :], 0.0)
    next_row = jnp.where(has_next, xn_ref[0, 0:1, :], 0.0)
    x_m1 = pltpu.roll(x, 1, 0)
    x_m1 = jnp.concatenate([jnp.where(row == 0, prev_row, x_m1[:8]), x_m1[8:]], axis=0)
    x_p1 = pltpu.roll(x, TILE - 1, 0)
    x_p1 = jnp.concatenate([x_p1[:TILE - 8], jnp.where(row == 7, next_row, x_p1[TILE - 8:])], axis=0)
    xs = _silu(cw_ref[0:1, :] * x_m1 + cw_ref[1:2, :] * x + cw_ref[2:3, :] * x_p1 + cb_ref[...])

    tri = _tri(reverse)
    tri_b = tri.astype(BF16)
    lane = lax.broadcasted_iota(jnp.int32, (CHUNK, V7X_LANES), 1)
    left = lane < SSD_HEAD_DIM
    col0 = SSD_HEADS if reverse else 0
    a_neg = -jnp.exp(alog_ref[...]) * math.log2(math.e)
    last = 0 if reverse else CHUNK - 1

    chunks = range(TILE // CHUNK)
    for ci in (reversed(chunks) if reverse else chunks):
        r0 = ci * CHUNK
        xc = xs[r0:r0 + CHUNK]
        dtc = dt_ref[0, r0:r0 + CHUNK, :]
        a = dtc * a_neg
        hi, mid, lo = _split3(a)
        a_cum = _dot(tri_b, hi) + _dot(tri_b, mid) + _dot(tri_b, lo)
        a_cum_t = jnp.transpose(a_cum)
        for g in range(SSD_GROUPS):
            bm = xc[:, SSD_D_INNER + g * SSD_STATE:SSD_D_INNER + (g + 1) * SSD_STATE]
            cm = xc[:, SSD_D_INNER + (SSD_GROUPS + g) * SSD_STATE:SSD_D_INNER + (SSD_GROUPS + g + 1) * SSD_STATE]
            bmb = bm.astype(BF16)
            cmb = cm.astype(BF16)
            cb = jnp.where(tri, _dot_nt(cmb, bmb), 0.0)
            st_prev = st_sc[g]
            y_off = _dot(cmb, st_prev.astype(BF16))
            xdd_pairs = []
            tot = []
            for pr in range(SSD_HEADS_PER_GROUP // 2):
                h0 = g * SSD_HEADS_PER_GROUP + 2 * pr
                lanes0 = h0 * SSD_HEAD_DIM
                x2 = xc[:, lanes0:lanes0 + V7X_LANES]
                acol = [_lane_bcast(a_cum, col0 + h0 + j) for j in range(2)]
                dcol = [_lane_bcast(dtc, col0 + h0 + j) for j in range(2)]
                a2 = jnp.where(left, acol[0], acol[1])
                xd2 = x2 * jnp.where(left, dcol[0], dcol[1])
                xd2b = xd2.astype(BF16)
                y_pair = []
                for j in range(2):
                    arow = a_cum_t[col0 + h0 + j:col0 + h0 + j + 1, :]
                    lmat = jnp.exp2(jnp.minimum(acol[j] - arow, 0.0))
                    y_pair.append(_dot((cb * lmat).astype(BF16), xd2b))
                y2 = jnp.where(left, y_pair[0], y_pair[1])
                y2 = y2 + jnp.exp2(a2) * y_off[:, pr * V7X_LANES:(pr + 1) * V7X_LANES]
                if not reverse:
                    y2 = y2 + dskip_ref[:, lanes0:lanes0 + V7X_LANES] * x2
                y_ref[0, r0:r0 + CHUNK, lanes0:lanes0 + V7X_LANES] = y2
                a_tot = a2[last:last + 1, :]
                xdd_pairs.append((xd2 * jnp.exp2(a_tot - a2)).astype(BF16))
                tot.append(a_tot)
            xdd = jnp.concatenate(xdd_pairs, axis=1)
            st_sc[g] = jnp.exp2(jnp.concatenate(tot, axis=1)) * st_prev + _dot_tn(bmb, xdd)


def _ret_head(hd, q_ref, k_ref, v_ref, ld_ref, y_ref, st_sc, *, reverse):
    n = RET_SCAN_CHUNK
    tri = _tri(reverse, n)
    r = lax.broadcasted_iota(jnp.int32, (n, n), 0)
    c = lax.broadcasted_iota(jnp.int32, (n, n), 1)
    dist = jnp.maximum((c - r) if reverse else (r - c), 0).astype(F32)
    rk = lax.broadcasted_iota(jnp.int32, (n, RET_DK), 0)
    pos = ((n - 1 - rk) if reverse else rk).astype(F32)
    col0 = RET_HEADS if reverse else 0

    c0 = hd * RET_DK
    log_g = -jnp.exp(ld_ref[0:1, col0 + hd:col0 + hd + 1])
    q = q_ref[0, :, c0:c0 + RET_DK]
    k = k_ref[0, :, c0:c0 + RET_DK]
    v = v_ref[0, :, c0:c0 + RET_DV]
    st_prev = st_sc[hd]
    sc = _dot_nt(q, k) * jnp.where(tri, jnp.exp(dist * jnp.broadcast_to(log_g, (n, n))), 0.0)
    lg = jnp.broadcast_to(log_g, (n, RET_DK))
    q_dec = (q.astype(F32) * jnp.exp((pos + 1.0) * lg)).astype(BF16)
    y_ref[0, :, c0:c0 + RET_DV] = _dot(jnp.concatenate([sc.astype(BF16), q_dec], axis=1),
                                       jnp.concatenate([v, st_prev.astype(BF16)], axis=0))
    k_dec = (k.astype(F32) * jnp.exp((n - 1.0 - pos) * lg)).astype(BF16)
    st_sc[hd] = jnp.exp(n * jnp.broadcast_to(log_g, (RET_DK, RET_DV))) * st_prev + _dot_tn(k_dec, v)


def _scan_kernel(xf_ref, xpf_ref, xnf_ref, dtf_ref, xb_ref, xpb_ref, xnb_ref, dtb_ref, cw_ref, cb_ref, alog_ref,
                 dskip_ref, qf_ref, kf_ref, vf_ref, qb_ref, kb_ref, vb_ref, ld_ref,
                 sf_ref, sb_ref, rf_ref, rb_ref, ssf_sc, ssb_sc, rsf_sc, rsb_sc, *, nl):
    s = pl.program_id(1)

    @pl.when(s == 0)
    def _():
        for sc in (ssf_sc, ssb_sc, rsf_sc, rsb_sc):
            sc[...] = jnp.zeros(sc.shape, F32)

    def ret_heads(heads):
        for hd in heads:
            _ret_head(hd, qf_ref, kf_ref, vf_ref, ld_ref, rf_ref, rsf_sc, reverse=False)
            _ret_head(hd, qb_ref, kb_ref, vb_ref, ld_ref, rb_ref, rsb_sc, reverse=True)

    _ssd_tile(_scan_tile(s, nl, False), xf_ref, xpf_ref, xnf_ref, dtf_ref, cw_ref, cb_ref, alog_ref, dskip_ref,
              sf_ref, ssf_sc, nl=nl, reverse=False)
    ret_heads(range(0, RET_HEADS // 2))
    _ssd_tile(_scan_tile(s, nl, True), xb_ref, xpb_ref, xnb_ref, dtb_ref, cw_ref, cb_ref, alog_ref, dskip_ref,
              sb_ref, ssb_sc, nl=nl, reverse=True)
    ret_heads(range(RET_HEADS // 2, RET_HEADS))


def _scan_call(xbc, dt, cw, cb, alog, dskip, rq, rk, rv, ld, nl):
    b, l, _ = xbc.shape
    nt = l // TILE
    rows8 = TILE // 8
    const = lambda shape: pl.BlockSpec(shape, lambda bi, s: (0,) * len(shape))
    tok = lambda c, reverse: pl.BlockSpec((1, TILE, c), lambda bi, s: (bi, _scan_tile(s, nl, reverse), 0))

    def ssd_stream(reverse):
        tile_of = lambda s: _scan_tile(s, nl, reverse)
        return [tok(SSD_CONV_DIM, reverse),
                pl.BlockSpec((1, 8, SSD_CONV_DIM), lambda bi, s: (bi, jnp.maximum(tile_of(s) * rows8 - 1, 0), 0)),
                pl.BlockSpec((1, 8, SSD_CONV_DIM),
                             lambda bi, s: (bi, jnp.minimum((tile_of(s) + 1) * rows8, nt * rows8 - 1), 0)),
                tok(DT_PAD, reverse)]

    ssd_state = pltpu.VMEM((SSD_GROUPS, SSD_STATE, SSD_HEADS_PER_GROUP * SSD_HEAD_DIM), F32)
    ret_state = pltpu.VMEM((RET_HEADS, RET_DK, RET_DV), F32)
    ys = jax.ShapeDtypeStruct((b, l, SSD_D_INNER), F32)
    yr = jax.ShapeDtypeStruct((b, l, RET_DIM), F32)
    return pl.pallas_call(
        functools.partial(_scan_kernel, nl=nl),
        out_shape=(ys, ys, yr, yr),
        grid=(b, nt),
        in_specs=(ssd_stream(False) + ssd_stream(True)
                  + [const(cw.shape), const(cb.shape), const(alog.shape), const(dskip.shape)]
                  + [tok(RET_DIM, False)] * 3 + [tok(RET_DIM, True)] * 3 + [const(ld.shape)]),
        out_specs=(tok(SSD_D_INNER, False), tok(SSD_D_INNER, True), tok(RET_DIM, False), tok(RET_DIM, True)),
        scratch_shapes=[ssd_state, ssd_state, ret_state, ret_state],
        compiler_params=pltpu.CompilerParams(dimension_semantics=("arbitrary", "arbitrary")),
        name="scans",
    )(xbc, xbc, xbc, dt, xbc, xbc, xbc, dt, cw, cb, alog, dskip, rq, rk, rv, rq, rk, rv, ld)


def _merge_kernel(h_ref, mod_ref, n1w_ref, ot_ref, sf_ref, sb_ref, rf_ref, rb_ref, wzg_ref, wgate_ref,
                  snw_ref, gnw_ref, wb_ref, wout_ref, o_ref):
    h = h_ref[0]
    u = _rms_rows(h, n1w_ref[...]) * (1.0 + mod_ref[0, 1:2, :]) + mod_ref[0, 0:1, :]
    ub = u.astype(BF16)
    zg = _dot(ub, wzg_ref[...])
    gates = _sigmoid(_dot(ub, wgate_ref[...]))

    br_attn = _dot_tn(ot_ref[0], wb_ref[0])

    y = (sf_ref[0] + sb_ref[0]) * _silu(zg[:, :SSD_D_INNER])
    br_ssd = _dot(_rms_rows(y, snw_ref[...]).astype(BF16), wb_ref[1])

    yr = rf_ref[0] + rb_ref[0]
    heads = []
    for hd in range(RET_HEADS):
        yh = yr[:, hd * RET_DV:(hd + 1) * RET_DV]
        yc = yh - jnp.mean(yh, axis=-1, keepdims=True)
        heads.append(yc * lax.rsqrt(jnp.mean(yc * yc, axis=-1, keepdims=True) + NORM_EPS))
    yn = jnp.concatenate(heads, axis=1) * gnw_ref[...] * _silu(zg[:, SSD_D_INNER:])
    br_ret = _dot(yn.astype(BF16), wb_ref[2])

    merged = (gates[:, :D_MODEL] * br_attn + gates[:, D_MODEL:2 * D_MODEL] * br_ssd
              + gates[:, 2 * D_MODEL:] * br_ret)
    o_ref[0] = h + mod_ref[0, 2:3, :] * _dot(merged.astype(BF16), wout_ref[...])


def _merge_call(h, mods, n1w, ot, sf, sb, rf, rb, wts, nl):
    b, l, _ = h.shape
    nt, ctx_i = _tok_tiles(l, nl)
    wzg, wgate, snw, gnw, wb, wout = wts
    const = _resident
    tok = lambda c: pl.BlockSpec((1, TOK_TILE, c), lambda bi, i: (bi, i, 0))
    weights_bytes = 2 * (wzg.size + wgate.size + wb.size + wout.size)
    return pl.pallas_call(
        _merge_kernel,
        out_shape=jax.ShapeDtypeStruct((b, l, D_MODEL), F32),
        grid=(b, nt),
        in_specs=[tok(D_MODEL),
                  pl.BlockSpec((1, 6, D_MODEL), lambda bi, i: (jnp.where(i == ctx_i, 4, bi), 0, 0)),
                  const((1, D_MODEL)),
                  pl.BlockSpec((1, ATTN_Q_DIM, TOK_TILE), lambda bi, i: (bi, 0, i)),
                  tok(SSD_D_INNER), tok(SSD_D_INNER), tok(RET_DIM), tok(RET_DIM),
                  const(wzg.shape), const(wgate.shape), const(snw.shape), const(gnw.shape),
                  const(wb.shape), const(wout.shape)],
        out_specs=tok(D_MODEL),
        compiler_params=pltpu.CompilerParams(dimension_semantics=("arbitrary", "arbitrary"),
                                             vmem_limit_bytes=_vmem_limit(weights_bytes)),
        name="merge",
    )(h, mods, n1w, ot, sf, sb, rf, rb, wzg, wgate, snw, gnw, wb, wout)


def _mlp_kernel(h_ref, mod_ref, n2w_ref, w1_ref, w2_ref, fw_ref, o_ref, *, final):
    h = h_ref[0]
    v = _rms_rows(h, n2w_ref[...]) * (1.0 + mod_ref[0, 4:5, :]) + mod_ref[0, 3:4, :]
    a = jnp.maximum(_dot(v.astype(BF16), w1_ref[...]), 0.0)
    out = h + mod_ref[0, 5:6, :] * _dot((a * a).astype(BF16), w2_ref[...])
    o_ref[0] = _rms_rows(out, fw_ref[...]) if final else out


def _mlp_call(h, mods, n2w, w1, w2, fw, nl, final):
    b, l, _ = h.shape
    nt, ctx_i = _tok_tiles(l, nl)
    if final:
        nt, l = ctx_i, nl * TILE
    const = _resident
    tok = pl.BlockSpec((1, TOK_TILE, D_MODEL), lambda bi, i: (bi, i, 0))
    return pl.pallas_call(
        functools.partial(_mlp_kernel, final=final),
        out_shape=jax.ShapeDtypeStruct((b, l, D_MODEL), F32),
        grid=(b, nt),
        in_specs=[tok,
                  pl.BlockSpec((1, 6, D_MODEL), lambda bi, i: (jnp.where(i == ctx_i, 4, bi), 0, 0)),
                  const((1, D_MODEL)), const(w1.shape), const(w2.shape), const((1, D_MODEL))],
        out_specs=tok,
        compiler_params=pltpu.CompilerParams(dimension_semantics=("arbitrary", "arbitrary"),
                                             vmem_limit_bytes=_vmem_limit(2 * (w1.size + w2.size))),
        name="mlp_final" if final else "mlp",
    )(h, mods, n2w, w1, w2, fw)


def _rope_tables(n, m):
    rows = n // GRID_W
    row = jnp.repeat(jnp.arange(rows, dtype=F32), GRID_W)
    col = jnp.tile(jnp.arange(GRID_W, dtype=F32), rows)
    inv = ROPE_THETA ** (-jnp.arange(ATTN_AXIS_FREQS, dtype=F32) / ATTN_AXIS_FREQS)
    ang = jnp.concatenate([row[:, None] * inv, col[:, None] * inv], axis=-1)
    cos_a = jnp.concatenate([jnp.cos(ang), jnp.ones((m, ATTN_HALF), F32)], axis=0).T
    sin_a = jnp.concatenate([jnp.sin(ang), jnp.zeros((m, ATTN_HALF), F32)], axis=0).T
    pos = jnp.concatenate([jnp.arange(n, dtype=F32) + m, jnp.arange(m, dtype=F32)])
    inv_r = ROPE_THETA ** (-jnp.linspace(0.0, 1.0, RET_DK // 2, dtype=F32))
    ang_r = pos[:, None] * inv_r
    cos_r = jnp.concatenate([jnp.cos(ang_r), jnp.cos(ang_r)], axis=-1)
    sin_r = jnp.concatenate([-jnp.sin(ang_r), jnp.sin(ang_r)], axis=-1)
    return cos_a, sin_a, cos_r, sin_r


def _layer_weights(w_in, q_norm, k_norm, dt_bias):
    offs = [0]
    for sz in IN_SPLITS:
        offs.append(offs[-1] + sz)
    col = lambda j: w_in[:, offs[j]:offs[j + 1]]
    wqkv = jnp.concatenate([col(0), col(1), col(2)], axis=1).T.astype(BF16)
    qkw = jnp.concatenate([jnp.tile(q_norm, ATTN_HEADS), jnp.tile(k_norm, ATTN_KV_HEADS)])
    qkw = jnp.broadcast_to(qkw[:, None], (ATTN_Q_DIM + ATTN_KV_DIM, TOK_TILE)).astype(F32)
    wxbc = col(4).astype(BF16)
    wdt = jnp.pad(col(5), ((0, 0), (0, DT_PAD - 2 * SSD_HEADS))).astype(BF16)
    dtb = jnp.pad(dt_bias.reshape(1, -1), ((0, 0), (0, DT_PAD - 2 * SSD_HEADS))).astype(F32)
    wr = jnp.concatenate([col(6), col(7), col(8)], axis=1).astype(BF16)
    wzg = jnp.concatenate([col(3), col(9)], axis=1).astype(BF16)
    wgate = col(10).astype(BF16)
    return (wqkv, qkw, wxbc, wdt, dtb, wr), (wzg, wgate)


def _pad_lanes(v):
    v = v.reshape(1, -1).astype(F32)
    return jnp.pad(v, ((0, 0), (0, V7X_LANES - v.shape[1])))


def kernel(x, c, ctx, c_ctx, w_mod, b_mod, norm1_w, norm2_w, w_in, attn_q_norm, attn_k_norm, ssd_conv_w,
           ssd_conv_b, ssd_dt_bias, ssd_a_log, ssd_d, ssd_norm_w, ret_log_decay, ret_gn_w, w_branch, w_out,
           w_mlp1, w_mlp2, final_norm_w):
    b, n, d = x.shape
    m = ctx.shape[1]
    depth = w_in.shape[0]
    assert d == D_MODEL and m == TILE and n % TILE == 0 and n % GRID_W == 0 and b <= 4
    nl = n // TILE

    tabs = _rope_tables(n, m)
    cc = jnp.zeros((8, D_MODEL), F32).at[:b].set(c).at[4].set(c_ctx)
    h = jnp.concatenate([x, ctx], axis=1)

    for layer in range(depth):
        final = layer == depth - 1
        in_w, (wzg, wgate) = _layer_weights(w_in[layer], attn_q_norm[layer], attn_k_norm[layer],
                                            ssd_dt_bias[layer])
        n1w = norm1_w[layer].reshape(1, -1)
        mods = _mod_call(cc, w_mod[layer], b_mod[layer])

        qt, k, vt, xbc, dt, rq, rk, rv = _inproj_call(h, mods, n1w, in_w, tabs, nl)
        ot = _attn_call(qt, k, vt, nl)

        cw = jnp.pad(ssd_conv_w[layer], ((0, 8 - SSD_CONV_K), (0, 0)))
        cb = ssd_conv_b[layer].reshape(1, -1)
        alog = _pad_lanes(ssd_a_log[layer])
        dskip = jnp.repeat(ssd_d[layer], SSD_HEAD_DIM).reshape(1, -1)
        ld = _pad_lanes(ret_log_decay[layer])
        sf, sb, rf, rb = _scan_call(xbc, dt, cw, cb, alog, dskip, rq, rk, rv, ld, nl)

        merge_w = (wzg, wgate, ssd_norm_w[layer].reshape(1, -1), ret_gn_w[layer].reshape(1, -1),
                   w_branch[layer].astype(BF16), w_out[layer].astype(BF16))
        h = _merge_call(h, mods, n1w, ot, sf, sb, rf, rb, merge_w, nl)
        h = _mlp_call(h, mods, norm2_w[layer].reshape(1, -1), w_mlp1[layer].astype(BF16),
                      w_mlp2[layer].astype(BF16), final_norm_w.reshape(1, -1), nl, final)
    return h
```

```python
import functools
import math

import jax
import jax.numpy as jnp
from jax import lax
from jax.experimental import pallas as pl
from jax.experimental.pallas import tpu as pltpu

F32 = jnp.float32
BF16 = jnp.bfloat16

D_MODEL = 1024
GRID_W = 64
NORM_EPS = 1e-6
ROPE_THETA = 10000.0

ATTN_HEADS = 8
ATTN_KV_HEADS = 2
ATTN_GROUP = ATTN_HEADS // ATTN_KV_HEADS
ATTN_HEAD_DIM = 64
ATTN_HALF = ATTN_HEAD_DIM // 2
ATTN_AXIS_FREQS = ATTN_HEAD_DIM // 4
ATTN_Q_DIM = ATTN_HEADS * ATTN_HEAD_DIM
ATTN_KV_DIM = ATTN_KV_HEADS * ATTN_HEAD_DIM

SSD_HEADS = 8
SSD_HEAD_DIM = 64
SSD_D_INNER = SSD_HEADS * SSD_HEAD_DIM
SSD_GROUPS = 2
SSD_STATE = 128
SSD_CONV_K = 3
SSD_CONV_DIM = SSD_D_INNER + 2 * SSD_GROUPS * SSD_STATE
SSD_HEADS_PER_GROUP = SSD_HEADS // SSD_GROUPS

RET_HEADS = 4
RET_DK = 128
RET_DV = 128
RET_DIM = RET_HEADS * RET_DK

N_BRANCH = 3
BRANCH_W = 512
MLP_HIDDEN = 4 * D_MODEL

IN_SPLITS = (ATTN_Q_DIM, ATTN_KV_DIM, ATTN_KV_DIM, SSD_D_INNER, SSD_CONV_DIM, 2 * SSD_HEADS,
             RET_DIM, RET_DIM, RET_HEADS * RET_DV, RET_HEADS * RET_DV, N_BRANCH * D_MODEL)

V7X_LANES = 128
V7X_VMEM_BYTES = 64 * 1024 * 1024

TILE = 256
CHUNK = 128
RET_SCAN_CHUNK = TILE
TOK_TILE = 512
DT_PAD = V7X_LANES
NEG_BIG = -1e30
ATTN_KEY_CHUNK = 768
ATTN_Q_TILE = 1024
ATTN_QK_TILE = 256
ATTN_PV_TILE = 256
ATTN_ONES_ROWS = 16


def _resident(shape):
    return pl.BlockSpec(shape, lambda bi, i: (0,) * len(shape), pipeline_mode=pl.Buffered(1))


def _tok_tiles(l, nl):
    assert (nl * TILE) % TOK_TILE == 0
    return pl.cdiv(l, TOK_TILE), nl * TILE // TOK_TILE


def _vmem_limit(resident_bytes):
    return int(min(V7X_VMEM_BYTES - 8 * 1024 * 1024, 2 * resident_bytes + 16 * 1024 * 1024))


def _sigmoid(x):
    return 0.5 * jnp.tanh(0.5 * x) + 0.5


def _silu(x):
    h = 0.5 * x
    return h + h * jnp.tanh(h)


def _rms_rows(x, w):
    return x * lax.rsqrt(jnp.mean(x * x, axis=-1, keepdims=True) + NORM_EPS) * w


def _dot(a, b):
    return jnp.dot(a, b, preferred_element_type=F32)


def _dot_nt(a, b):
    return lax.dot_general(a, b, (((1,), (1,)), ((), ())), preferred_element_type=F32)


def _dot_tn(a, b):
    return lax.dot_general(a, b, (((0,), (0,)), ((), ())), preferred_element_type=F32)


def _split3(x):
    hi = x.astype(BF16)
    r1 = x - hi.astype(F32)
    mid = r1.astype(BF16)
    lo = (r1 - mid.astype(F32)).astype(BF16)
    return hi, mid, lo


def _mod_kernel(c_ref, w_ref, b_ref, o_ref):
    s = _silu(c_ref[...])
    o_ref[...] = jnp.dot(s, w_ref[...], preferred_element_type=F32,
                         precision=lax.Precision.HIGHEST) + b_ref[...]


def _mod_call(cc, w_mod, b_mod):
    nblk = w_mod.shape[1] // D_MODEL
    out = pl.pallas_call(
        _mod_kernel,
        out_shape=jax.ShapeDtypeStruct((8, nblk * D_MODEL), F32),
        grid=(nblk,),
        in_specs=[pl.BlockSpec((8, D_MODEL), lambda j: (0, 0)),
                  pl.BlockSpec((D_MODEL, D_MODEL), lambda j: (0, j)),
                  pl.BlockSpec((1, D_MODEL), lambda j: (0, j))],
        out_specs=pl.BlockSpec((8, D_MODEL), lambda j: (0, j)),
        compiler_params=pltpu.CompilerParams(dimension_semantics=("arbitrary",)),
        name="mod",
    )(cc, w_mod, b_mod.reshape(1, -1))
    return out.reshape(8, nblk, D_MODEL)


def _inproj_kernel(h_ref, mod_ref, n1w_ref, wqkv_ref, qkw_ref, cos_ref, sin_ref, wxbc_ref, wdt_ref,
                   dtb_ref, wr_ref, cosr_ref, sinr_ref,
                   qt_ref, k_ref, vt_ref, xbc_ref, dt_ref, rq_ref, rk_ref, rv_ref):
    h = h_ref[0]
    u = _rms_rows(h, n1w_ref[...]) * (1.0 + mod_ref[0, 1:2, :]) + mod_ref[0, 0:1, :]
    ub = u.astype(BF16)

    qkv_t = _dot_nt(wqkv_ref[...], ub)
    cos = cos_ref[...]
    sin = sin_ref[...]
    k_rows = []
    for hd in range(ATTN_HEADS + ATTN_KV_HEADS):
        r0 = hd * ATTN_HEAD_DIM
        xh = qkv_t[r0:r0 + ATTN_HEAD_DIM]
        yh = xh * lax.rsqrt(jnp.mean(xh * xh, axis=0, keepdims=True) + NORM_EPS) * qkw_ref[r0:r0 + ATTN_HEAD_DIM, :]
        y1 = yh[:ATTN_HALF]
        y2 = yh[ATTN_HALF:]
        o1 = y1 * cos - y2 * sin
        o2 = y1 * sin + y2 * cos
        if hd < ATTN_HEADS:
            scale = ATTN_HEAD_DIM ** -0.5 * math.log2(math.e)
            qt_ref[0, r0:r0 + ATTN_HALF, :] = (o1 * scale).astype(BF16)
            qt_ref[0, r0 + ATTN_HALF:r0 + ATTN_HEAD_DIM, :] = (o2 * scale).astype(BF16)
        else:
            k_rows += [o1, o2]
    k_t = jnp.concatenate(k_rows, axis=0)
    k_ref[0] = jnp.transpose(k_t).astype(BF16)
    vt_ref[0] = qkv_t[ATTN_Q_DIM + ATTN_KV_DIM:].astype(BF16)

    xbc_ref[0] = _dot(ub, wxbc_ref[...])
    dt_raw = _dot(ub, wdt_ref[...]) + dtb_ref[...]
    dt_ref[0] = jnp.maximum(dt_raw, 0.0) + jnp.log1p(jnp.exp(-jnp.abs(dt_raw)))

    r = _dot(ub, wr_ref[...])
    cosr = cosr_ref[...]
    sinr = sinr_ref[...]
    for hd in range(RET_HEADS):
        c0 = hd * RET_DK
        qh = r[:, c0:c0 + RET_DK]
        kh = r[:, RET_DIM + c0:RET_DIM + c0 + RET_DK]
        rq_ref[0, :, c0:c0 + RET_DK] = (qh * cosr + pltpu.roll(qh, RET_DK // 2, 1) * sinr).astype(BF16)
        rk_ref[0, :, c0:c0 + RET_DK] = ((kh * cosr + pltpu.roll(kh, RET_DK // 2, 1) * sinr)
                                        * (RET_DK ** -0.5)).astype(BF16)
    rv_ref[0] = r[:, 2 * RET_DIM:].astype(BF16)


def _inproj_call(h, mods, n1w, wts, tabs, nl):
    b, l, _ = h.shape
    nt, ctx_i = _tok_tiles(l, nl)
    wqkv, qkw, wxbc, wdt, dtb, wr = wts
    cos_a, sin_a, cos_r, sin_r = tabs
    const = _resident
    tok = lambda c: pl.BlockSpec((1, TOK_TILE, c), lambda bi, i: (bi, i, 0))
    tok_t = lambda r: pl.BlockSpec((1, r, TOK_TILE), lambda bi, i: (bi, 0, i))
    weights_bytes = 2 * (wqkv.size + wxbc.size + wdt.size + wr.size) + 4 * qkw.size
    return pl.pallas_call(
        _inproj_kernel,
        out_shape=(jax.ShapeDtypeStruct((b, ATTN_Q_DIM, l), BF16),
                   jax.ShapeDtypeStruct((b, l, ATTN_KV_DIM), BF16),
                   jax.ShapeDtypeStruct((b, ATTN_KV_DIM, l), BF16),
                   jax.ShapeDtypeStruct((b, l, SSD_CONV_DIM), F32),
                   jax.ShapeDtypeStruct((b, l, DT_PAD), F32),
                   jax.ShapeDtypeStruct((b, l, RET_DIM), BF16),
                   jax.ShapeDtypeStruct((b, l, RET_DIM), BF16),
                   jax.ShapeDtypeStruct((b, l, RET_DIM), BF16)),
        grid=(b, nt),
        in_specs=[tok(D_MODEL),
                  pl.BlockSpec((1, 6, D_MODEL), lambda bi, i: (jnp.where(i == ctx_i, 4, bi), 0, 0)),
                  const((1, D_MODEL)),
                  const(wqkv.shape), const(qkw.shape),
                  pl.BlockSpec((ATTN_HALF, TOK_TILE), lambda bi, i: (0, i)),
                  pl.BlockSpec((ATTN_HALF, TOK_TILE), lambda bi, i: (0, i)),
                  const(wxbc.shape), const(wdt.shape), const(dtb.shape), const(wr.shape),
                  pl.BlockSpec((TOK_TILE, RET_DK), lambda bi, i: (i, 0)),
                  pl.BlockSpec((TOK_TILE, RET_DK), lambda bi, i: (i, 0))],
        out_specs=(tok_t(ATTN_Q_DIM), tok(ATTN_KV_DIM), tok_t(ATTN_KV_DIM), tok(SSD_CONV_DIM), tok(DT_PAD),
                   tok(RET_DIM), tok(RET_DIM), tok(RET_DIM)),
        compiler_params=pltpu.CompilerParams(dimension_semantics=("arbitrary", "arbitrary"),
                                             vmem_limit_bytes=_vmem_limit(weights_bytes)),
        name="inproj",
    )(h, mods, n1w, wqkv, qkw, cos_a, sin_a, wxbc, wdt, dtb, wr, cos_r, sin_r)


def _attn_kernel(qt_ref, k_ref, vt_ref, o_ref, qpad_sc, m_sc, acc_sc, s0_sc, s1_sc, mx0_sc, mx1_sc, *, nl, kc):
    kv = pl.program_id(1)
    i = pl.program_id(2)
    ctx_i = nl * TILE // ATTN_Q_TILE

    row = lax.broadcasted_iota(jnp.int32, (2 * ATTN_HEAD_DIM, ATTN_Q_TILE), 0)
    keep = (row >= ATTN_HEAD_DIM).astype(jnp.int32) == kv
    for g in range(ATTN_GROUP):
        qg = qt_ref[0, g * ATTN_HEAD_DIM:(g + 1) * ATTN_HEAD_DIM, :]
        qq = jnp.concatenate([qg, qg], axis=0)
        qpad_sc[:, g * ATTN_Q_TILE:(g + 1) * ATTN_Q_TILE] = jnp.where(keep, qq, jnp.zeros_like(qq))
    m_sc[...] = jnp.full(m_sc.shape, NEG_BIG, F32)
    acc_sc[...] = jnp.zeros(acc_sc.shape, F32)

    wide = ATTN_GROUP * ATTN_Q_TILE

    def scores(off, size, c0, width):
        return _dot(k_ref[0, pl.ds(off, size), :], qpad_sc[:, c0:c0 + width])

    def accumulate(off, size, c0, s, mx):
        cols = slice(c0, c0 + ATTN_PV_TILE)
        m_old = m_sc[:, cols]
        m_new = jnp.maximum(m_old, mx)
        m_sc[:, cols] = m_new
        p = jnp.exp2(s - m_new).astype(BF16)
        v_aug = jnp.concatenate([vt_ref[0, :, pl.ds(off, size)], jnp.ones((ATTN_ONES_ROWS, size), BF16)], axis=0)
        acc_sc[:, cols] = jnp.exp2(m_old - m_new) * acc_sc[:, cols] + _dot(v_aug, p)

    def qk(off, size, c0, s_sc, mx_sc):
        s = scores(off, size, c0, ATTN_QK_TILE)
        s_sc[:size, c0:c0 + ATTN_QK_TILE] = s
        mx_sc[:, c0:c0 + ATTN_QK_TILE] = jnp.max(s, axis=0, keepdims=True)

    def pv(off, size, c0, s_sc, mx_sc):
        accumulate(off, size, c0, s_sc[:size, c0:c0 + ATTN_PV_TILE], mx_sc[:, c0:c0 + ATTN_PV_TILE])

    bufs = ((s0_sc, mx0_sc), (s1_sc, mx1_sc))

    def step(c, off_qk, off_pv):
        for q0 in range(0, wide, ATTN_QK_TILE):
            if c < len(sizes):
                qk(off_qk, sizes[c], q0, *bufs[c % 2])
            if c >= 1:
                for c0 in range(q0, q0 + ATTN_QK_TILE, ATTN_PV_TILE):
                    pv(off_pv, sizes[c - 1], c0, *bufs[(c - 1) % 2])

    n_mid, rest = divmod(nl * TILE, kc)
    sizes = [TILE] + [kc] * n_mid + ([rest] if rest else [])
    offs = [sum(sizes[:c]) for c in range(len(sizes) + 1)]
    steady = [c for c in range(1, len(sizes)) if sizes[c] == kc and sizes[c - 1] == kc]
    pairs = len(steady) // 2
    looped = steady[:2 * pairs]

    @pl.when(i != ctx_i)
    def _():
        for c in range(len(sizes) + 1):
            if c in looped:
                if c == looped[0]:
                    def body(j, carry):
                        for d in range(2):
                            off = pl.multiple_of(offs[looped[0] + d] + 2 * j * kc, TILE)
                            step(looped[0] + d, off, off - kc)
                        return carry
                    lax.fori_loop(0, pairs, body, 0)
            else:
                step(c, offs[c], offs[c - 1] if c >= 1 else None)

    @pl.when(i == ctx_i)
    def _():
        for g in range(ATTN_GROUP):
            for c0 in range(g * ATTN_Q_TILE, g * ATTN_Q_TILE + TILE, ATTN_PV_TILE):
                s = scores(nl * TILE, TILE, c0, ATTN_PV_TILE)
                accumulate(nl * TILE, TILE, c0, s, jnp.max(s, axis=0, keepdims=True))

    acc = acc_sc[...]
    out = acc[:ATTN_HEAD_DIM] / acc[ATTN_HEAD_DIM:ATTN_HEAD_DIM + 1]
    for g in range(ATTN_GROUP):
        o_ref[0, g * ATTN_HEAD_DIM:(g + 1) * ATTN_HEAD_DIM, :] = (
            out[:, g * ATTN_Q_TILE:(g + 1) * ATTN_Q_TILE].astype(BF16))


def _attn_call(qt, k, vt, nl):
    b, _, l = qt.shape
    assert (nl * TILE) % ATTN_Q_TILE == 0
    nt = pl.cdiv(l, ATTN_Q_TILE)
    kc = ATTN_KEY_CHUNK
    gq = ATTN_GROUP * ATTN_HEAD_DIM
    wide = ATTN_GROUP * ATTN_Q_TILE
    resident = 2 * (l * ATTN_KV_DIM + ATTN_HEAD_DIM * l) + 4 * kc * wide
    return pl.pallas_call(
        functools.partial(_attn_kernel, nl=nl, kc=kc),
        out_shape=jax.ShapeDtypeStruct((b, ATTN_Q_DIM, l), BF16),
        grid=(b, ATTN_KV_HEADS, nt),
        in_specs=[pl.BlockSpec((1, gq, ATTN_Q_TILE), lambda bi, kv, i: (bi, kv, i)),
                  pl.BlockSpec((1, l, ATTN_KV_DIM), lambda bi, kv, i: (bi, 0, 0)),
                  pl.BlockSpec((1, ATTN_HEAD_DIM, l), lambda bi, kv, i: (bi, kv, 0))],
        out_specs=pl.BlockSpec((1, gq, ATTN_Q_TILE), lambda bi, kv, i: (bi, kv, i)),
        scratch_shapes=[pltpu.VMEM((2 * ATTN_HEAD_DIM, wide), BF16),
                        pltpu.VMEM((1, wide), F32),
                        pltpu.VMEM((ATTN_HEAD_DIM + ATTN_ONES_ROWS, wide), F32),
                        pltpu.VMEM((kc, wide), F32), pltpu.VMEM((kc, wide), F32),
                        pltpu.VMEM((1, wide), F32), pltpu.VMEM((1, wide), F32)],
        compiler_params=pltpu.CompilerParams(dimension_semantics=("arbitrary",) * 3,
                                             vmem_limit_bytes=_vmem_limit(resident)),
        name="attn",
    )(qt, k, vt)


def _scan_tile(s, nl, reverse):
    return jnp.where(s == 0, nl, nl - s) if reverse else jnp.where(s == 0, nl, s - 1)


def _tri(reverse, n=CHUNK):
    r = lax.broadcasted_iota(jnp.int32, (n, n), 0)
    c = lax.broadcasted_iota(jnp.int32, (n, n), 1)
    return (c >= r) if reverse else (c <= r)


def _lane_bcast(x, c):
    return jnp.broadcast_to(x[:, c:c + 1], (x.shape[0], V7X_LANES))


def _ssd_tile(t, x_ref, xp_ref, xn_ref, dt_ref, cw_ref, cb_ref, alog_ref, dskip_ref, y_ref, st_sc,
              *, nl, reverse):
    x = x_ref[0]
    row = lax.broadcasted_iota(jnp.int32, (8, SSD_CONV_DIM), 0)
    has_prev = jnp.logical_and(t != nl, t != 0)
    has_next = t < nl - 1
    prev_row = jnp.where(has_prev, xp_ref[0, 7:8, :], 0.0)
    next_row = jnp.where(has_next, xn_ref[0, 0:1, :], 0.0)
    x_m1 = pltpu.roll(x, 1, 0)
    x_m1 = jnp.concatenate([jnp.where(row == 0, prev_row, x_m1[:8]), x_m1[8:]], axis=0)
    x_p1 = pltpu.roll(x, TILE - 1, 0)
    x_p1 = jnp.concatenate([x_p1[:TILE - 8], jnp.where(row == 7, next_row, x_p1[TILE - 8:])], axis=0)
    xs = _silu(cw_ref[0:1, :] * x_m1 + cw_ref[1:2, :] * x + cw_ref[2:3, :] * x_p1 + cb_ref[...])

    tri = _tri(reverse)
    tri_b = tri.astype(BF16)
    lane = lax.broadcasted_iota(jnp.int32, (CHUNK, V7X_LANES), 1)
    left = lane < SSD_HEAD_DIM
    col0 = SSD_HEADS if reverse else 0
    a_neg = -jnp.exp(alog_ref[...]) * math.log2(math.e)
    last = 0 if reverse else CHUNK - 1

    chunks = range(TILE // CHUNK)
    for ci in (reversed(chunks) if reverse else chunks):
        r0 = ci * CHUNK
        xc = xs[r0:r0 + CHUNK]
        dtc = dt_ref[0, r0:r0 + CHUNK, :]
        a = dtc * a_neg
        hi, mid, lo = _split3(a)
        a_cum = _dot(tri_b, hi) + _dot(tri_b, mid) + _dot(tri_b, lo)
        a_cum_t = jnp.transpose(a_cum)
        for g in range(SSD_GROUPS):
            bm = xc[:, SSD_D_INNER + g * SSD_STATE:SSD_D_INNER + (g + 1) * SSD_STATE]
            cm = xc[:, SSD_D_INNER + (SSD_GROUPS + g) * SSD_STATE:SSD_D_INNER + (SSD_GROUPS + g + 1) * SSD_STATE]
            bmb = bm.astype(BF16)
            cmb = cm.astype(BF16)
            cb = jnp.where(tri, _dot_nt(cmb, bmb), 0.0)
            st_prev = st_sc[g]
            y_off = _dot(cmb, st_prev.astype(BF16))
            xdd_pairs = []
            tot = []
            for pr in range(SSD_HEADS_PER_GROUP // 2):
                h0 = g * SSD_HEADS_PER_GROUP + 2 * pr
                lanes0 = h0 * SSD_HEAD_DIM
                x2 = xc[:, lanes0:lanes0 + V7X_LANES]
                acol = [_lane_bcast(a_cum, col0 + h0 + j) for j in range(2)]
                dcol = [_lane_bcast(dtc, col0 + h0 + j) for j in range(2)]
                a2 = jnp.where(left, acol[0], acol[1])
                xd2 = x2 * jnp.where(left, dcol[0], dcol[1])
                xd2b = xd2.astype(BF16)
                y_pair = []
                for j in range(2):
                    arow = a_cum_t[col0 + h0 + j:col0 + h0 + j + 1, :]
                    lmat = jnp.exp2(jnp.minimum(acol[j] - arow, 0.0))
                    y_pair.append(_dot((cb * lmat).astype(BF16), xd2b))
                y2 = jnp.where(left, y_pair[0], y_pair[1])
                y2 = y2 + jnp.exp2(a2) * y_off[:, pr * V7X_LANES:(pr + 1) * V7X_LANES]
                if not reverse:
                    y2 = y2 + dskip_ref[:, lanes0:lanes0 + V7X_LANES] * x2
                y_ref[0, r0:r0 + CHUNK, lanes0:lanes0 + V7X_LANES] = y2
                a_tot = a2[last:last + 1, :]
                xdd_pairs.append((xd2 * jnp.exp2(a_tot - a2)).astype(BF16))
                tot.append(a_tot)
            xdd = jnp.concatenate(xdd_pairs, axis=1)
            st_sc[g] = jnp.exp2(jnp.concatenate(tot, axis=1)) * st_prev + _dot_tn(bmb, xdd)


def _ret_head(hd, q_ref, k_ref, v_ref, ld_ref, y_ref, st_sc, *, reverse):
    n = RET_SCAN_CHUNK
    tri = _tri(reverse, n)
    r = lax.broadcasted_iota(jnp.int32, (n, n), 0)
    c = lax.broadcasted_iota(jnp.int32, (n, n), 1)
    dist = jnp.maximum((c - r) if reverse else (r - c), 0).astype(F32)
    rk = lax.broadcasted_iota(jnp.int32, (n, RET_DK), 0)
    pos = ((n - 1 - rk) if reverse else rk).astype(F32)
    col0 = RET_HEADS if reverse else 0

    c0 = hd * RET_DK
    log_g = -jnp.exp(ld_ref[0:1, col0 + hd:col0 + hd + 1]) * math.log2(math.e)
    q = q_ref[0, :, c0:c0 + RET_DK]
    k = k_ref[0, :, c0:c0 + RET_DK]
    v = v_ref[0, :, c0:c0 + RET_DV]
    st_prev = st_sc[hd]
    sc = _dot_nt(q, k) * jnp.where(tri, jnp.exp2(dist * jnp.broadcast_to(log_g, (n, n))), 0.0)
    lg = jnp.broadcast_to(log_g, (n, RET_DK))
    q_dec = (q.astype(F32) * jnp.exp2((pos + 1.0) * lg)).astype(BF16)
    y_ref[0, :, c0:c0 + RET_DV] = _dot(jnp.concatenate([sc.astype(BF16), q_dec], axis=1),
                                       jnp.concatenate([v, st_prev.astype(BF16)], axis=0))
    k_dec = (k.astype(F32) * jnp.exp2((n - 1.0 - pos) * lg)).astype(BF16)
    st_sc[hd] = jnp.exp2(n * jnp.broadcast_to(log_g, (RET_DK, RET_DV))) * st_prev + _dot_tn(k_dec, v)


def _scan_kernel(xf_ref, xpf_ref, xnf_ref, dtf_ref, xb_ref, xpb_ref, xnb_ref, dtb_ref, cw_ref, cb_ref, alog_ref,
                 dskip_ref, qf_ref, kf_ref, vf_ref, qb_ref, kb_ref, vb_ref, ld_ref,
                 sf_ref, sb_ref, rf_ref, rb_ref, ssf_sc, ssb_sc, rsf_sc, rsb_sc, *, nl):
    s = pl.program_id(1)

    @pl.when(s == 0)
    def _():
        for sc in (ssf_sc, ssb_sc, rsf_sc, rsb_sc):
            sc[...] = jnp.zeros(sc.shape, F32)

    def ret_heads(heads):
        for hd in heads:
            _ret_head(hd, qf_ref, kf_ref, vf_ref, ld_ref, rf_ref, rsf_sc, reverse=False)
            _ret_head(hd, qb_ref, kb_ref, vb_ref, ld_ref, rb_ref, rsb_sc, reverse=True)

    _ssd_tile(_scan_tile(s, nl, False), xf_ref, xpf_ref, xnf_ref, dtf_ref, cw_ref, cb_ref, alog_ref, dskip_ref,
              sf_ref, ssf_sc, nl=nl, reverse=False)
    ret_heads(range(0, RET_HEADS // 2))
    _ssd_tile(_scan_tile(s, nl, True), xb_ref, xpb_ref, xnb_ref, dtb_ref, cw_ref, cb_ref, alog_ref, dskip_ref,
              sb_ref, ssb_sc, nl=nl, reverse=True)
    ret_heads(range(RET_HEADS // 2, RET_HEADS))


def _scan_call(xbc, dt, cw, cb, alog, dskip, rq, rk, rv, ld, nl):
    b, l, _ = xbc.shape
    nt = l // TILE
    rows8 = TILE // 8
    const = lambda shape: pl.BlockSpec(shape, lambda bi, s: (0,) * len(shape))
    tok = lambda c, reverse: pl.BlockSpec((1, TILE, c), lambda bi, s: (bi, _scan_tile(s, nl, reverse), 0))

    def ssd_stream(reverse):
        tile_of = lambda s: _scan_tile(s, nl, reverse)
        return [tok(SSD_CONV_DIM, reverse),
                pl.BlockSpec((1, 8, SSD_CONV_DIM), lambda bi, s: (bi, jnp.maximum(tile_of(s) * rows8 - 1, 0), 0)),
                pl.BlockSpec((1, 8, SSD_CONV_DIM),
                             lambda bi, s: (bi, jnp.minimum((tile_of(s) + 1) * rows8, nt * rows8 - 1), 0)),
                tok(DT_PAD, reverse)]

    ssd_state = pltpu.VMEM((SSD_GROUPS, SSD_STATE, SSD_HEADS_PER_GROUP * SSD_HEAD_DIM), F32)
    ret_state = pltpu.VMEM((RET_HEADS, RET_DK, RET_DV), F32)
    ys = jax.ShapeDtypeStruct((b, l, SSD_D_INNER), F32)
    yr = jax.ShapeDtypeStruct((b, l, RET_DIM), F32)
    return pl.pallas_call(
        functools.partial(_scan_kernel, nl=nl),
        out_shape=(ys, ys, yr, yr),
        grid=(b, nt),
        in_specs=(ssd_stream(False) + ssd_stream(True)
                  + [const(cw.shape), const(cb.shape), const(alog.shape), const(dskip.shape)]
                  + [tok(RET_DIM, False)] * 3 + [tok(RET_DIM, True)] * 3 + [const(ld.shape)]),
        out_specs=(tok(SSD_D_INNER, False), tok(SSD_D_INNER, True), tok(RET_DIM, False), tok(RET_DIM, True)),
        scratch_shapes=[ssd_state, ssd_state, ret_state, ret_state],
        compiler_params=pltpu.CompilerParams(dimension_semantics=("arbitrary", "arbitrary")),
        name="scans",
    )(xbc, xbc, xbc, dt, xbc, xbc, xbc, dt, cw, cb, alog, dskip, rq, rk, rv, rq, rk, rv, ld)


def _merge_kernel(h_ref, mod_ref, n1w_ref, ot_ref, sf_ref, sb_ref, rf_ref, rb_ref, wzg_ref, wgate_ref,
                  snw_ref, gnw_ref, wb_ref, wout_ref, o_ref):
    h = h_ref[0]
    u = _rms_rows(h, n1w_ref[...]) * (1.0 + mod_ref[0, 1:2, :]) + mod_ref[0, 0:1, :]
    ub = u.astype(BF16)
    zg = _dot(ub, wzg_ref[...])
    gates = _sigmoid(_dot(ub, wgate_ref[...]))

    br_attn = _dot_tn(ot_ref[0], wb_ref[0])

    y = (sf_ref[0] + sb_ref[0]) * _silu(zg[:, :SSD_D_INNER])
    br_ssd = _dot(_rms_rows(y, snw_ref[...]).astype(BF16), wb_ref[1])

    yr = rf_ref[0] + rb_ref[0]
    heads = []
    for hd in range(RET_HEADS):
        yh = yr[:, hd * RET_DV:(hd + 1) * RET_DV]
        yc = yh - jnp.mean(yh, axis=-1, keepdims=True)
        heads.append(yc * lax.rsqrt(jnp.mean(yc * yc, axis=-1, keepdims=True) + NORM_EPS))
    yn = jnp.concatenate(heads, axis=1) * gnw_ref[...] * _silu(zg[:, SSD_D_INNER:])
    br_ret = _dot(yn.astype(BF16), wb_ref[2])

    merged = (gates[:, :D_MODEL] * br_attn + gates[:, D_MODEL:2 * D_MODEL] * br_ssd
              + gates[:, 2 * D_MODEL:] * br_ret)
    o_ref[0] = h + mod_ref[0, 2:3, :] * _dot(merged.astype(BF16), wout_ref[...])


def _merge_call(h, mods, n1w, ot, sf, sb, rf, rb, wts, nl):
    b, l, _ = h.shape
    nt, ctx_i = _tok_tiles(l, nl)
    wzg, wgate, snw, gnw, wb, wout = wts
    const = _resident
    tok = lambda c: pl.BlockSpec((1, TOK_TILE, c), lambda bi, i: (bi, i, 0))
    weights_bytes = 2 * (wzg.size + wgate.size + wb.size + wout.size)
    return pl.pallas_call(
        _merge_kernel,
        out_shape=jax.ShapeDtypeStruct((b, l, D_MODEL), F32),
        grid=(b, nt),
        in_specs=[tok(D_MODEL),
                  pl.BlockSpec((1, 6, D_MODEL), lambda bi, i: (jnp.where(i == ctx_i, 4, bi), 0, 0)),
                  const((1, D_MODEL)),
                  pl.BlockSpec((1, ATTN_Q_DIM, TOK_TILE), lambda bi, i: (bi, 0, i)),
                  tok(SSD_D_INNER), tok(SSD_D_INNER), tok(RET_DIM), tok(RET_DIM),
                  const(wzg.shape), const(wgate.shape), const(snw.shape), const(gnw.shape),
                  const(wb.shape), const(wout.shape)],
        out_specs=tok(D_MODEL),
        compiler_params=pltpu.CompilerParams(dimension_semantics=("arbitrary", "arbitrary"),
                                             vmem_limit_bytes=_vmem_limit(weights_bytes)),
        name="merge",
    )(h, mods, n1w, ot, sf, sb, rf, rb, wzg, wgate, snw, gnw, wb, wout)


def _mlp_kernel(h_ref, mod_ref, n2w_ref, w1_ref, w2_ref, fw_ref, o_ref, *, final):
    h = h_ref[0]
    v = _rms_rows(h, n2w_ref[...]) * (1.0 + mod_ref[0, 4:5, :]) + mod_ref[0, 3:4, :]
    a = jnp.maximum(_dot(v.astype(BF16), w1_ref[...]), 0.0)
    out = h + mod_ref[0, 5:6, :] * _dot((a * a).astype(BF16), w2_ref[...])
    o_ref[0] = _rms_rows(out, fw_ref[...]) if final else out


def _mlp_call(h, mods, n2w, w1, w2, fw, nl, final):
    b, l, _ = h.shape
    nt, ctx_i = _tok_tiles(l, nl)
    if final:
        nt, l = ctx_i, nl * TILE
    const = _resident
    tok = pl.BlockSpec((1, TOK_TILE, D_MODEL), lambda bi, i: (bi, i, 0))
    return pl.pallas_call(
        functools.partial(_mlp_kernel, final=final),
        out_shape=jax.ShapeDtypeStruct((b, l, D_MODEL), F32),
        grid=(b, nt),
        in_specs=[tok,
                  pl.BlockSpec((1, 6, D_MODEL), lambda bi, i: (jnp.where(i == ctx_i, 4, bi), 0, 0)),
                  const((1, D_MODEL)), const(w1.shape), const(w2.shape), const((1, D_MODEL))],
        out_specs=tok,
        compiler_params=pltpu.CompilerParams(dimension_semantics=("arbitrary", "arbitrary"),
                                             vmem_limit_bytes=_vmem_limit(2 * (w1.size + w2.size))),
        name="mlp_final" if final else "mlp",
    )(h, mods, n2w, w1, w2, fw)


def _rope_tables(n, m):
    rows = n // GRID_W
    row = jnp.repeat(jnp.arange(rows, dtype=F32), GRID_W)
    col = jnp.tile(jnp.arange(GRID_W, dtype=F32), rows)
    inv = ROPE_THETA ** (-jnp.arange(ATTN_AXIS_FREQS, dtype=F32) / ATTN_AXIS_FREQS)
    ang = jnp.concatenate([row[:, None] * inv, col[:, None] * inv], axis=-1)
    cos_a = jnp.concatenate([jnp.cos(ang), jnp.ones((m, ATTN_HALF), F32)], axis=0).T
    sin_a = jnp.concatenate([jnp.sin(ang), jnp.zeros((m, ATTN_HALF), F32)], axis=0).T
    pos = jnp.concatenate([jnp.arange(n, dtype=F32) + m, jnp.arange(m, dtype=F32)])
    inv_r = ROPE_THETA ** (-jnp.linspace(0.0, 1.0, RET_DK // 2, dtype=F32))
    ang_r = pos[:, None] * inv_r
    cos_r = jnp.concatenate([jnp.cos(ang_r), jnp.cos(ang_r)], axis=-1)
    sin_r = jnp.concatenate([-jnp.sin(ang_r), jnp.sin(ang_r)], axis=-1)
    return cos_a, sin_a, cos_r, sin_r


def _layer_weights(w_in, q_norm, k_norm, dt_bias):
    offs = [0]
    for sz in IN_SPLITS:
        offs.append(offs[-1] + sz)
    col = lambda j: w_in[:, offs[j]:offs[j + 1]]
    wqkv = jnp.concatenate([col(0), col(1), col(2)], axis=1).T.astype(BF16)
    qkw = jnp.concatenate([jnp.tile(q_norm, ATTN_HEADS), jnp.tile(k_norm, ATTN_KV_HEADS)])
    qkw = jnp.broadcast_to(qkw[:, None], (ATTN_Q_DIM + ATTN_KV_DIM, TOK_TILE)).astype(F32)
    wxbc = col(4).astype(BF16)
    wdt = jnp.pad(col(5), ((0, 0), (0, DT_PAD - 2 * SSD_HEADS))).astype(BF16)
    dtb = jnp.pad(dt_bias.reshape(1, -1), ((0, 0), (0, DT_PAD - 2 * SSD_HEADS))).astype(F32)
    wr = jnp.concatenate([col(6), col(7), col(8)], axis=1).astype(BF16)
    wzg = jnp.concatenate([col(3), col(9)], axis=1).astype(BF16)
    wgate = col(10).astype(BF16)
    return (wqkv, qkw, wxbc, wdt, dtb, wr), (wzg, wgate)


def _pad_lanes(v):
    v = v.reshape(1, -1).astype(F32)
    return jnp.pad(v, ((0, 0), (0, V7X_LANES - v.shape[1])))


def kernel(x, c, ctx, c_ctx, w_mod, b_mod, norm1_w, norm2_w, w_in, attn_q_norm, attn_k_norm, ssd_conv_w,
           ssd_conv_b, ssd_dt_bias, ssd_a_log, ssd_d, ssd_norm_w, ret_log_decay, ret_gn_w, w_branch, w_out,
           w_mlp1, w_mlp2, final_norm_w):
    b, n, d = x.shape
    m = ctx.shape[1]
    depth = w_in.shape[0]
    assert d == D_MODEL and m == TILE and n % TILE == 0 and n % GRID_W == 0 and b <= 4
    nl = n // TILE

    tabs = _rope_tables(n, m)
    cc = jnp.zeros((8, D_MODEL), F32).at[:b].set(c).at[4].set(c_ctx)
    h = jnp.concatenate([x, ctx], axis=1)

    for layer in range(depth):
        final = layer == depth - 1
        in_w, (wzg, wgate) = _layer_weights(w_in[layer], attn_q_norm[layer], attn_k_norm[layer],
                                            ssd_dt_bias[layer])
        n1w = norm1_w[layer].reshape(1, -1)
        mods = _mod_call(cc, w_mod[layer], b_mod[layer])

        qt, k, vt, xbc, dt, rq, rk, rv = _inproj_call(h, mods, n1w, in_w, tabs, nl)
        ot = _attn_call(qt, k, vt, nl)

        cw = jnp.pad(ssd_conv_w[layer], ((0, 8 - SSD_CONV_K), (0, 0)))
        cb = ssd_conv_b[layer].reshape(1, -1)
        alog = _pad_lanes(ssd_a_log[layer])
        dskip = jnp.repeat(ssd_d[layer], SSD_HEAD_DIM).reshape(1, -1)
        ld = _pad_lanes(ret_log_decay[layer])
        sf, sb, rf, rb = _scan_call(xbc, dt, cw, cb, alog, dskip, rq, rk, rv, ld, nl)

        merge_w = (wzg, wgate, ssd_norm_w[layer].reshape(1, -1), ret_gn_w[layer].reshape(1, -1),
                   w_branch[layer].astype(BF16), w_out[layer].astype(BF16))
        h = _merge_call(h, mods, n1w, ot, sf, sb, rf, rb, merge_w, nl)
        h = _mlp_call(h, mods, norm2_w[layer].reshape(1, -1), w_mlp1[layer].astype(BF16),
                      w_mlp2[layer].astype(BF16), final_norm_w.reshape(1, -1), nl, final)
    return h
```
